```python
import math
import jax, jax.numpy as jnp
from jax import lax
import numpy as np

D_MODEL = 1024
BATCH = 2
SEQ = 16384
DEPTH = 2

SGU_CHUNK = 128
SGU_GROUPS = 4
SGU_WIDTH = 512
SGU_GROUP_W = SGU_WIDTH // SGU_GROUPS
SSM_WIDTH = 512
SSM_GROUP = 16
SSM_GROUPS = SSM_WIDTH // SSM_GROUP
SSM_STATE = 64
DT_MIN = 0.001
DT_MAX = 0.1
ATT_HEADS = 8
HEAD_DIM = 64
ATT_WIDTH = ATT_HEADS * HEAD_DIM
MOBA_BLOCK = 256
MOBA_TOPK = 3
Q_CHUNK = 128
N_BRANCH = 3
IN_COLS = 2 * SGU_WIDTH + SSM_WIDTH + 3 * ATT_WIDTH + N_BRANCH * D_MODEL
D_FF = -(-8 * D_MODEL // (3 * 256)) * 256
DN_ALPHA = (2 * DEPTH) ** 0.25
DN_BETA = (8 * DEPTH) ** -0.25
LN_EPS = 1e-5
NEG_INF = -1e30

kernel_name = "gated_hybrid_gmlp_s5_moba_deepnorm"


def layer_norm(x, g, b):
    xf = x.astype(jnp.float32)
    mu = jnp.mean(xf, axis=-1, keepdims=True)
    var = jnp.mean(jnp.square(xf - mu), axis=-1, keepdims=True)
    return ((xf - mu) * lax.rsqrt(var + LN_EPS) * g.astype(jnp.float32) + b.astype(jnp.float32)).astype(x.dtype)


def spatial_gating(z, ln_g, ln_b, w_s, b_s):
    u, v = jnp.split(z, 2, axis=-1)
    v = layer_norm(v, ln_g, ln_b)
    bn, s, _ = v.shape
    v = v.reshape(bn, s // SGU_CHUNK, SGU_CHUNK, SGU_GROUPS, SGU_GROUP_W)
    w = jnp.tril(w_s)
    mixed = jnp.einsum('gts,bnsgc->bntgc', w, v) + b_s.T[:, :, None]
    return u * mixed.reshape(bn, s, SGU_WIDTH)


def s5_branch(u, lam_re, lam_im, log_dt, b_re, b_im, c_re, c_im, d, glu_w, glu_b):
    f32 = jnp.float32
    bn, s, _ = u.shape
    lam = lax.complex(lam_re.astype(f32), lam_im.astype(f32))
    dt = jnp.exp(log_dt.astype(f32))[:, None]
    lam_bar = jnp.exp(lam * dt)
    b_mat = lax.complex(b_re.astype(f32), b_im.astype(f32))
    b_bar = ((lam_bar - 1.0) / lam)[..., None] * b_mat
    uf = u.astype(f32)
    ug = uf.reshape(bn, s, SSM_GROUPS, SSM_GROUP).astype(jnp.complex64)
    bu = jnp.einsum('gph,bsgh->bsgp', b_bar, ug)
    a = jnp.broadcast_to(lam_bar, bu.shape)

    def combine(left, right):
        a_l, b_l = left
        a_r, b_r = right
        return a_r * a_l, a_r * b_l + b_r

    _, h = lax.associative_scan(combine, (a, bu), axis=1)
    c_mat = lax.complex(c_re.astype(f32), c_im.astype(f32))
    y = jnp.real(jnp.einsum('ghp,bsgp->bsgh', c_mat, h)).reshape(bn, s, SSM_WIDTH)
    y = jax.nn.gelu(y + d.astype(f32) * uf)
    y = y * jax.nn.sigmoid(y @ glu_w.astype(f32) + glu_b.astype(f32))
    return y.astype(u.dtype)


def moba_attention(q, k, v):
    f32 = jnp.float32
    bn, s = q.shape[0], q.shape[1]
    s_pad = -(-s // MOBA_BLOCK) * MOBA_BLOCK
    pad = ((0, 0), (0, s_pad - s), (0, 0), (0, 0))
    q = jnp.pad(q, pad).transpose(0, 2, 1, 3)
    k = jnp.pad(k, pad).transpose(0, 2, 1, 3)
    v = jnp.pad(v, pad).transpose(0, 2, 1, 3)
    nb = s_pad // MOBA_BLOCK
    n_sel = min(MOBA_TOPK, nb)
    kb = k.reshape(bn, ATT_HEADS, nb, MOBA_BLOCK, HEAD_DIM)
    vb = v.reshape(bn, ATT_HEADS, nb, MOBA_BLOCK, HEAD_DIM)
    k_mean = jnp.mean(kb.astype(f32), axis=3)
    pos = jnp.arange(s_pad, dtype=jnp.int32)
    q_blk = pos // MOBA_BLOCK
    gate = jnp.einsum('bhsd,bhnd->bhsn', q.astype(f32), k_mean)
    fully_past = jnp.arange(nb, dtype=jnp.int32)[None, :] < q_blk[:, None]
    gate = jnp.where(fully_past, gate, -jnp.inf)
    _, sel = lax.top_k(gate, n_sel)
    kb_flat = kb.reshape(bn * ATT_HEADS * nb, MOBA_BLOCK, HEAD_DIM)
    vb_flat = vb.reshape(bn * ATT_HEADS * nb, MOBA_BLOCK, HEAD_DIM)
    bh_off = (jnp.arange(bn * ATT_HEADS, dtype=jnp.int32) * nb).reshape(bn, ATT_HEADS, 1, 1)
    slopes = 2.0 ** (-8.0 * jnp.arange(1, ATT_HEADS + 1, dtype=f32) / ATT_HEADS)
    scale = HEAD_DIM ** -0.5
    n_chunks = s_pad // Q_CHUNK
    qc = q.reshape(bn, ATT_HEADS, n_chunks, Q_CHUNK, HEAD_DIM).transpose(2, 0, 1, 3, 4)
    selc = sel.reshape(bn, ATT_HEADS, n_chunks, Q_CHUNK, n_sel).transpose(2, 0, 1, 3, 4)
    chunk_ids = jnp.arange(n_chunks, dtype=jnp.int32)
    blk_ar = jnp.arange(MOBA_BLOCK, dtype=jnp.int32)

    def body(args):
        c, q_c, sel_c = args
        t = c * Q_CHUNK + jnp.arange(Q_CHUNK, dtype=jnp.int32)
        own = (c * Q_CHUNK) // MOBA_BLOCK
        kg = kb_flat[sel_c + bh_off]
        vg = vb_flat[sel_c + bh_off]
        s_g = jnp.einsum('bhqd,bhqkld->bhqkl', q_c, kg, preferred_element_type=f32) * scale
        key_pos_g = sel_c[..., None] * MOBA_BLOCK + blk_ar
        dist_g = jnp.abs(t[:, None, None] - key_pos_g).astype(f32)
        valid_g = jnp.arange(n_sel, dtype=jnp.int32)[None, :] < (t // MOBA_BLOCK)[:, None]
        s_g = jnp.where(valid_g[None, None, :, :, None], s_g - slopes[None, :, None, None, None] * dist_g, NEG_INF)
        k_own = lax.dynamic_index_in_dim(kb, own, axis=2, keepdims=False)
        v_own = lax.dynamic_index_in_dim(vb, own, axis=2, keepdims=False)
        s_o = jnp.einsum('bhqd,bhld->bhql', q_c, k_own, preferred_element_type=f32) * scale
        dist_o = t[:, None] - (own * MOBA_BLOCK + blk_ar)[None, :]
        s_o = jnp.where((dist_o >= 0)[None, None], s_o - slopes[None, :, None, None] * jnp.abs(dist_o).astype(f32), NEG_INF)
        scores = jnp.concatenate([s_g.reshape(bn, ATT_HEADS, Q_CHUNK, n_sel * MOBA_BLOCK), s_o], axis=-1)
        p = jax.nn.softmax(scores, axis=-1).astype(v.dtype)
        p_g = p[..., : n_sel * MOBA_BLOCK].reshape(bn, ATT_HEADS, Q_CHUNK, n_sel, MOBA_BLOCK)
        p_o = p[..., n_sel * MOBA_BLOCK:]
        out = (jnp.einsum('bhqkl,bhqkld->bhqd', p_g, vg, preferred_element_type=f32)
               + jnp.einsum('bhql,bhld->bhqd', p_o, v_own, preferred_element_type=f32))
        return out.astype(v.dtype)

    o = lax.map(body, (chunk_ids, qc, selc))
    o = o.transpose(1, 0, 3, 2, 4).reshape(bn, s_pad, ATT_WIDTH)
    return o[:, :s]


def hybrid_mixer(x, w_in, sgu_ln_g, sgu_ln_b, sgu_w, sgu_b, lam_re, lam_im, log_dt, b_re, b_im,
                 c_re, c_im, d, glu_w, glu_b, w_a, w_b, w_c, w_out):
    bn, s, _ = x.shape
    proj = x @ w_in
    splits = np.cumsum([2 * SGU_WIDTH, SSM_WIDTH, ATT_WIDTH, ATT_WIDTH, ATT_WIDTH]).tolist()
    z_sgu, z_ssm, q, k, v, gates = jnp.split(proj, splits, axis=-1)
    br_a = spatial_gating(jax.nn.gelu(z_sgu), sgu_ln_g, sgu_ln_b, sgu_w, sgu_b)
    br_b = s5_branch(z_ssm, lam_re, lam_im, log_dt, b_re, b_im, c_re, c_im, d, glu_w, glu_b)
    hd = (bn, s, ATT_HEADS, HEAD_DIM)
    br_c = moba_attention(q.reshape(hd), k.reshape(hd), v.reshape(hd))
    g = jax.nn.sigmoid(gates.astype(jnp.float32)).astype(x.dtype).reshape(bn, s, N_BRANCH, D_MODEL)
    merged = g[:, :, 0] * (br_a @ w_a) + g[:, :, 1] * (br_b @ w_b) + g[:, :, 2] * (br_c @ w_c)
    return merged @ w_out


def swiglu(x, w1, w3, w2):
    return (jax.nn.silu(x @ w1) * (x @ w3)) @ w2


def setup_inputs(seed: int = 0) -> dict:
    key = jax.random.key(seed)
    ks = jax.random.split(key, 32)
    L = DEPTH
    nrm = lambda k, shape, sc: jax.random.normal(k, shape, jnp.float32) * sc
    lam_im0 = jnp.pi * jnp.arange(SSM_STATE, dtype=jnp.float32)
    return {
        "x": nrm(ks[0], (BATCH, SEQ, D_MODEL), 1.0),
        "w_in": nrm(ks[1], (L, D_MODEL, IN_COLS), D_MODEL ** -0.5),
        "sgu_ln_g": 1.0 + nrm(ks[2], (L, SGU_WIDTH), 0.02),
        "sgu_ln_b": nrm(ks[3], (L, SGU_WIDTH), 0.02),
        "sgu_w": nrm(ks[4], (L, SGU_GROUPS, SGU_CHUNK, SGU_CHUNK), SGU_CHUNK ** -0.5),
        "sgu_b": 1.0 + nrm(ks[5], (L, SGU_GROUPS, SGU_CHUNK), 0.02),
        "ssm_lambda_re": -0.5 + nrm(ks[6], (L, SSM_GROUPS, SSM_STATE), 0.01),
        "ssm_lambda_im": lam_im0 + nrm(ks[7], (L, SSM_GROUPS, SSM_STATE), 0.01),
        "ssm_log_dt": jax.random.uniform(ks[8], (L, SSM_GROUPS), jnp.float32, math.log(DT_MIN), math.log(DT_MAX)),
        "ssm_b_re": nrm(ks[9], (L, SSM_GROUPS, SSM_STATE, SSM_GROUP), (2 * SSM_GROUP) ** -0.5),
        "ssm_b_im": nrm(ks[10], (L, SSM_GROUPS, SSM_STATE, SSM_GROUP), (2 * SSM_GROUP) ** -0.5),
        "ssm_c_re": nrm(ks[11], (L, SSM_GROUPS, SSM_GROUP, SSM_STATE), (2 * SSM_STATE) ** -0.5),
        "ssm_c_im": nrm(ks[12], (L, SSM_GROUPS, SSM_GROUP, SSM_STATE), (2 * SSM_STATE) ** -0.5),
        "ssm_d": nrm(ks[13], (L, SSM_WIDTH), 1.0),
        "glu_w": nrm(ks[14], (L, SSM_WIDTH, SSM_WIDTH), SSM_WIDTH ** -0.5),
        "glu_b": nrm(ks[15], (L, SSM_WIDTH), 0.02),
        "w_branch_a": nrm(ks[16], (L, SGU_WIDTH, D_MODEL), SGU_WIDTH ** -0.5),
        "w_branch_b": nrm(ks[17], (L, SSM_WIDTH, D_MODEL), SSM_WIDTH ** -0.5),
        "w_branch_c": nrm(ks[18], (L, ATT_WIDTH, D_MODEL), ATT_WIDTH ** -0.5),
        "w_out": nrm(ks[19], (L, D_MODEL, D_MODEL), DN_BETA * D_MODEL ** -0.5),
        "ln1_g": 1.0 + nrm(ks[20], (L, D_MODEL), 0.02),
        "ln1_b": nrm(ks[21], (L, D_MODEL), 0.02),
        "ffn_w1": nrm(ks[22], (L, D_MODEL, D_FF), D_MODEL ** -0.5),
        "ffn_w3": nrm(ks[23], (L, D_MODEL, D_FF), D_MODEL ** -0.5),
        "ffn_w2": nrm(ks[24], (L, D_FF, D_MODEL), DN_BETA * D_FF ** -0.5),
        "ln2_g": 1.0 + nrm(ks[25], (L, D_MODEL), 0.02),
        "ln2_b": nrm(ks[26], (L, D_MODEL), 0.02),
    }


def reference(x, w_in, sgu_ln_g, sgu_ln_b, sgu_w, sgu_b, ssm_lambda_re, ssm_lambda_im, ssm_log_dt,
              ssm_b_re, ssm_b_im, ssm_c_re, ssm_c_im, ssm_d, glu_w, glu_b, w_branch_a, w_branch_b,
              w_branch_c, w_out, ln1_g, ln1_b, ffn_w1, ffn_w3, ffn_w2, ln2_g, ln2_b):
    for l in range(DEPTH):
        mix = hybrid_mixer(x, w_in[l], sgu_ln_g[l], sgu_ln_b[l], sgu_w[l], sgu_b[l],
                           ssm_lambda_re[l], ssm_lambda_im[l], ssm_log_dt[l], ssm_b_re[l], ssm_b_im[l],
                           ssm_c_re[l], ssm_c_im[l], ssm_d[l], glu_w[l], glu_b[l],
                           w_branch_a[l], w_branch_b[l], w_branch_c[l], w_out[l])
        x = layer_norm(DN_ALPHA * x + mix, ln1_g[l], ln1_b[l])
        x = layer_norm(DN_ALPHA * x + swiglu(x, ffn_w1[l], ffn_w3[l], ffn_w2[l]), ln2_g[l], ln2_b[l])
    return x
```

```python
import functools

import jax
import jax.numpy as jnp
from jax import lax
from jax.experimental import pallas as pl
from jax.experimental.pallas import tpu as pltpu

F32 = jnp.float32
BF16 = jnp.bfloat16

D_MODEL = 1024
SGU_CHUNK = 128
SGU_GROUPS = 4
SGU_WIDTH = 512
SSM_WIDTH = 512
SSM_GROUP = 16
SSM_GROUPS = 32
SSM_STATE = 64
ATT_HEADS = 8
HEAD_DIM = 64
ATT_WIDTH = 512
MOBA_BLOCK = 256
MOBA_TOPK = 3
D_FF = 2816
DEPTH = 2
DN_ALPHA = (2 * DEPTH) ** 0.25
LN_EPS = 1e-5
NEG_BIG = -1e30

HEAD_PAD = 128
QK_PAD = ATT_HEADS * HEAD_PAD
SEL_ROWS = 8
S5_CHUNK = 64
S5_COLS = S5_CHUNK * SSM_GROUP
VMEM_LIMIT = 56 * 1024 * 1024

_HI = lax.Precision.HIGHEST


def _dot(a, b, precision=None):
    return jnp.dot(a, b, preferred_element_type=F32, precision=precision)


def _dot_nt(a, b, precision=None):
    return lax.dot_general(a, b, (((1,), (1,)), ((), ())),
                           preferred_element_type=F32, precision=precision)


def _layer_norm(x, g, b):
    mu = jnp.mean(x, axis=-1, keepdims=True)
    xc = x - mu
    var = jnp.mean(xc * xc, axis=-1, keepdims=True)
    return xc * lax.rsqrt(var + LN_EPS) * g + b


def _inproj_kernel(nb, x_ref, wz_ref, wq_ref, wk_ref, wvt_ref, eq_ref, ek_ref, lng_ref, lnb_ref,
                   sw_ref, sbt_ref, bra_ref, u_ref, q_ref, k_ref, vt_ref, sel_ref, kmean_ref):
    i = pl.program_id(0)
    il = i % nb

    @pl.when(i == 0)
    def _():
        kmean_ref[...] = jnp.zeros_like(kmean_ref)

    xb = x_ref[...].astype(BF16)

    z = jax.nn.gelu(_dot(xb, wz_ref[:, : 2 * SGU_WIDTH]))
    u = z[:, :SGU_WIDTH]
    vn = _layer_norm(z[:, SGU_WIDTH:], lng_ref[...], lnb_ref[...]).astype(BF16)
    r_io = lax.broadcasted_iota(jnp.int32, (SGU_CHUNK, SGU_CHUNK), 0)
    c_io = lax.broadcasted_iota(jnp.int32, (SGU_CHUNK, SGU_CHUNK), 1)
    tril = r_io >= c_io
    for g in range(SGU_GROUPS):
        w = jnp.where(tril, sw_ref[g], jnp.zeros((), BF16))
        bias = sbt_ref[:, g:g + 1]
        gs = slice(g * SGU_CHUNK, (g + 1) * SGU_CHUNK)
        for c in range(MOBA_BLOCK // SGU_CHUNK):
            rs = slice(c * SGU_CHUNK, (c + 1) * SGU_CHUNK)
            mixed = _dot(w, vn[rs, gs]) + bias
            bra_ref[rs, gs] = (u[rs, gs] * mixed).astype(BF16)

    u_ref[...] = _dot(xb, wz_ref[:, 2 * SGU_WIDTH:])

    qa = _dot(xb, wq_ref[...]) + eq_ref[...]
    q_ref[...] = qa.astype(BF16)
    ka = _dot(xb, wk_ref[...])
    k_ref[...] = (ka + ek_ref[...]).astype(BF16)
    kmean_ref[pl.ds(il, 1), :] = jnp.mean(ka, axis=0, keepdims=True)
    vt_ref[...] = _dot_nt(wvt_ref[...], xb).astype(BF16)

    blk = lax.broadcasted_iota(jnp.int32, (nb, MOBA_BLOCK), 0)
    neg_inf = jnp.full((), -jnp.inf, F32)
    for h in range(ATT_HEADS):
        hs = slice(h * HEAD_PAD, h * HEAD_PAD + HEAD_DIM)
        gate = _dot_nt(kmean_ref[:, hs], qa[:, hs], precision=_HI)
        gate = jnp.where(blk < il, gate, neg_inf)
        base = (h // 2) * SEL_ROWS + (h % 2) * MOBA_TOPK
        for r in range(MOBA_TOPK):
            m = jnp.max(gate, axis=0, keepdims=True)
            idx = jnp.min(jnp.where(gate == m, blk, nb), axis=0, keepdims=True)
            sel_ref[base + r:base + r + 1, :] = jnp.where(r < il, idx, -1)
            gate = jnp.where(blk == idx, neg_inf, gate)
    pad = jnp.full((SEL_ROWS - 2 * MOBA_TOPK, MOBA_BLOCK), -1, jnp.int32)
    for hp in range(ATT_HEADS // 2):
        sel_ref[hp * SEL_ROWS + 2 * MOBA_TOPK:(hp + 1) * SEL_ROWS, :] = pad


def _inproj(x2, wz, wq, wk, wvt, eq, ek, lng, lnb, sw, sbt, nb):
    n = x2.shape[0]
    grid = n // MOBA_BLOCK
    const = lambda i: (0, 0)
    return pl.pallas_call(
        functools.partial(_inproj_kernel, nb),
        grid=(grid,),
        in_specs=[
            pl.BlockSpec((MOBA_BLOCK, D_MODEL), lambda i: (i, 0)),
            pl.BlockSpec(wz.shape, const),
            pl.BlockSpec(wq.shape, const),
            pl.BlockSpec(wk.shape, const),
            pl.BlockSpec(wvt.shape, const),
            pl.BlockSpec(eq.shape, const),
            pl.BlockSpec(ek.shape, const),
            pl.BlockSpec(lng.shape, const),
            pl.BlockSpec(lnb.shape, const),
            pl.BlockSpec(sw.shape, lambda i: (0, 0, 0)),
            pl.BlockSpec(sbt.shape, const),
        ],
        out_specs=[
            pl.BlockSpec((MOBA_BLOCK, SGU_WIDTH), lambda i: (i, 0)),
            pl.BlockSpec((MOBA_BLOCK, SSM_WIDTH), lambda i: (i, 0)),
            pl.BlockSpec((MOBA_BLOCK, QK_PAD), lambda i: (i, 0)),
            pl.BlockSpec((MOBA_BLOCK, QK_PAD), lambda i: (i, 0)),
            pl.BlockSpec((ATT_WIDTH, MOBA_BLOCK), lambda i: (0, i)),
            pl.BlockSpec((ATT_HEADS // 2 * SEL_ROWS, MOBA_BLOCK), lambda i: (0, i)),
        ],
        out_shape=[
            jax.ShapeDtypeStruct((n, SGU_WIDTH), BF16),
            jax.ShapeDtypeStruct((n, SSM_WIDTH), F32),
            jax.ShapeDtypeStruct((n, QK_PAD), BF16),
            jax.ShapeDtypeStruct((n, QK_PAD), BF16),
            jax.ShapeDtypeStruct((ATT_WIDTH, n), BF16),
            jax.ShapeDtypeStruct((ATT_HEADS // 2 * SEL_ROWS, n), jnp.int32),
        ],
        scratch_shapes=[pltpu.VMEM((nb, QK_PAD), F32)],
        compiler_params=pltpu.CompilerParams(
            dimension_semantics=("arbitrary",), vmem_limit_bytes=VMEM_LIMIT),
        name="inproj_sgu_gate",
    )(x2, wz, wq, wk, wvt, eq, ek, lng, lnb, sw, sbt)


def _s5_param_kernel(lre_r, lim_r, lre_c, lim_c, ldt, bt_re, bt_im, ct_re, ct_im,
                     t_ref, w_ref, m_ref, a_ref):
    two_p = 2 * SSM_STATE
    dt = jnp.exp(ldt[0])

    def powers(ar, ai, e):
        mag = jnp.exp(ar * e)
        return mag * jnp.cos(ai * e), mag * jnp.sin(ai * e)

    lr, li = lre_r[0], lim_r[0]
    ar, ai = lr * dt, li * dt
    lbr, lbi = powers(ar, ai, 1.0)
    den = lr * lr + li * li
    cfr = ((lbr - 1.0) * lr + lbi * li) / den
    cfi = (lbi * lr - (lbr - 1.0) * li) / den
    bbr = cfr * bt_re[0] - cfi * bt_im[0]
    bbi = cfr * bt_im[0] + cfi * bt_re[0]

    s_col = lax.broadcasted_iota(jnp.int32, (S5_CHUNK, two_p), 0).astype(F32)
    neg_r, neg_i = powers(ar, ai, -s_col)
    rev_r, rev_i = powers(ar, ai, (S5_CHUNK - 1.0) - s_col)

    row = lax.broadcasted_iota(jnp.int32, (S5_COLS, S5_CHUNK), 0)
    col = lax.broadcasted_iota(jnp.int32, (S5_COLS, S5_CHUNK), 1)
    exp_rows = (lax.shift_right_logical(row, 4) == col).astype(F32)
    row = lax.broadcasted_iota(jnp.int32, (S5_COLS, SSM_GROUP), 0)
    col = lax.broadcasted_iota(jnp.int32, (S5_COLS, SSM_GROUP), 1)
    tile_rows = ((row & (SSM_GROUP - 1)) == col).astype(F32)
    bbr_t = _dot(tile_rows, bbr, _HI)
    bbi_t = _dot(tile_rows, bbi, _HI)
    lane = lax.broadcasted_iota(jnp.int32, (S5_COLS, two_p), 1)
    first = lane < SSM_STATE

    pr, pi_ = _dot(exp_rows, neg_r, _HI), _dot(exp_rows, neg_i, _HI)
    xt = jnp.where(first, pr * bbr_t - pi_ * bbi_t, -(pr * bbi_t + pi_ * bbr_t))
    pr, pi_ = _dot(exp_rows, rev_r, _HI), _dot(exp_rows, rev_i, _HI)
    w_ref[0] = jnp.where(first, pr * bbr_t - pi_ * bbi_t, pr * bbi_t + pi_ * bbr_t).astype(BF16)

    lrc, lic = lre_c[0], lim_c[0]
    arc, aic = lrc * dt, lic * dt
    t_row = lax.broadcasted_iota(jnp.int32, (two_p, S5_CHUNK), 1).astype(F32)
    yr, yi = powers(arc, aic, t_row)
    er, ei = powers(arc, aic, t_row + 1.0)
    row = lax.broadcasted_iota(jnp.int32, (S5_CHUNK, S5_COLS), 0)
    col = lax.broadcasted_iota(jnp.int32, (S5_CHUNK, S5_COLS), 1)
    exp_cols = (lax.shift_right_logical(col, 4) == row).astype(F32)
    row = lax.broadcasted_iota(jnp.int32, (SSM_GROUP, S5_COLS), 0)
    col = lax.broadcasted_iota(jnp.int32, (SSM_GROUP, S5_COLS), 1)
    tile_cols = ((col & (SSM_GROUP - 1)) == row).astype(F32)
    cr_t = _dot(ct_re[0], tile_cols, _HI)
    ci_t = _dot(ct_im[0], tile_cols, _HI)
    rowp = lax.broadcasted_iota(jnp.int32, (two_p, S5_COLS), 0)
    top = rowp < SSM_STATE

    pr, pi_ = _dot(yr, exp_cols, _HI), _dot(yi, exp_cols, _HI)
    y2 = jnp.where(top, pr * cr_t - pi_ * ci_t, pr * ci_t + pi_ * cr_t)
    pr, pi_ = _dot(er, exp_cols, _HI), _dot(ei, exp_cols, _HI)
    m_ref[0] = jnp.where(top, pr * cr_t - pi_ * ci_t, -(pr * ci_t + pi_ * cr_t)).astype(BF16)

    t_full = _dot(xt, y2, _HI)
    rs = lax.shift_right_logical(lax.broadcasted_iota(jnp.int32, (S5_COLS, S5_COLS), 0), 4)
    cs = lax.shift_right_logical(lax.broadcasted_iota(jnp.int32, (S5_COLS, S5_COLS), 1), 4)
    t_ref[0] = jnp.where(rs <= cs, t_full, 0.0).astype(BF16)

    pr, pi_ = lbr, lbi
    for _ in range(S5_CHUNK.bit_length() - 1):
        pr, pi_ = pr * pr - pi_ * pi_, 2.0 * pr * pi_
    a_ref[0, 0:1, :] = pr
    a_ref[0, 1:2, :] = pi_


def _s5_params(lre, lim, ldt, b_re, b_im, c_re, c_im):
    g, p = lre.shape
    dup = lambda a: jnp.concatenate([a, a], axis=-1)
    lre_r, lim_r = dup(lre)[:, None, :], dup(lim)[:, None, :]
    lre_c, lim_c = dup(lre)[:, :, None], dup(lim)[:, :, None]
    bt_re = dup(jnp.swapaxes(b_re, 1, 2))
    bt_im = dup(jnp.swapaxes(b_im, 1, 2))
    ct = lambda c: jnp.concatenate([jnp.swapaxes(c, 1, 2)] * 2, axis=1)
    ct_re, ct_im = ct(c_re), ct(c_im)
    ldt3 = ldt[:, None, None]
    args = (lre_r, lim_r, lre_c, lim_c, ldt3, bt_re, bt_im, ct_re, ct_im)
    in_specs = [pl.BlockSpec((1,) + a.shape[1:], lambda i: (i, 0, 0)) for a in args]
    two_p = 2 * p
    return pl.pallas_call(
        _s5_param_kernel,
        grid=(g,),
        in_specs=in_specs,
        out_specs=[
            pl.BlockSpec((1, S5_COLS, S5_COLS), lambda i: (i, 0, 0)),
            pl.BlockSpec((1, S5_COLS, two_p), lambda i: (i, 0, 0)),
            pl.BlockSpec((1, two_p, S5_COLS), lambda i: (i, 0, 0)),
            pl.BlockSpec((1, 2, two_p), lambda i: (i, 0, 0)),
        ],
        out_shape=[
            jax.ShapeDtypeStruct((g, S5_COLS, S5_COLS), BF16),
            jax.ShapeDtypeStruct((g, S5_COLS, two_p), BF16),
            jax.ShapeDtypeStruct((g, two_p, S5_COLS), BF16),
            jax.ShapeDtypeStruct((g, 2, two_p), F32),
        ],
        compiler_params=pltpu.CompilerParams(vmem_limit_bytes=VMEM_LIMIT),
        name="s5_params",
    )(*args)


def _s5_scan_kernel(cpb, u_ref, t_ref, w_ref, m_ref, a_ref, d_ref, y_ref):
    u = u_ref[0]
    ub = u.astype(BF16)
    rows = u.shape[0]
    two_p = 2 * SSM_STATE
    h = _dot(ub, w_ref[0])
    ar = a_ref[0, 0:1, :]
    ai = a_ref[0, 1:2, :]
    lane = lax.broadcasted_iota(jnp.int32, (1, two_p), 1)
    sign = jnp.where(lane < SSM_STATE, -1.0, 1.0)
    chunk = lax.broadcasted_iota(jnp.int32, (rows, two_p), 0) % cpb
    step = 1
    while step < cpb:
        prev = jnp.where(chunk >= step, pltpu.roll(h, step, axis=0), 0.0)
        swapped = pltpu.roll(prev, SSM_STATE, axis=1)
        h = h + prev * ar + swapped * (ai * sign)
        ar, ai = ar * ar - ai * ai, 2.0 * ar * ai
        step *= 2
    hprev = jnp.where(chunk >= 1, pltpu.roll(h, 1, axis=0), 0.0)
    y = _dot(ub, t_ref[0]) + _dot(hprev.astype(BF16), m_ref[0])
    y_ref[0] = jax.nn.gelu(y + d_ref[0] * u)


def _s5_scan(ug, tmat, wmat, mmat, amat, dtile, cpb):
    g, rows, cols = ug.shape
    two_p = 2 * SSM_STATE
    idx = lambda i: (i, 0, 0)
    return pl.pallas_call(
        functools.partial(_s5_scan_kernel, cpb),
        grid=(g,),
        in_specs=[
            pl.BlockSpec((1, rows, cols), idx),
            pl.BlockSpec((1, cols, cols), idx),
            pl.BlockSpec((1, cols, two_p), idx),
            pl.BlockSpec((1, two_p, cols), idx),
            pl.BlockSpec((1, 2, two_p), idx),
            pl.BlockSpec((1, 1, cols), idx),
        ],
        out_specs=pl.BlockSpec((1, rows, cols), idx),
        out_shape=jax.ShapeDtypeStruct((g, rows, cols), F32),
        compiler_params=pltpu.CompilerParams(vmem_limit_bytes=VMEM_LIMIT),
        name="s5_scan",
    )(ug, tmat, wmat, mmat, amat, dtile)


def _attn_kernel(q_ref, k_ref, vt_ref, sel_ref, slope_ref, o_ref):
    i = pl.program_id(2)
    q = q_ref[...]
    trel = lax.broadcasted_iota(jnp.int32, (1, MOBA_BLOCK), 1).astype(F32)
    kio = lax.broadcasted_iota(jnp.int32, (MOBA_BLOCK, MOBA_BLOCK), 0)
    qio = lax.broadcasted_iota(jnp.int32, (MOBA_BLOCK, MOBA_BLOCK), 1)
    causal = kio <= qio
    own = pl.multiple_of(i * MOBA_BLOCK, MOBA_BLOCK)

    qs, rqs, slopes, sels = [], [], [], []
    init = []
    for h in range(2):
        qh = q[:, h * HEAD_PAD:(h + 1) * HEAD_PAD]
        slope = slope_ref[0, h:h + 1, :]
        rq = slope * trel
        z = _dot_nt(k_ref[pl.ds(own, MOBA_BLOCK), h * HEAD_PAD:(h + 1) * HEAD_PAD], qh)
        z = jnp.where(causal, z, NEG_BIG)
        m = jnp.max(z, axis=0, keepdims=True) - rq
        p = jnp.exp(z - (rq + m))
        l = jnp.sum(p, axis=0, keepdims=True)
        acc = _dot(vt_ref[h * HEAD_DIM:(h + 1) * HEAD_DIM, pl.ds(own, MOBA_BLOCK)], p.astype(BF16))
        qs.append(qh)
        rqs.append(rq)
        slopes.append(slope)
        sels.append([sel_ref[h * MOBA_TOPK + r:h * MOBA_TOPK + r + 1, :] for r in range(MOBA_TOPK)])
        init += [m, l, acc]

    def body(j, carry):
        out = []
        ks = pl.multiple_of(j * MOBA_BLOCK, MOBA_BLOCK)
        gap = ((i - j) * MOBA_BLOCK).astype(F32)
        for h in range(2):
            m, l, acc = carry[3 * h:3 * h + 3]
            z = _dot_nt(k_ref[pl.ds(ks, MOBA_BLOCK), h * HEAD_PAD:(h + 1) * HEAD_PAD], qs[h])
            s0, s1, s2 = sels[h]
            act = (s0 == j) | (s1 == j) | (s2 == j)
            off = rqs[h] + slopes[h] * gap
            m_new = jnp.where(act, jnp.maximum(m, jnp.max(z, axis=0, keepdims=True) - off), m)
            p = jnp.exp(z - jnp.where(act, off + m_new, -NEG_BIG))
            alpha = jnp.exp(m - m_new)
            l = alpha * l + jnp.sum(p, axis=0, keepdims=True)
            pv = _dot(vt_ref[h * HEAD_DIM:(h + 1) * HEAD_DIM, pl.ds(ks, MOBA_BLOCK)], p.astype(BF16))
            out += [m_new, l, alpha * acc + pv]
        return tuple(out)

    fin = lax.fori_loop(0, i, body, tuple(init))
    o = jnp.concatenate([fin[2] / fin[1], fin[5] / fin[4]], axis=0)
    o_ref[...] = o.T.astype(BF16)


def _attention(qa, ka, vt, sel, slope_rows, bsz, seq):
    n = bsz * seq
    nb = seq // MOBA_BLOCK
    hp = ATT_HEADS // 2
    return pl.pallas_call(
        _attn_kernel,
        grid=(bsz, hp, nb),
        in_specs=[
            pl.BlockSpec((MOBA_BLOCK, 2 * HEAD_PAD), lambda b, p, i: (b * nb + i, p)),
            pl.BlockSpec((seq, 2 * HEAD_PAD), lambda b, p, i: (b, p)),
            pl.BlockSpec((2 * HEAD_DIM, seq), lambda b, p, i: (p, b)),
            pl.BlockSpec((SEL_ROWS, MOBA_BLOCK), lambda b, p, i: (p, b * nb + i)),
            pl.BlockSpec((1, SEL_ROWS, MOBA_BLOCK), lambda b, p, i: (p, 0, 0)),
        ],
        out_specs=pl.BlockSpec((MOBA_BLOCK, 2 * HEAD_DIM), lambda b, p, i: (b * nb + i, p)),
        out_shape=jax.ShapeDtypeStruct((n, ATT_WIDTH), BF16),
        compiler_params=pltpu.CompilerParams(vmem_limit_bytes=VMEM_LIMIT),
        name="moba_attention",
    )(qa, ka, vt, sel, slope_rows)


def _merge_kernel(x_ref, bra_ref, ys_ref, brc_ref, wg_ref, wa_ref, wb_ref, wc_ref, gw_ref, gb_ref,
                  wo_ref, lg_ref, lb_ref, o_ref):
    x = x_ref[...]
    xb = x.astype(BF16)

    def gate(k):
        return jax.nn.sigmoid(_dot(xb, wg_ref[:, k * D_MODEL:(k + 1) * D_MODEL]))

    merged = gate(0) * _dot(bra_ref[...], wa_ref[...])
    ys = ys_ref[...]
    brb = ys * jax.nn.sigmoid(_dot(ys.astype(BF16), gw_ref[...]) + gb_ref[...])
    merged = merged + gate(1) * _dot(brb.astype(BF16), wb_ref[...])
    merged = merged + gate(2) * _dot(brc_ref[...], wc_ref[...])
    mix = _dot(merged.astype(BF16), wo_ref[...])
    o_ref[...] = _layer_norm(DN_ALPHA * x + mix, lg_ref[...], lb_ref[...])


def _merge(x2, bra, ys, brc, wg, wa, wb, wc, gw, gb, wo, lg, lb, tm):
    n = x2.shape[0]
    const = lambda i: (0, 0)
    tile = lambda w: pl.BlockSpec((tm, w), lambda i: (i, 0))
    full = lambda a: pl.BlockSpec(a.shape, const)
    return pl.pallas_call(
        _merge_kernel,
        grid=(n // tm,),
        in_specs=[tile(D_MODEL), tile(SGU_WIDTH), tile(SSM_WIDTH), tile(ATT_WIDTH),
                  full(wg), full(wa), full(wb), full(wc), full(gw), full(gb), full(wo), full(lg), full(lb)],
        out_specs=tile(D_MODEL),
        out_shape=jax.ShapeDtypeStruct((n, D_MODEL), F32),
        compiler_params=pltpu.CompilerParams(vmem_limit_bytes=VMEM_LIMIT),
        name="merge_ln",
    )(x2, bra, ys, brc, wg, wa, wb, wc, gw, gb, wo, lg, lb)


FF_CHUNK = D_FF // 2


def _ffn_kernel(x_ref, w1_ref, w3_ref, w2_ref, lg_ref, lb_ref, o_ref):
    x = x_ref[...]
    xb = x.astype(BF16)
    acc = None
    for c in range(D_FF // FF_CHUNK):
        cs = slice(c * FF_CHUNK, (c + 1) * FF_CHUNK)
        h = (jax.nn.silu(_dot(xb, w1_ref[:, cs])) * _dot(xb, w3_ref[:, cs])).astype(BF16)
        part = _dot(h, w2_ref[cs, :])
        acc = part if acc is None else acc + part
    o_ref[...] = _layer_norm(DN_ALPHA * x + acc, lg_ref[...], lb_ref[...])


def _ffn(x2, w1, w3, w2, lg, lb, tm):
    n = x2.shape[0]
    const = lambda i: (0, 0)
    full = lambda a: pl.BlockSpec(a.shape, const)
    return pl.pallas_call(
        _ffn_kernel,
        grid=(n // tm,),
        in_specs=[pl.BlockSpec((tm, D_MODEL), lambda i: (i, 0)),
                  full(w1), full(w3), full(w2), full(lg), full(lb)],
        out_specs=pl.BlockSpec((tm, D_MODEL), lambda i: (i, 0)),
        out_shape=jax.ShapeDtypeStruct((n, D_MODEL), F32),
        compiler_params=pltpu.CompilerParams(vmem_limit_bytes=VMEM_LIMIT),
        name="ffn_ln",
    )(x2, w1, w3, w2, lg, lb)


def _pad_heads(w):
    d = w.shape[0]
    w = w.reshape(d, ATT_HEADS, HEAD_DIM)
    return jnp.pad(w, ((0, 0), (0, 0), (0, HEAD_PAD - HEAD_DIM))).reshape(d, QK_PAD)


def _alibi_extras():
    slopes = 2.0 ** (-8.0 * jnp.arange(1, ATT_HEADS + 1, dtype=F32) / ATT_HEADS)
    srel = jnp.arange(MOBA_BLOCK, dtype=F32)
    eq = jnp.zeros((MOBA_BLOCK, ATT_HEADS, HEAD_PAD), F32).at[:, :, HEAD_DIM].set(1.0)
    ek = jnp.zeros((MOBA_BLOCK, ATT_HEADS, HEAD_PAD), F32).at[:, :, HEAD_DIM].set(
        srel[:, None] * slopes[None, :])
    slope_rows = jnp.zeros((ATT_HEADS // 2, SEL_ROWS, MOBA_BLOCK), F32).at[:, :2, :].set(
        jnp.broadcast_to(slopes.reshape(ATT_HEADS // 2, 2, 1), (ATT_HEADS // 2, 2, MOBA_BLOCK)))
    return eq.reshape(MOBA_BLOCK, QK_PAD), ek.reshape(MOBA_BLOCK, QK_PAD), slope_rows


def kernel(x, w_in, sgu_ln_g, sgu_ln_b, sgu_w, sgu_b, ssm_lambda_re, ssm_lambda_im, ssm_log_dt,
           ssm_b_re, ssm_b_im, ssm_c_re, ssm_c_im, ssm_d, glu_w, glu_b, w_branch_a, w_branch_b,
           w_branch_c, w_out, ln1_g, ln1_b, ffn_w1, ffn_w3, ffn_w2, ln2_g, ln2_b):
    bsz, seq, _ = x.shape
    n = bsz * seq
    nb = seq // MOBA_BLOCK
    cpb = seq // S5_CHUNK
    tm = 512 if n % 512 == 0 else MOBA_BLOCK
    eq, ek, slope_rows = _alibi_extras()
    scale = HEAD_DIM ** -0.5
    o_q = 2 * SGU_WIDTH + SSM_WIDTH
    o_g = o_q + 3 * ATT_WIDTH

    x2 = x.reshape(n, D_MODEL)
    for l in range(DEPTH):
        wl = w_in[l]
        wz = wl[:, :o_q].astype(BF16)
        wq = _pad_heads(wl[:, o_q:o_q + ATT_WIDTH] * scale).astype(BF16)
        wk = _pad_heads(wl[:, o_q + ATT_WIDTH:o_q + 2 * ATT_WIDTH]).astype(BF16)
        wvt = wl[:, o_q + 2 * ATT_WIDTH:o_g].T.astype(BF16)
        wg = wl[:, o_g:].astype(BF16)

        bra, u, qa, ka, vt, sel = _inproj(
            x2, wz, wq, wk, wvt, eq, ek, sgu_ln_g[l][None, :], sgu_ln_b[l][None, :],
            sgu_w[l].astype(BF16), sgu_b[l].T, nb)

        tmat, wmat, mmat, amat = _s5_params(
            ssm_lambda_re[l], ssm_lambda_im[l], ssm_log_dt[l], ssm_b_re[l], ssm_b_im[l],
            ssm_c_re[l], ssm_c_im[l])
        ug = u.reshape(bsz * cpb, S5_CHUNK, SSM_GROUPS, SSM_GROUP).transpose(2, 0, 1, 3)
        ug = ug.reshape(SSM_GROUPS, bsz * cpb, S5_COLS)
        dtile = jnp.tile(ssm_d[l].reshape(SSM_GROUPS, 1, SSM_GROUP), (1, 1, S5_CHUNK))
        yg = _s5_scan(ug, tmat, wmat, mmat, amat, dtile, cpb)
        ys = yg.reshape(SSM_GROUPS, bsz * cpb, S5_CHUNK, SSM_GROUP).transpose(1, 2, 0, 3)
        ys = ys.reshape(n, SSM_WIDTH)

        brc = _attention(qa, ka, vt, sel, slope_rows, bsz, seq)

        x2 = _merge(x2, bra, ys, brc, wg, w_branch_a[l].astype(BF16), w_branch_b[l].astype(BF16),
                    w_branch_c[l].astype(BF16), glu_w[l].astype(BF16), glu_b[l][None, :],
                    w_out[l].astype(BF16), ln1_g[l][None, :], ln1_b[l][None, :], tm)
        x2 = _ffn(x2, ffn_w1[l].astype(BF16), ffn_w3[l].astype(BF16), ffn_w2[l].astype(BF16),
                  ln2_g[l][None, :], ln2_b[l][None, :], tm)
    return x2.reshape(bsz, seq, D_MODEL)
```

```python
import functools

import jax
import jax.numpy as jnp
from jax import lax
from jax.experimental import pallas as pl
from jax.experimental.pallas import tpu as pltpu
from jax.experimental.pallas import tpu_sc as plsc

F32 = jnp.float32
BF16 = jnp.bfloat16

D_MODEL = 1024
SGU_CHUNK = 128
SGU_GROUPS = 4
SGU_WIDTH = 512
SSM_WIDTH = 512
SSM_GROUP = 16
SSM_GROUPS = 32
SSM_STATE = 64
ATT_HEADS = 8
HEAD_DIM = 64
ATT_WIDTH = 512
MOBA_BLOCK = 256
MOBA_TOPK = 3
D_FF = 2816
DEPTH = 2
DN_ALPHA = (2 * DEPTH) ** 0.25
LN_EPS = 1e-5
NEG_BIG = -1e30

HEAD_PAD = 128
QK_PAD = ATT_HEADS * HEAD_PAD
Q_LANE_ONE = HEAD_DIM
Q_LANE_BLK = HEAD_DIM + 1
Q_LANE_ROW = HEAD_DIM + 2
SC_WINDOW = 128
ROUTED_TILES = 2
S5_CHUNK = 64
S5_COLS = S5_CHUNK * SSM_GROUP
VMEM_LIMIT = 56 * 1024 * 1024

_HI = lax.Precision.HIGHEST


def _dot(a, b, precision=None):
    return jnp.dot(a, b, preferred_element_type=F32, precision=precision)


def _dot_nt(a, b, precision=None):
    return lax.dot_general(a, b, (((1,), (1,)), ((), ())),
                           preferred_element_type=F32, precision=precision)


def _layer_norm(x, g, b):
    mu = jnp.mean(x, axis=-1, keepdims=True)
    xc = x - mu
    var = jnp.mean(xc * xc, axis=-1, keepdims=True)
    return xc * lax.rsqrt(var + LN_EPS) * g + b


def _inproj_kernel(nb, x_ref, wz_ref, wq_ref, wk_ref, wvt_ref, eq_ref, ek_ref, lng_ref, lnb_ref,
                   sw_ref, sbt_ref, bra_ref, u_ref, q_ref, k_ref, vt_ref, sel_ref, rank_ref, cnt_ref,
                   kmean_ref, carry_ref):
    i = pl.program_id(0)
    il = i % nb

    @pl.when(i == 0)
    def _():
        kmean_ref[...] = jnp.zeros_like(kmean_ref)

    @pl.when(il == 0)
    def _():
        carry_ref[...] = jnp.zeros_like(carry_ref)

    xb = x_ref[...].astype(BF16)

    z = jax.nn.gelu(_dot(xb, wz_ref[:, : 2 * SGU_WIDTH]))
    u = z[:, :SGU_WIDTH]
    vn = _layer_norm(z[:, SGU_WIDTH:], lng_ref[...], lnb_ref[...]).astype(BF16)
    r_io = lax.broadcasted_iota(jnp.int32, (SGU_CHUNK, SGU_CHUNK), 0)
    c_io = lax.broadcasted_iota(jnp.int32, (SGU_CHUNK, SGU_CHUNK), 1)
    tril = r_io >= c_io
    for g in range(SGU_GROUPS):
        w = jnp.where(tril, sw_ref[g], jnp.zeros((), BF16))
        bias = sbt_ref[:, g:g + 1]
        gs = slice(g * SGU_CHUNK, (g + 1) * SGU_CHUNK)
        for c in range(MOBA_BLOCK // SGU_CHUNK):
            rs = slice(c * SGU_CHUNK, (c + 1) * SGU_CHUNK)
            mixed = _dot(w, vn[rs, gs]) + bias
            bra_ref[rs, gs] = (u[rs, gs] * mixed).astype(BF16)

    u_ref[...] = _dot(xb, wz_ref[:, 2 * SGU_WIDTH:])

    lane = lax.broadcasted_iota(jnp.int32, (1, QK_PAD), 1) & (HEAD_PAD - 1)
    qa = _dot(xb, wq_ref[...]) + eq_ref[...] + jnp.where(lane == Q_LANE_BLK, il.astype(F32), 0.0)
    for h in range(ATT_HEADS):
        q_ref[h] = qa[:, h * HEAD_PAD:(h + 1) * HEAD_PAD]
    ka = _dot(xb, wk_ref[...])
    k_ref[...] = (ka + ek_ref[...]).astype(BF16)
    kmean_ref[pl.ds(il, 1), :] = jnp.mean(ka, axis=0, keepdims=True)
    vt_ref[...] = _dot_nt(wvt_ref[...], xb).astype(BF16)

    blk = lax.broadcasted_iota(jnp.int32, (nb, MOBA_BLOCK), 0)
    neg_inf = jnp.full((), -jnp.inf, F32)
    r_io = lax.broadcasted_iota(jnp.int32, (MOBA_BLOCK, MOBA_BLOCK), 0)
    c_io = lax.broadcasted_iota(jnp.int32, (MOBA_BLOCK, MOBA_BLOCK), 1)
    earlier = (r_io < c_io).astype(BF16)
    for h in range(ATT_HEADS):
        hs = slice(h * HEAD_PAD, h * HEAD_PAD + HEAD_DIM)
        gate = _dot_nt(kmean_ref[:, hs], qa[:, hs], precision=_HI)
        gate = jnp.where(blk < il, gate, neg_inf)
        sels = []
        for r in range(MOBA_TOPK):
            m = jnp.max(gate, axis=0, keepdims=True)
            idx = jnp.min(jnp.where(gate == m, blk, nb), axis=0, keepdims=True)
            sels.append(jnp.where(r < il, idx, -1))
            gate = jnp.where(blk == idx, neg_inf, gate)
        hit = [blk == s for s in sels]
        onehot = jnp.where(hit[0] | hit[1] | hit[2], 1.0, 0.0)
        before = carry_ref[h] + _dot(onehot.astype(BF16), earlier)
        carry_ref[h] = carry_ref[h] + jnp.sum(onehot, axis=1, keepdims=True)
        for r in range(MOBA_TOPK):
            sel_ref[r, h:h + 1, :] = sels[r]
            rank = jnp.sum(jnp.where(hit[r], before, 0.0), axis=0, keepdims=True)
            rank_ref[r, h:h + 1, :] = rank.astype(jnp.int32)

    @pl.when(il == nb - 1)
    def _():
        cnt_ref[0] = carry_ref[...]


def _inproj(x2, wz, wq, wk, wvt, eq, ek, lng, lnb, sw, sbt, nb):
    n = x2.shape[0]
    grid = n // MOBA_BLOCK
    const = lambda i: (0, 0)
    return pl.pallas_call(
        functools.partial(_inproj_kernel, nb),
        grid=(grid,),
        in_specs=[
            pl.BlockSpec((MOBA_BLOCK, D_MODEL), lambda i: (i, 0)),
            pl.BlockSpec(wz.shape, const),
            pl.BlockSpec(wq.shape, const),
            pl.BlockSpec(wk.shape, const),
            pl.BlockSpec(wvt.shape, const),
            pl.BlockSpec(eq.shape, const),
            pl.BlockSpec(ek.shape, const),
            pl.BlockSpec(lng.shape, const),
            pl.BlockSpec(lnb.shape, const),
            pl.BlockSpec(sw.shape, lambda i: (0, 0, 0)),
            pl.BlockSpec(sbt.shape, const),
        ],
        out_specs=[
            pl.BlockSpec((MOBA_BLOCK, SGU_WIDTH), lambda i: (i, 0)),
            pl.BlockSpec((MOBA_BLOCK, SSM_WIDTH), lambda i: (i, 0)),
            pl.BlockSpec((ATT_HEADS, MOBA_BLOCK, HEAD_PAD), lambda i: (0, i, 0)),
            pl.BlockSpec((MOBA_BLOCK, QK_PAD), lambda i: (i, 0)),
            pl.BlockSpec((ATT_WIDTH, MOBA_BLOCK), lambda i: (0, i)),
            pl.BlockSpec((MOBA_TOPK, ATT_HEADS, MOBA_BLOCK), lambda i: (0, 0, i)),
            pl.BlockSpec((MOBA_TOPK, ATT_HEADS, MOBA_BLOCK), lambda i: (0, 0, i)),
            pl.BlockSpec((1, ATT_HEADS, nb, MOBA_BLOCK), lambda i: (i // nb, 0, 0, 0)),
        ],
        out_shape=[
            jax.ShapeDtypeStruct((n, SGU_WIDTH), BF16),
            jax.ShapeDtypeStruct((n, SSM_WIDTH), F32),
            jax.ShapeDtypeStruct((ATT_HEADS, n, HEAD_PAD), F32),
            jax.ShapeDtypeStruct((n, QK_PAD), BF16),
            jax.ShapeDtypeStruct((ATT_WIDTH, n), BF16),
            jax.ShapeDtypeStruct((MOBA_TOPK, ATT_HEADS, n), jnp.int32),
            jax.ShapeDtypeStruct((MOBA_TOPK, ATT_HEADS, n), jnp.int32),
            jax.ShapeDtypeStruct((n // (nb * MOBA_BLOCK), ATT_HEADS, nb, MOBA_BLOCK), F32),
        ],
        scratch_shapes=[pltpu.VMEM((nb, QK_PAD), F32),
                        pltpu.VMEM((ATT_HEADS, nb, MOBA_BLOCK), F32)],
        compiler_params=pltpu.CompilerParams(
            dimension_semantics=("arbitrary",), vmem_limit_bytes=VMEM_LIMIT),
        name="inproj_sgu_gate",
    )(x2, wz, wq, wk, wvt, eq, ek, lng, lnb, sw, sbt)


def _s5_param_kernel(lre_r, lim_r, lre_c, lim_c, ldt, bt_re, bt_im, ct_re, ct_im,
                     t_ref, w_ref, m_ref, a_ref):
    two_p = 2 * SSM_STATE
    dt = jnp.exp(ldt[0])

    def powers(ar, ai, e):
        mag = jnp.exp(ar * e)
        return mag * jnp.cos(ai * e), mag * jnp.sin(ai * e)

    lr, li = lre_r[0], lim_r[0]
    ar, ai = lr * dt, li * dt
    lbr, lbi = powers(ar, ai, 1.0)
    den = lr * lr + li * li
    cfr = ((lbr - 1.0) * lr + lbi * li) / den
    cfi = (lbi * lr - (lbr - 1.0) * li) / den
    bbr = cfr * bt_re[0] - cfi * bt_im[0]
    bbi = cfr * bt_im[0] + cfi * bt_re[0]

    s_col = lax.broadcasted_iota(jnp.int32, (S5_CHUNK, two_p), 0).astype(F32)
    neg_r, neg_i = powers(ar, ai, -s_col)
    rev_r, rev_i = powers(ar, ai, (S5_CHUNK - 1.0) - s_col)

    row = lax.broadcasted_iota(jnp.int32, (S5_COLS, S5_CHUNK), 0)
    col = lax.broadcasted_iota(jnp.int32, (S5_COLS, S5_CHUNK), 1)
    exp_rows = (lax.shift_right_logical(row, 4) == col).astype(F32)
    row = lax.broadcasted_iota(jnp.int32, (S5_COLS, SSM_GROUP), 0)
    col = lax.broadcasted_iota(jnp.int32, (S5_COLS, SSM_GROUP), 1)
    tile_rows = ((row & (SSM_GROUP - 1)) == col).astype(F32)
    bbr_t = _dot(tile_rows, bbr, _HI)
    bbi_t = _dot(tile_rows, bbi, _HI)
    lane = lax.broadcasted_iota(jnp.int32, (S5_COLS, two_p), 1)
    first = lane < SSM_STATE

    pr, pi_ = _dot(exp_rows, neg_r, _HI), _dot(exp_rows, neg_i, _HI)
    xt = jnp.where(first, pr * bbr_t - pi_ * bbi_t, -(pr * bbi_t + pi_ * bbr_t))
    pr, pi_ = _dot(exp_rows, rev_r, _HI), _dot(exp_rows, rev_i, _HI)
    w_ref[0] = jnp.where(first, pr * bbr_t - pi_ * bbi_t, pr * bbi_t + pi_ * bbr_t).astype(BF16)

    lrc, lic = lre_c[0], lim_c[0]
    arc, aic = lrc * dt, lic * dt
    t_row = lax.broadcasted_iota(jnp.int32, (two_p, S5_CHUNK), 1).astype(F32)
    yr, yi = powers(arc, aic, t_row)
    er, ei = powers(arc, aic, t_row + 1.0)
    row = lax.broadcasted_iota(jnp.int32, (S5_CHUNK, S5_COLS), 0)
    col = lax.broadcasted_iota(jnp.int32, (S5_CHUNK, S5_COLS), 1)
    exp_cols = (lax.shift_right_logical(col, 4) == row).astype(F32)
    row = lax.broadcasted_iota(jnp.int32, (SSM_GROUP, S5_COLS), 0)
    col = lax.broadcasted_iota(jnp.int32, (SSM_GROUP, S5_COLS), 1)
    tile_cols = ((col & (SSM_GROUP - 1)) == row).astype(F32)
    cr_t = _dot(ct_re[0], tile_cols, _HI)
    ci_t = _dot(ct_im[0], tile_cols, _HI)
    rowp = lax.broadcasted_iota(jnp.int32, (two_p, S5_COLS), 0)
    top = rowp < SSM_STATE

    pr, pi_ = _dot(yr, exp_cols, _HI), _dot(yi, exp_cols, _HI)
    y2 = jnp.where(top, pr * cr_t - pi_ * ci_t, pr * ci_t + pi_ * cr_t)
    pr, pi_ = _dot(er, exp_cols, _HI), _dot(ei, exp_cols, _HI)
    m_ref[0] = jnp.where(top, pr * cr_t - pi_ * ci_t, -(pr * ci_t + pi_ * cr_t)).astype(BF16)

    t_full = _dot(xt, y2, _HI)
    rs = lax.shift_right_logical(lax.broadcasted_iota(jnp.int32, (S5_COLS, S5_COLS), 0), 4)
    cs = lax.shift_right_logical(lax.broadcasted_iota(jnp.int32, (S5_COLS, S5_COLS), 1), 4)
    t_ref[0] = jnp.where(rs <= cs, t_full, 0.0).astype(BF16)

    pr, pi_ = lbr, lbi
    for _ in range(S5_CHUNK.bit_length() - 1):
        pr, pi_ = pr * pr - pi_ * pi_, 2.0 * pr * pi_
    a_ref[0, 0:1, :] = pr
    a_ref[0, 1:2, :] = pi_


def _s5_params(lre, lim, ldt, b_re, b_im, c_re, c_im):
    g, p = lre.shape
    dup = lambda a: jnp.concatenate([a, a], axis=-1)
    lre_r, lim_r = dup(lre)[:, None, :], dup(lim)[:, None, :]
    lre_c, lim_c = dup(lre)[:, :, None], dup(lim)[:, :, None]
    bt_re = dup(jnp.swapaxes(b_re, 1, 2))
    bt_im = dup(jnp.swapaxes(b_im, 1, 2))
    ct = lambda c: jnp.concatenate([jnp.swapaxes(c, 1, 2)] * 2, axis=1)
    ct_re, ct_im = ct(c_re), ct(c_im)
    ldt3 = ldt[:, None, None]
    args = (lre_r, lim_r, lre_c, lim_c, ldt3, bt_re, bt_im, ct_re, ct_im)
    in_specs = [pl.BlockSpec((1,) + a.shape[1:], lambda i: (i, 0, 0)) for a in args]
    two_p = 2 * p
    return pl.pallas_call(
        _s5_param_kernel,
        grid=(g,),
        in_specs=in_specs,
        out_specs=[
            pl.BlockSpec((1, S5_COLS, S5_COLS), lambda i: (i, 0, 0)),
            pl.BlockSpec((1, S5_COLS, two_p), lambda i: (i, 0, 0)),
            pl.BlockSpec((1, two_p, S5_COLS), lambda i: (i, 0, 0)),
            pl.BlockSpec((1, 2, two_p), lambda i: (i, 0, 0)),
        ],
        out_shape=[
            jax.ShapeDtypeStruct((g, S5_COLS, S5_COLS), BF16),
            jax.ShapeDtypeStruct((g, S5_COLS, two_p), BF16),
            jax.ShapeDtypeStruct((g, two_p, S5_COLS), BF16),
            jax.ShapeDtypeStruct((g, 2, two_p), F32),
        ],
        compiler_params=pltpu.CompilerParams(vmem_limit_bytes=VMEM_LIMIT),
        name="s5_params",
    )(*args)


def _s5_scan_kernel(cpb, u_ref, t_ref, w_ref, m_ref, a_ref, d_ref, y_ref):
    u = u_ref[0]
    ub = u.astype(BF16)
    rows = u.shape[0]
    two_p = 2 * SSM_STATE
    h = _dot(ub, w_ref[0])
    ar = a_ref[0, 0:1, :]
    ai = a_ref[0, 1:2, :]
    lane = lax.broadcasted_iota(jnp.int32, (1, two_p), 1)
    sign = jnp.where(lane < SSM_STATE, -1.0, 1.0)
    chunk = lax.broadcasted_iota(jnp.int32, (rows, two_p), 0) % cpb
    step = 1
    while step < cpb:
        prev = jnp.where(chunk >= step, pltpu.roll(h, step, axis=0), 0.0)
        swapped = pltpu.roll(prev, SSM_STATE, axis=1)
        h = h + prev * ar + swapped * (ai * sign)
        ar, ai = ar * ar - ai * ai, 2.0 * ar * ai
        step *= 2
    hprev = jnp.where(chunk >= 1, pltpu.roll(h, 1, axis=0), 0.0)
    y = _dot(ub, t_ref[0]) + _dot(hprev.astype(BF16), m_ref[0])
    y_ref[0] = jax.nn.gelu(y + d_ref[0] * u)


def _s5_scan(ug, tmat, wmat, mmat, amat, dtile, cpb):
    g, rows, cols = ug.shape
    two_p = 2 * SSM_STATE
    idx = lambda i: (i, 0, 0)
    return pl.pallas_call(
        functools.partial(_s5_scan_kernel, cpb),
        grid=(g,),
        in_specs=[
            pl.BlockSpec((1, rows, cols), idx),
            pl.BlockSpec((1, cols, cols), idx),
            pl.BlockSpec((1, cols, two_p), idx),
            pl.BlockSpec((1, two_p, cols), idx),
            pl.BlockSpec((1, 2, two_p), idx),
            pl.BlockSpec((1, 1, cols), idx),
        ],
        out_specs=pl.BlockSpec((1, rows, cols), idx),
        out_shape=jax.ShapeDtypeStruct((g, rows, cols), F32),
        compiler_params=pltpu.CompilerParams(vmem_limit_bytes=VMEM_LIMIT),
        name="s5_scan",
    )(ug, tmat, wmat, mmat, amat, dtile)


def _plan_kernel(nb, tpb, cnt_ref, start_ref, tmap_ref, nused_ref):
    cnt = cnt_ref[0, 0]
    padded = jnp.floor((cnt + (MOBA_BLOCK - 1.0)) * (1.0 / MOBA_BLOCK)) * MOBA_BLOCK
    r_io = lax.broadcasted_iota(jnp.int32, (nb, nb), 0)
    c_io = lax.broadcasted_iota(jnp.int32, (nb, nb), 1)
    start = _dot((c_io < r_io).astype(F32), padded, _HI)
    start_ref[0, 0] = start
    end = (start + padded)[:, 0:1]
    tile_row = lax.broadcasted_iota(jnp.int32, (nb, tpb), 1).astype(F32) * MOBA_BLOCK
    blk_of_tile = jnp.sum(jnp.where(end <= tile_row, 1.0, 0.0), axis=0, keepdims=True)
    tmap_ref[0] = jnp.minimum(blk_of_tile, nb - 1.0).astype(jnp.int32)
    total = jnp.max(end, axis=0, keepdims=True)
    nused_ref[0] = jnp.broadcast_to(total * (1.0 / MOBA_BLOCK), (1, HEAD_PAD)).astype(jnp.int32)


def _plan(cnt, nb, tpb):
    bsz = cnt.shape[0]
    nbh = bsz * ATT_HEADS
    return pl.pallas_call(
        functools.partial(_plan_kernel, nb, tpb),
        grid=(bsz, ATT_HEADS),
        in_specs=[pl.BlockSpec((1, 1, nb, MOBA_BLOCK), lambda b, h: (b, h, 0, 0))],
        out_specs=[
            pl.BlockSpec((1, 1, nb, MOBA_BLOCK), lambda b, h: (b, h, 0, 0)),
            pl.BlockSpec((1, 1, tpb), lambda b, h: (b * ATT_HEADS + h, 0, 0)),
            pl.BlockSpec((1, 1, HEAD_PAD), lambda b, h: (b * ATT_HEADS + h, 0, 0)),
        ],
        out_shape=[
            jax.ShapeDtypeStruct((bsz, ATT_HEADS, nb, MOBA_BLOCK), F32),
            jax.ShapeDtypeStruct((nbh, 1, tpb), jnp.int32),
            jax.ShapeDtypeStruct((nbh, 1, HEAD_PAD), jnp.int32),
        ],
        name="route_plan",
    )(cnt)


def _pos_kernel(nb, cap, trash, sel_ref, rank_ref, start_ref, dst_ref):
    b = pl.program_id(0) // nb
    blk = lax.broadcasted_iota(jnp.int32, (nb, MOBA_BLOCK), 0)
    lane = lax.broadcasted_iota(jnp.int32, (1, MOBA_BLOCK), 1)
    for h in range(ATT_HEADS):
        start = start_ref[0, h]
        base = (b * ATT_HEADS + h) * cap
        for r in range(MOBA_TOPK):
            s = sel_ref[r, h:h + 1, :]
            first = jnp.sum(jnp.where(blk == s, start, 0.0), axis=0, keepdims=True).astype(jnp.int32)
            dst = base + first + rank_ref[r, h:h + 1, :]
            dst_ref[r, h:h + 1, :] = jnp.where(s >= 0, dst, trash + lane)


def _pos(sel, rank, start, nb, cap, trash):
    n = sel.shape[-1]
    blk3 = pl.BlockSpec((MOBA_TOPK, ATT_HEADS, MOBA_BLOCK), lambda i: (0, 0, i))
    return pl.pallas_call(
        functools.partial(_pos_kernel, nb, cap, trash),
        grid=(n // MOBA_BLOCK,),
        in_specs=[blk3, blk3,
                  pl.BlockSpec((1, ATT_HEADS, nb, MOBA_BLOCK), lambda i: (i // nb, 0, 0, 0))],
        out_specs=blk3,
        out_shape=jax.ShapeDtypeStruct(sel.shape, jnp.int32),
        name="route_pos",
    )(sel, rank, start)


def _sc_mesh():
    return plsc.VectorSubcoreMesh(core_axis_name="core", subcore_axis_name="subcore")


def _sc_scatter_rows(x, idx, rows_out):
    nrep, nin = idx.shape

    @pl.kernel(out_type=jax.ShapeDtypeStruct((rows_out, HEAD_PAD), x.dtype), mesh=_sc_mesh(),
               scratch_types=[])
    def scatter(x_hbm, i_hbm, o_hbm):
        def body(x_vmem, i_vmem):
            pltpu.sync_copy(x_vmem, o_hbm.at[i_vmem.at[0]])

        pltpu.emit_pipeline(
            body,
            grid=(nrep, nin // SC_WINDOW),
            in_specs=[pl.BlockSpec((SC_WINDOW, HEAD_PAD), lambda r, i: (i, 0)),
                      pl.BlockSpec((1, SC_WINDOW), lambda r, i: (r, i))],
            out_specs=[],
            core_axis_name=("core", "subcore"),
            dimension_semantics=(pltpu.PARALLEL, pltpu.PARALLEL),
        )(x_hbm, i_hbm)

    return scatter(x, idx)


def _sc_gather_rows(x, idx):
    nout = idx.shape[1]

    @pl.kernel(out_type=jax.ShapeDtypeStruct((nout, HEAD_PAD), x.dtype), mesh=_sc_mesh())
    def gather(x_hbm, i_hbm, o_hbm):
        def body(i_vmem, o_vmem):
            pltpu.sync_copy(x_hbm.at[i_vmem.at[0]], o_vmem)

        pltpu.emit_pipeline(
            body,
            grid=(nout // SC_WINDOW,),
            in_specs=[pl.BlockSpec((1, SC_WINDOW), lambda i: (0, i))],
            out_specs=[pl.BlockSpec((SC_WINDOW, HEAD_PAD), lambda i: (i, 0))],
            core_axis_name=("core", "subcore"),
            dimension_semantics=(pltpu.PARALLEL,),
        )(i_hbm, o_hbm)

    return gather(x, idx)


def _softmax_tile(z, vt):
    m = jnp.max(z, axis=0, keepdims=True)
    p = jnp.exp(z - m)
    l = jnp.sum(p, axis=0, keepdims=True)
    return m, l, _dot(vt, p.astype(BF16))


def _routed_kernel(nb, tmap_ref, nused_ref, qs_ref, slope_ref, *refs):
    k_refs = refs[:ROUTED_TILES]
    vt_refs = refs[ROUTED_TILES:2 * ROUTED_TILES]
    o_ref = refs[2 * ROUTED_TILES]
    bh = pl.program_id(0)
    s = pl.program_id(1)

    @pl.when(s * ROUTED_TILES < nused_ref[bh])
    def _():
        slope = slope_ref[pl.ds(bh % ATT_HEADS, 1), :]
        row = lax.broadcasted_iota(jnp.int32, (HEAD_DIM, MOBA_BLOCK), 0)
        for u in range(ROUTED_TILES):
            rs = slice(u * MOBA_BLOCK, (u + 1) * MOBA_BLOCK)
            j = tmap_ref[bh, s * ROUTED_TILES + u]
            z = _dot_nt(k_refs[u][...], qs_ref[rs, :].astype(BF16))
            m, l, ot = _softmax_tile(z, vt_refs[u][...])
            m = m + slope * jnp.full((1, MOBA_BLOCK), j * MOBA_BLOCK, jnp.int32).astype(F32)
            stats = jnp.where(row == 0, m, jnp.where(row == 1, l, 0.0))
            o_ref[rs, :] = jnp.concatenate([ot, stats], axis=0).T


def _routed(qs, ka, vt, slope_tab, tmap, nused, bsz, nb, tpb):
    nbh = bsz * ATT_HEADS
    steps = tpb // ROUTED_TILES
    rows = ROUTED_TILES * MOBA_BLOCK

    def step_of(bh, s, nu):
        last = jnp.maximum((nu[bh] + ROUTED_TILES - 1) // ROUTED_TILES - 1, 0)
        return bh * steps + jnp.minimum(s, last)

    def k_map(u):
        return lambda bh, s, tm, nu: ((bh // ATT_HEADS) * nb + tm[bh, s * ROUTED_TILES + u],
                                      bh % ATT_HEADS)

    def v_map(u):
        return lambda bh, s, tm, nu: (bh % ATT_HEADS,
                                      (bh // ATT_HEADS) * nb + tm[bh, s * ROUTED_TILES + u])

    grid_spec = pltpu.PrefetchScalarGridSpec(
        num_scalar_prefetch=2,
        grid=(nbh, steps),
        in_specs=(
            [pl.BlockSpec((rows, HEAD_PAD), lambda bh, s, tm, nu: (step_of(bh, s, nu), 0)),
             pl.BlockSpec(slope_tab.shape, lambda bh, s, tm, nu: (0, 0))]
            + [pl.BlockSpec((MOBA_BLOCK, HEAD_PAD), k_map(u)) for u in range(ROUTED_TILES)]
            + [pl.BlockSpec((HEAD_DIM, MOBA_BLOCK), v_map(u)) for u in range(ROUTED_TILES)]),
        out_specs=pl.BlockSpec((rows, HEAD_PAD), lambda bh, s, tm, nu: (step_of(bh, s, nu), 0)),
    )
    return pl.pallas_call(
        functools.partial(_routed_kernel, nb),
        grid_spec=grid_spec,
        out_shape=jax.ShapeDtypeStruct(qs.shape, F32),
        compiler_params=pltpu.CompilerParams(vmem_limit_bytes=VMEM_LIMIT),
        name="moba_routed",
    )(tmap, nused, qs, slope_tab, *([ka] * ROUTED_TILES), *([vt] * ROUTED_TILES))


def _combine_kernel(nb, q_ref, k_ref, vt_ref, sel_ref, g_ref, slope_ref, o_ref):
    il = pl.program_id(0) % nb
    kio = lax.broadcasted_iota(jnp.int32, (MOBA_BLOCK, MOBA_BLOCK), 0)
    qio = lax.broadcasted_iota(jnp.int32, (MOBA_BLOCK, MOBA_BLOCK), 1)
    causal = kio <= qio
    own_shift = jnp.full((1, MOBA_BLOCK), il * MOBA_BLOCK, jnp.int32).astype(F32)
    outs = []
    for h in range(ATT_HEADS):
        z = _dot_nt(k_ref[:, h * HEAD_PAD:(h + 1) * HEAD_PAD], q_ref[h].astype(BF16))
        z = jnp.where(causal, z, NEG_BIG)
        m0, l0, o0 = _softmax_tile(z, vt_ref[h * HEAD_DIM:(h + 1) * HEAD_DIM, :])
        m0 = m0 + slope_ref[h:h + 1, :] * own_shift
        parts = []
        for r in range(MOBA_TOPK):
            gt = g_ref[r, h].T
            valid = sel_ref[r, h:h + 1, :] >= 0
            parts.append((jnp.where(valid, gt[HEAD_DIM:HEAD_DIM + 1, :], NEG_BIG),
                          jnp.where(valid, gt[HEAD_DIM + 1:HEAD_DIM + 2, :], 0.0),
                          jnp.where(valid, gt[:HEAD_DIM, :], 0.0)))
        m = m0
        for mr, _, _ in parts:
            m = jnp.maximum(m, mr)
        w = jnp.exp(m0 - m)
        num, den = w * o0, w * l0
        for mr, lr, orr in parts:
            w = jnp.exp(mr - m)
            num, den = num + w * orr, den + w * lr
        outs.append(num / den)
    o_ref[...] = jnp.concatenate(outs, axis=0).T.astype(BF16)


def _combine(qhm, ka, vt, sel, g4, slope_tab, nb):
    n = ka.shape[0]
    return pl.pallas_call(
        functools.partial(_combine_kernel, nb),
        grid=(n // MOBA_BLOCK,),
        in_specs=[
            pl.BlockSpec((ATT_HEADS, MOBA_BLOCK, HEAD_PAD), lambda i: (0, i, 0)),
            pl.BlockSpec((MOBA_BLOCK, QK_PAD), lambda i: (i, 0)),
            pl.BlockSpec((ATT_WIDTH, MOBA_BLOCK), lambda i: (0, i)),
            pl.BlockSpec((MOBA_TOPK, ATT_HEADS, MOBA_BLOCK), lambda i: (0, 0, i)),
            pl.BlockSpec((MOBA_TOPK, ATT_HEADS, MOBA_BLOCK, HEAD_PAD), lambda i: (0, 0, i, 0)),
            pl.BlockSpec(slope_tab.shape, lambda i: (0, 0)),
        ],
        out_specs=pl.BlockSpec((MOBA_BLOCK, ATT_WIDTH), lambda i: (i, 0)),
        out_shape=jax.ShapeDtypeStruct((n, ATT_WIDTH), BF16),
        compiler_params=pltpu.CompilerParams(vmem_limit_bytes=VMEM_LIMIT),
        name="moba_own_combine",
    )(qhm, ka, vt, sel, g4, slope_tab)


def _merge_kernel(x_ref, bra_ref, ys_ref, brc_ref, wg_ref, wa_ref, wb_ref, wc_ref, gw_ref, gb_ref,
                  wo_ref, lg_ref, lb_ref, o_ref):
    x = x_ref[...]
    xb = x.astype(BF16)

    def gate(k):
        return jax.nn.sigmoid(_dot(xb, wg_ref[:, k * D_MODEL:(k + 1) * D_MODEL]))

    merged = gate(0) * _dot(bra_ref[...], wa_ref[...])
    ys = ys_ref[...]
    brb = ys * jax.nn.sigmoid(_dot(ys.astype(BF16), gw_ref[...]) + gb_ref[...])
    merged = merged + gate(1) * _dot(brb.astype(BF16), wb_ref[...])
    merged = merged + gate(2) * _dot(brc_ref[...], wc_ref[...])
    mix = _dot(merged.astype(BF16), wo_ref[...])
    o_ref[...] = _layer_norm(DN_ALPHA * x + mix, lg_ref[...], lb_ref[...])


def _merge(x2, bra, ys, brc, wg, wa, wb, wc, gw, gb, wo, lg, lb, tm):
    n = x2.shape[0]
    const = lambda i: (0, 0)
    tile = lambda w: pl.BlockSpec((tm, w), lambda i: (i, 0))
    full = lambda a: pl.BlockSpec(a.shape, const)
    return pl.pallas_call(
        _merge_kernel,
        grid=(n // tm,),
        in_specs=[tile(D_MODEL), tile(SGU_WIDTH), tile(SSM_WIDTH), tile(ATT_WIDTH),
                  full(wg), full(wa), full(wb), full(wc), full(gw), full(gb), full(wo), full(lg), full(lb)],
        out_specs=tile(D_MODEL),
        out_shape=jax.ShapeDtypeStruct((n, D_MODEL), F32),
        compiler_params=pltpu.CompilerParams(vmem_limit_bytes=VMEM_LIMIT),
        name="merge_ln",
    )(x2, bra, ys, brc, wg, wa, wb, wc, gw, gb, wo, lg, lb)


FF_CHUNK = D_FF // 2


def _ffn_kernel(x_ref, w1_ref, w3_ref, w2_ref, lg_ref, lb_ref, o_ref):
    x = x_ref[...]
    xb = x.astype(BF16)
    acc = None
    for c in range(D_FF // FF_CHUNK):
        cs = slice(c * FF_CHUNK, (c + 1) * FF_CHUNK)
        h = (jax.nn.silu(_dot(xb, w1_ref[:, cs])) * _dot(xb, w3_ref[:, cs])).astype(BF16)
        part = _dot(h, w2_ref[cs, :])
        acc = part if acc is None else acc + part
    o_ref[...] = _layer_norm(DN_ALPHA * x + acc, lg_ref[...], lb_ref[...])


def _ffn(x2, w1, w3, w2, lg, lb, tm):
    n = x2.shape[0]
    const = lambda i: (0, 0)
    full = lambda a: pl.BlockSpec(a.shape, const)
    return pl.pallas_call(
        _ffn_kernel,
        grid=(n // tm,),
        in_specs=[pl.BlockSpec((tm, D_MODEL), lambda i: (i, 0)),
                  full(w1), full(w3), full(w2), full(lg), full(lb)],
        out_specs=pl.BlockSpec((tm, D_MODEL), lambda i: (i, 0)),
        out_shape=jax.ShapeDtypeStruct((n, D_MODEL), F32),
        compiler_params=pltpu.CompilerParams(vmem_limit_bytes=VMEM_LIMIT),
        name="ffn_ln",
    )(x2, w1, w3, w2, lg, lb)


def _pad_heads(w):
    d = w.shape[0]
    w = w.reshape(d, ATT_HEADS, HEAD_DIM)
    return jnp.pad(w, ((0, 0), (0, 0), (0, HEAD_PAD - HEAD_DIM))).reshape(d, QK_PAD)


def _alibi_extras():
    slopes = 2.0 ** (-8.0 * jnp.arange(1, ATT_HEADS + 1, dtype=F32) / ATT_HEADS)
    row = jnp.arange(MOBA_BLOCK, dtype=F32)
    eq = jnp.zeros((MOBA_BLOCK, ATT_HEADS, HEAD_PAD), F32)
    eq = eq.at[:, :, Q_LANE_ONE].set(1.0).at[:, :, Q_LANE_ROW].set(row[:, None])
    ek = jnp.zeros((MOBA_BLOCK, ATT_HEADS, HEAD_PAD), F32)
    ek = ek.at[:, :, Q_LANE_ONE].set(row[:, None] * slopes[None, :])
    ek = ek.at[:, :, Q_LANE_BLK].set(-slopes[None, :] * MOBA_BLOCK).at[:, :, Q_LANE_ROW].set(-slopes[None, :])
    slope_tab = jnp.broadcast_to(slopes[:, None], (ATT_HEADS, MOBA_BLOCK))
    return eq.reshape(MOBA_BLOCK, QK_PAD), ek.reshape(MOBA_BLOCK, QK_PAD), slope_tab


def kernel(x, w_in, sgu_ln_g, sgu_ln_b, sgu_w, sgu_b, ssm_lambda_re, ssm_lambda_im, ssm_log_dt,
           ssm_b_re, ssm_b_im, ssm_c_re, ssm_c_im, ssm_d, glu_w, glu_b, w_branch_a, w_branch_b,
           w_branch_c, w_out, ln1_g, ln1_b, ffn_w1, ffn_w3, ffn_w2, ln2_g, ln2_b):
    bsz, seq, _ = x.shape
    n = bsz * seq
    nb = seq // MOBA_BLOCK
    cpb = seq // S5_CHUNK
    tm = 512 if n % 512 == 0 else MOBA_BLOCK
    eq, ek, slope_tab = _alibi_extras()
    scale = HEAD_DIM ** -0.5
    cap = (MOBA_TOPK + 1) * seq
    tpb = cap // MOBA_BLOCK
    trash = bsz * ATT_HEADS * cap
    o_q = 2 * SGU_WIDTH + SSM_WIDTH
    o_g = o_q + 3 * ATT_WIDTH

    x2 = x.reshape(n, D_MODEL)
    for l in range(DEPTH):
        wl = w_in[l]
        wz = wl[:, :o_q].astype(BF16)
        wq = _pad_heads(wl[:, o_q:o_q + ATT_WIDTH] * scale).astype(BF16)
        wk = _pad_heads(wl[:, o_q + ATT_WIDTH:o_q + 2 * ATT_WIDTH]).astype(BF16)
        wvt = wl[:, o_q + 2 * ATT_WIDTH:o_g].T.astype(BF16)
        wg = wl[:, o_g:].astype(BF16)

        bra, u, qhm, ka, vt, sel, rank, cnt = _inproj(
            x2, wz, wq, wk, wvt, eq, ek, sgu_ln_g[l][None, :], sgu_ln_b[l][None, :],
            sgu_w[l].astype(BF16), sgu_b[l].T, nb)

        tmat, wmat, mmat, amat = _s5_params(
            ssm_lambda_re[l], ssm_lambda_im[l], ssm_log_dt[l], ssm_b_re[l], ssm_b_im[l],
            ssm_c_re[l], ssm_c_im[l])
        ug = u.reshape(bsz * cpb, S5_CHUNK, SSM_GROUPS, SSM_GROUP).transpose(2, 0, 1, 3)
        ug = ug.reshape(SSM_GROUPS, bsz * cpb, S5_COLS)
        dtile = jnp.tile(ssm_d[l].reshape(SSM_GROUPS, 1, SSM_GROUP), (1, 1, S5_CHUNK))
        yg = _s5_scan(ug, tmat, wmat, mmat, amat, dtile, cpb)
        ys = yg.reshape(SSM_GROUPS, bsz * cpb, S5_CHUNK, SSM_GROUP).transpose(1, 2, 0, 3)
        ys = ys.reshape(n, SSM_WIDTH)

        start, tmap, nused = _plan(cnt, nb, tpb)
        dst = _pos(sel, rank, start, nb, cap, trash).reshape(MOBA_TOPK, ATT_HEADS * n)
        qs = _sc_scatter_rows(qhm.reshape(ATT_HEADS * n, HEAD_PAD), dst, trash + MOBA_BLOCK)
        part = _routed(qs, ka, vt, slope_tab, tmap[:, 0, :], nused[:, 0, 0], bsz, nb, tpb)
        g = _sc_gather_rows(part, dst.reshape(1, MOBA_TOPK * ATT_HEADS * n))
        brc = _combine(qhm, ka, vt, sel, g.reshape(MOBA_TOPK, ATT_HEADS, n, HEAD_PAD), slope_tab, nb)

        x2 = _merge(x2, bra, ys, brc, wg, w_branch_a[l].astype(BF16), w_branch_b[l].astype(BF16),
                    w_branch_c[l].astype(BF16), glu_w[l].astype(BF16), glu_b[l][None, :],
                    w_out[l].astype(BF16), ln1_g[l][None, :], ln1_b[l][None, :], tm)
        x2 = _ffn(x2, ffn_w1[l].astype(BF16), ffn_w3[l].astype(BF16), ffn_w2[l].astype(BF16),
                  ln2_g[l][None, :], ln2_b[l][None, :], tm)
    return x2.reshape(bsz, seq, D_MODEL)
```

```python
import functools

import jax
import jax.numpy as jnp
from jax import lax
from jax.experimental import pallas as pl
from jax.experimental.pallas import tpu as pltpu
from jax.experimental.pallas import tpu_sc as plsc

F32 = jnp.float32
BF16 = jnp.bfloat16

D_MODEL = 1024
SGU_CHUNK = 128
SGU_GROUPS = 4
SGU_WIDTH = 512
SSM_WIDTH = 512
SSM_GROUP = 16
SSM_GROUPS = 32
SSM_STATE = 64
ATT_HEADS = 8
HEAD_DIM = 64
ATT_WIDTH = 512
MOBA_BLOCK = 256
MOBA_TOPK = 3
D_FF = 2816
DEPTH = 2
DN_ALPHA = (2 * DEPTH) ** 0.25
LN_EPS = 1e-5
NEG_BIG = -1e30

HEAD_PAD = 128
QK_PAD = ATT_HEADS * HEAD_PAD
Q_LANE_ONE = HEAD_DIM
Q_LANE_BLK = HEAD_DIM + 1
Q_LANE_ROW = HEAD_DIM + 2
SC_WINDOW = 128
ROUTED_TILES = 8
S5_CHUNK = 64
S5_COLS = S5_CHUNK * SSM_GROUP
VMEM_LIMIT = 56 * 1024 * 1024

_HI = lax.Precision.HIGHEST


def _dot(a, b, precision=None):
    return jnp.dot(a, b, preferred_element_type=F32, precision=precision)


def _dot_nt(a, b, precision=None):
    return lax.dot_general(a, b, (((1,), (1,)), ((), ())),
                           preferred_element_type=F32, precision=precision)


def _layer_norm(x, g, b):
    mu = jnp.mean(x, axis=-1, keepdims=True)
    xc = x - mu
    var = jnp.mean(xc * xc, axis=-1, keepdims=True)
    return xc * lax.rsqrt(var + LN_EPS) * g + b


def _inproj_kernel(nb, x_ref, wz_ref, wq_ref, wk_ref, wvt_ref, eq_ref, ek_ref, lng_ref, lnb_ref,
                   sw_ref, sbt_ref, bra_ref, u_ref, q_ref, k_ref, vt_ref, sel_ref, rank_ref, cnt_ref,
                   kmean_ref, carry_ref):
    i = pl.program_id(0)
    il = i % nb

    @pl.when(i == 0)
    def _():
        kmean_ref[...] = jnp.zeros_like(kmean_ref)

    @pl.when(il == 0)
    def _():
        carry_ref[...] = jnp.zeros_like(carry_ref)

    xb = x_ref[...].astype(BF16)

    z = jax.nn.gelu(_dot(xb, wz_ref[:, : 2 * SGU_WIDTH]))
    u = z[:, :SGU_WIDTH]
    vn = _layer_norm(z[:, SGU_WIDTH:], lng_ref[...], lnb_ref[...]).astype(BF16)
    r_io = lax.broadcasted_iota(jnp.int32, (SGU_CHUNK, SGU_CHUNK), 0)
    c_io = lax.broadcasted_iota(jnp.int32, (SGU_CHUNK, SGU_CHUNK), 1)
    tril = r_io >= c_io
    for g in range(SGU_GROUPS):
        w = jnp.where(tril, sw_ref[g], jnp.zeros((), BF16))
        bias = sbt_ref[:, g:g + 1]
        gs = slice(g * SGU_CHUNK, (g + 1) * SGU_CHUNK)
        for c in range(MOBA_BLOCK // SGU_CHUNK):
            rs = slice(c * SGU_CHUNK, (c + 1) * SGU_CHUNK)
            mixed = _dot(w, vn[rs, gs]) + bias
            bra_ref[rs, gs] = (u[rs, gs] * mixed).astype(BF16)

    u_ref[...] = _dot(xb, wz_ref[:, 2 * SGU_WIDTH:])

    lane = lax.broadcasted_iota(jnp.int32, (1, QK_PAD), 1) & (HEAD_PAD - 1)
    qa = _dot(xb, wq_ref[...]) + eq_ref[...] + jnp.where(lane == Q_LANE_BLK, il.astype(F32), 0.0)
    for h in range(ATT_HEADS):
        q_ref[h] = qa[:, h * HEAD_PAD:(h + 1) * HEAD_PAD]
    ka = _dot(xb, wk_ref[...])
    k_ref[...] = (ka + ek_ref[...]).astype(BF16)
    kmean_ref[pl.ds(il, 1), :] = jnp.mean(ka, axis=0, keepdims=True)
    vt_ref[...] = _dot_nt(wvt_ref[...], xb).astype(BF16)

    blk = lax.broadcasted_iota(jnp.int32, (nb, MOBA_BLOCK), 0)
    neg_inf = jnp.full((), -jnp.inf, F32)
    r_io = lax.broadcasted_iota(jnp.int32, (MOBA_BLOCK, MOBA_BLOCK), 0)
    c_io = lax.broadcasted_iota(jnp.int32, (MOBA_BLOCK, MOBA_BLOCK), 1)
    earlier = (r_io < c_io).astype(BF16)
    for h in range(ATT_HEADS):
        hs = slice(h * HEAD_PAD, h * HEAD_PAD + HEAD_DIM)
        gate = _dot_nt(kmean_ref[:, hs], qa[:, hs], precision=_HI)
        gate = jnp.where(blk < il, gate, neg_inf)
        sels = []
        for r in range(MOBA_TOPK):
            m = jnp.max(gate, axis=0, keepdims=True)
            idx = jnp.min(jnp.where(gate == m, blk, nb), axis=0, keepdims=True)
            sels.append(jnp.where(r < il, idx, -1))
            gate = jnp.where(blk == idx, neg_inf, gate)
        hit = [blk == s for s in sels]
        onehot = jnp.where(hit[0] | hit[1] | hit[2], 1.0, 0.0)
        before = carry_ref[h] + _dot(onehot.astype(BF16), earlier)
        carry_ref[h] = carry_ref[h] + jnp.sum(onehot, axis=1, keepdims=True)
        for r in range(MOBA_TOPK):
            sel_ref[r, h:h + 1, :] = sels[r]
            rank = jnp.sum(jnp.where(hit[r], before, 0.0), axis=0, keepdims=True)
            rank_ref[r, h:h + 1, :] = rank.astype(jnp.int32)

    @pl.when(il == nb - 1)
    def _():
        cnt_ref[0] = carry_ref[...]


def _inproj(x2, wz, wq, wk, wvt, eq, ek, lng, lnb, sw, sbt, nb):
    n = x2.shape[0]
    grid = n // MOBA_BLOCK
    const = lambda i: (0, 0)
    return pl.pallas_call(
        functools.partial(_inproj_kernel, nb),
        grid=(grid,),
        in_specs=[
            pl.BlockSpec((MOBA_BLOCK, D_MODEL), lambda i: (i, 0)),
            pl.BlockSpec(wz.shape, const),
            pl.BlockSpec(wq.shape, const),
            pl.BlockSpec(wk.shape, const),
            pl.BlockSpec(wvt.shape, const),
            pl.BlockSpec(eq.shape, const),
            pl.BlockSpec(ek.shape, const),
            pl.BlockSpec(lng.shape, const),
            pl.BlockSpec(lnb.shape, const),
            pl.BlockSpec(sw.shape, lambda i: (0, 0, 0)),
            pl.BlockSpec(sbt.shape, const),
        ],
        out_specs=[
            pl.BlockSpec((MOBA_BLOCK, SGU_WIDTH), lambda i: (i, 0)),
            pl.BlockSpec((MOBA_BLOCK, SSM_WIDTH), lambda i: (i, 0)),
            pl.BlockSpec((ATT_HEADS, MOBA_BLOCK, HEAD_PAD), lambda i: (0, i, 0)),
            pl.BlockSpec((MOBA_BLOCK, QK_PAD), lambda i: (i, 0)),
            pl.BlockSpec((ATT_WIDTH, MOBA_BLOCK), lambda i: (0, i)),
            pl.BlockSpec((MOBA_TOPK, ATT_HEADS, MOBA_BLOCK), lambda i: (0, 0, i)),
            pl.BlockSpec((MOBA_TOPK, ATT_HEADS, MOBA_BLOCK), lambda i: (0, 0, i)),
            pl.BlockSpec((1, ATT_HEADS, nb, MOBA_BLOCK), lambda i: (i // nb, 0, 0, 0)),
        ],
        out_shape=[
            jax.ShapeDtypeStruct((n, SGU_WIDTH), BF16),
            jax.ShapeDtypeStruct((n, SSM_WIDTH), F32),
            jax.ShapeDtypeStruct((ATT_HEADS, n, HEAD_PAD), F32),
            jax.ShapeDtypeStruct((n, QK_PAD), BF16),
            jax.ShapeDtypeStruct((ATT_WIDTH, n), BF16),
            jax.ShapeDtypeStruct((MOBA_TOPK, ATT_HEADS, n), jnp.int32),
            jax.ShapeDtypeStruct((MOBA_TOPK, ATT_HEADS, n), jnp.int32),
            jax.ShapeDtypeStruct((n // (nb * MOBA_BLOCK), ATT_HEADS, nb, MOBA_BLOCK), F32),
        ],
        scratch_shapes=[pltpu.VMEM((nb, QK_PAD), F32),
                        pltpu.VMEM((ATT_HEADS, nb, MOBA_BLOCK), F32)],
        compiler_params=pltpu.CompilerParams(
            dimension_semantics=("arbitrary",), vmem_limit_bytes=VMEM_LIMIT),
        name="inproj_sgu_gate",
    )(x2, wz, wq, wk, wvt, eq, ek, lng, lnb, sw, sbt)


def _s5_param_kernel(lre_r, lim_r, lre_c, lim_c, ldt, bt_re, bt_im, ct_re, ct_im,
                     t_ref, w_ref, m_ref, a_ref):
    two_p = 2 * SSM_STATE
    dt = jnp.exp(ldt[0])

    def powers(ar, ai, e):
        mag = jnp.exp(ar * e)
        return mag * jnp.cos(ai * e), mag * jnp.sin(ai * e)

    lr, li = lre_r[0], lim_r[0]
    ar, ai = lr * dt, li * dt
    lbr, lbi = powers(ar, ai, 1.0)
    den = lr * lr + li * li
    cfr = ((lbr - 1.0) * lr + lbi * li) / den
    cfi = (lbi * lr - (lbr - 1.0) * li) / den
    bbr = cfr * bt_re[0] - cfi * bt_im[0]
    bbi = cfr * bt_im[0] + cfi * bt_re[0]

    s_col = lax.broadcasted_iota(jnp.int32, (S5_CHUNK, two_p), 0).astype(F32)
    neg_r, neg_i = powers(ar, ai, -s_col)
    rev_r, rev_i = powers(ar, ai, (S5_CHUNK - 1.0) - s_col)

    row = lax.broadcasted_iota(jnp.int32, (S5_COLS, S5_CHUNK), 0)
    col = lax.broadcasted_iota(jnp.int32, (S5_COLS, S5_CHUNK), 1)
    exp_rows = (lax.shift_right_logical(row, 4) == col).astype(F32)
    row = lax.broadcasted_iota(jnp.int32, (S5_COLS, SSM_GROUP), 0)
    col = lax.broadcasted_iota(jnp.int32, (S5_COLS, SSM_GROUP), 1)
    tile_rows = ((row & (SSM_GROUP - 1)) == col).astype(F32)
    bbr_t = _dot(tile_rows, bbr, _HI)
    bbi_t = _dot(tile_rows, bbi, _HI)
    lane = lax.broadcasted_iota(jnp.int32, (S5_COLS, two_p), 1)
    first = lane < SSM_STATE

    pr, pi_ = _dot(exp_rows, neg_r, _HI), _dot(exp_rows, neg_i, _HI)
    xt = jnp.where(first, pr * bbr_t - pi_ * bbi_t, -(pr * bbi_t + pi_ * bbr_t))
    pr, pi_ = _dot(exp_rows, rev_r, _HI), _dot(exp_rows, rev_i, _HI)
    w_ref[0] = jnp.where(first, pr * bbr_t - pi_ * bbi_t, pr * bbi_t + pi_ * bbr_t).astype(BF16)

    lrc, lic = lre_c[0], lim_c[0]
    arc, aic = lrc * dt, lic * dt
    t_row = lax.broadcasted_iota(jnp.int32, (two_p, S5_CHUNK), 1).astype(F32)
    yr, yi = powers(arc, aic, t_row)
    er, ei = powers(arc, aic, t_row + 1.0)
    row = lax.broadcasted_iota(jnp.int32, (S5_CHUNK, S5_COLS), 0)
    col = lax.broadcasted_iota(jnp.int32, (S5_CHUNK, S5_COLS), 1)
    exp_cols = (lax.shift_right_logical(col, 4) == row).astype(F32)
    row = lax.broadcasted_iota(jnp.int32, (SSM_GROUP, S5_COLS), 0)
    col = lax.broadcasted_iota(jnp.int32, (SSM_GROUP, S5_COLS), 1)
    tile_cols = ((col & (SSM_GROUP - 1)) == row).astype(F32)
    cr_t = _dot(ct_re[0], tile_cols, _HI)
    ci_t = _dot(ct_im[0], tile_cols, _HI)
    rowp = lax.broadcasted_iota(jnp.int32, (two_p, S5_COLS), 0)
    top = rowp < SSM_STATE

    pr, pi_ = _dot(yr, exp_cols, _HI), _dot(yi, exp_cols, _HI)
    y2 = jnp.where(top, pr * cr_t - pi_ * ci_t, pr * ci_t + pi_ * cr_t)
    pr, pi_ = _dot(er, exp_cols, _HI), _dot(ei, exp_cols, _HI)
    m_ref[0] = jnp.where(top, pr * cr_t - pi_ * ci_t, -(pr * ci_t + pi_ * cr_t)).astype(BF16)

    t_full = _dot(xt, y2, _HI)
    rs = lax.shift_right_logical(lax.broadcasted_iota(jnp.int32, (S5_COLS, S5_COLS), 0), 4)
    cs = lax.shift_right_logical(lax.broadcasted_iota(jnp.int32, (S5_COLS, S5_COLS), 1), 4)
    t_ref[0] = jnp.where(rs <= cs, t_full, 0.0).astype(BF16)

    pr, pi_ = lbr, lbi
    for _ in range(S5_CHUNK.bit_length() - 1):
        pr, pi_ = pr * pr - pi_ * pi_, 2.0 * pr * pi_
    a_ref[0, 0:1, :] = pr
    a_ref[0, 1:2, :] = pi_


def _s5_params(lre, lim, ldt, b_re, b_im, c_re, c_im):
    g, p = lre.shape
    dup = lambda a: jnp.concatenate([a, a], axis=-1)
    lre_r, lim_r = dup(lre)[:, None, :], dup(lim)[:, None, :]
    lre_c, lim_c = dup(lre)[:, :, None], dup(lim)[:, :, None]
    bt_re = dup(jnp.swapaxes(b_re, 1, 2))
    bt_im = dup(jnp.swapaxes(b_im, 1, 2))
    ct = lambda c: jnp.concatenate([jnp.swapaxes(c, 1, 2)] * 2, axis=1)
    ct_re, ct_im = ct(c_re), ct(c_im)
    ldt3 = ldt[:, None, None]
    args = (lre_r, lim_r, lre_c, lim_c, ldt3, bt_re, bt_im, ct_re, ct_im)
    in_specs = [pl.BlockSpec((1,) + a.shape[1:], lambda i: (i, 0, 0)) for a in args]
    two_p = 2 * p
    return pl.pallas_call(
        _s5_param_kernel,
        grid=(g,),
        in_specs=in_specs,
        out_specs=[
            pl.BlockSpec((1, S5_COLS, S5_COLS), lambda i: (i, 0, 0)),
            pl.BlockSpec((1, S5_COLS, two_p), lambda i: (i, 0, 0)),
            pl.BlockSpec((1, two_p, S5_COLS), lambda i: (i, 0, 0)),
            pl.BlockSpec((1, 2, two_p), lambda i: (i, 0, 0)),
        ],
        out_shape=[
            jax.ShapeDtypeStruct((g, S5_COLS, S5_COLS), BF16),
            jax.ShapeDtypeStruct((g, S5_COLS, two_p), BF16),
            jax.ShapeDtypeStruct((g, two_p, S5_COLS), BF16),
            jax.ShapeDtypeStruct((g, 2, two_p), F32),
        ],
        compiler_params=pltpu.CompilerParams(vmem_limit_bytes=VMEM_LIMIT),
        name="s5_params",
    )(*args)


def _s5_scan_kernel(cpb, u_ref, t_ref, w_ref, m_ref, a_ref, d_ref, y_ref):
    u = u_ref[0]
    ub = u.astype(BF16)
    rows = u.shape[0]
    two_p = 2 * SSM_STATE
    h = _dot(ub, w_ref[0])
    ar = a_ref[0, 0:1, :]
    ai = a_ref[0, 1:2, :]
    lane = lax.broadcasted_iota(jnp.int32, (1, two_p), 1)
    sign = jnp.where(lane < SSM_STATE, -1.0, 1.0)
    chunk = lax.broadcasted_iota(jnp.int32, (rows, two_p), 0) % cpb
    step = 1
    while step < cpb:
        prev = jnp.where(chunk >= step, pltpu.roll(h, step, axis=0), 0.0)
        swapped = pltpu.roll(prev, SSM_STATE, axis=1)
        h = h + prev * ar + swapped * (ai * sign)
        ar, ai = ar * ar - ai * ai, 2.0 * ar * ai
        step *= 2
    hprev = jnp.where(chunk >= 1, pltpu.roll(h, 1, axis=0), 0.0)
    y = _dot(ub, t_ref[0]) + _dot(hprev.astype(BF16), m_ref[0])
    y_ref[0] = jax.nn.gelu(y + d_ref[0] * u)


def _s5_scan(ug, tmat, wmat, mmat, amat, dtile, cpb):
    g, rows, cols = ug.shape
    two_p = 2 * SSM_STATE
    idx = lambda i: (i, 0, 0)
    return pl.pallas_call(
        functools.partial(_s5_scan_kernel, cpb),
        grid=(g,),
        in_specs=[
            pl.BlockSpec((1, rows, cols), idx),
            pl.BlockSpec((1, cols, cols), idx),
            pl.BlockSpec((1, cols, two_p), idx),
            pl.BlockSpec((1, two_p, cols), idx),
            pl.BlockSpec((1, 2, two_p), idx),
            pl.BlockSpec((1, 1, cols), idx),
        ],
        out_specs=pl.BlockSpec((1, rows, cols), idx),
        out_shape=jax.ShapeDtypeStruct((g, rows, cols), F32),
        compiler_params=pltpu.CompilerParams(vmem_limit_bytes=VMEM_LIMIT),
        name="s5_scan",
    )(ug, tmat, wmat, mmat, amat, dtile)


def _plan_kernel(nb, tpb, cnt_ref, start_ref, tmap_ref, nused_ref):
    cnt = cnt_ref[0, 0]
    padded = jnp.floor((cnt + (MOBA_BLOCK - 1.0)) * (1.0 / MOBA_BLOCK)) * MOBA_BLOCK
    r_io = lax.broadcasted_iota(jnp.int32, (nb, nb), 0)
    c_io = lax.broadcasted_iota(jnp.int32, (nb, nb), 1)
    start = _dot((c_io < r_io).astype(F32), padded, _HI)
    start_ref[0, 0] = start
    end = (start + padded)[:, 0:1]
    tile_row = lax.broadcasted_iota(jnp.int32, (nb, tpb), 1).astype(F32) * MOBA_BLOCK
    blk_of_tile = jnp.sum(jnp.where(end <= tile_row, 1.0, 0.0), axis=0, keepdims=True)
    tmap_ref[0] = jnp.minimum(blk_of_tile, nb - 1.0).astype(jnp.int32)
    total = jnp.max(end, axis=0, keepdims=True)
    nused_ref[0] = jnp.broadcast_to(total * (1.0 / MOBA_BLOCK), (1, HEAD_PAD)).astype(jnp.int32)


def _plan(cnt, nb, tpb):
    bsz = cnt.shape[0]
    nbh = bsz * ATT_HEADS
    return pl.pallas_call(
        functools.partial(_plan_kernel, nb, tpb),
        grid=(bsz, ATT_HEADS),
        in_specs=[pl.BlockSpec((1, 1, nb, MOBA_BLOCK), lambda b, h: (b, h, 0, 0))],
        out_specs=[
            pl.BlockSpec((1, 1, nb, MOBA_BLOCK), lambda b, h: (b, h, 0, 0)),
            pl.BlockSpec((1, 1, tpb), lambda b, h: (b * ATT_HEADS + h, 0, 0)),
            pl.BlockSpec((1, 1, HEAD_PAD), lambda b, h: (b * ATT_HEADS + h, 0, 0)),
        ],
        out_shape=[
            jax.ShapeDtypeStruct((bsz, ATT_HEADS, nb, MOBA_BLOCK), F32),
            jax.ShapeDtypeStruct((nbh, 1, tpb), jnp.int32),
            jax.ShapeDtypeStruct((nbh, 1, HEAD_PAD), jnp.int32),
        ],
        name="route_plan",
    )(cnt)


def _pos_kernel(nb, cap, trash, sel_ref, rank_ref, start_ref, dst_ref):
    b = pl.program_id(0) // nb
    blk = lax.broadcasted_iota(jnp.int32, (nb, MOBA_BLOCK), 0)
    lane = lax.broadcasted_iota(jnp.int32, (1, MOBA_BLOCK), 1)
    for h in range(ATT_HEADS):
        start = start_ref[0, h]
        base = (b * ATT_HEADS + h) * cap
        for r in range(MOBA_TOPK):
            s = sel_ref[r, h:h + 1, :]
            first = jnp.sum(jnp.where(blk == s, start, 0.0), axis=0, keepdims=True).astype(jnp.int32)
            dst = base + first + rank_ref[r, h:h + 1, :]
            dst_ref[r, h:h + 1, :] = jnp.where(s >= 0, dst, trash + lane)


def _pos(sel, rank, start, nb, cap, trash):
    n = sel.shape[-1]
    blk3 = pl.BlockSpec((MOBA_TOPK, ATT_HEADS, MOBA_BLOCK), lambda i: (0, 0, i))
    return pl.pallas_call(
        functools.partial(_pos_kernel, nb, cap, trash),
        grid=(n // MOBA_BLOCK,),
        in_specs=[blk3, blk3,
                  pl.BlockSpec((1, ATT_HEADS, nb, MOBA_BLOCK), lambda i: (i // nb, 0, 0, 0))],
        out_specs=blk3,
        out_shape=jax.ShapeDtypeStruct(sel.shape, jnp.int32),
        name="route_pos",
    )(sel, rank, start)


def _sc_mesh():
    return plsc.VectorSubcoreMesh(core_axis_name="core", subcore_axis_name="subcore")


def _sc_scatter_rows(x, idx, rows_out):
    nrep, nin = idx.shape

    @pl.kernel(out_type=jax.ShapeDtypeStruct((rows_out, HEAD_PAD), x.dtype), mesh=_sc_mesh(),
               scratch_types=[])
    def scatter(x_hbm, i_hbm, o_hbm):
        def body(x_vmem, i_vmem):
            pltpu.sync_copy(x_vmem, o_hbm.at[i_vmem.at[0]])

        pltpu.emit_pipeline(
            body,
            grid=(nrep, nin // SC_WINDOW),
            in_specs=[pl.BlockSpec((SC_WINDOW, HEAD_PAD), lambda r, i: (i, 0)),
                      pl.BlockSpec((1, SC_WINDOW), lambda r, i: (r, i))],
            out_specs=[],
            core_axis_name=("core", "subcore"),
            dimension_semantics=(pltpu.PARALLEL, pltpu.PARALLEL),
        )(x_hbm, i_hbm)

    return scatter(x, idx)


def _sc_gather_rows(x, idx):
    nout = idx.shape[1]

    @pl.kernel(out_type=jax.ShapeDtypeStruct((nout, HEAD_PAD), x.dtype), mesh=_sc_mesh())
    def gather(x_hbm, i_hbm, o_hbm):
        def body(i_vmem, o_vmem):
            pltpu.sync_copy(x_hbm.at[i_vmem.at[0]], o_vmem)

        pltpu.emit_pipeline(
            body,
            grid=(nout // SC_WINDOW,),
            in_specs=[pl.BlockSpec((1, SC_WINDOW), lambda i: (0, i))],
            out_specs=[pl.BlockSpec((SC_WINDOW, HEAD_PAD), lambda i: (i, 0))],
            core_axis_name=("core", "subcore"),
            dimension_semantics=(pltpu.PARALLEL,),
        )(i_hbm, o_hbm)

    return gather(x, idx)


def _softmax_tile(z, vt):
    m = jnp.max(z, axis=0, keepdims=True)
    p = jnp.exp(z - m)
    l = jnp.sum(p, axis=0, keepdims=True)
    return m, l, _dot(vt, p.astype(BF16))


def _routed_kernel(tmap_ref, nused_ref, qs_ref, slope_ref, k_ref, vt_ref, o_ref):
    h = pl.program_id(1)
    bh = pl.program_id(0) * ATT_HEADS + h
    s = pl.program_id(2)

    @pl.when(s * ROUTED_TILES < nused_ref[bh])
    def _():
        slope = slope_ref[pl.ds(h, 1), :]
        row = lax.broadcasted_iota(jnp.int32, (HEAD_DIM, MOBA_BLOCK), 0)
        tiles = range(ROUTED_TILES)
        js = [tmap_ref[bh, s * ROUTED_TILES + u] for u in tiles]
        ks = [pl.multiple_of(j * MOBA_BLOCK, MOBA_BLOCK) for j in js]
        zs = [_dot_nt(k_ref[pl.ds(ks[u], MOBA_BLOCK), :],
                      qs_ref[u * MOBA_BLOCK:(u + 1) * MOBA_BLOCK, :].astype(BF16)) for u in tiles]
        ms = [jnp.max(z, axis=0, keepdims=True) for z in zs]
        ps = [jnp.exp(zs[u] - ms[u]) for u in tiles]
        ls = [jnp.sum(p, axis=0, keepdims=True) for p in ps]
        ots = [_dot(vt_ref[:, pl.ds(ks[u], MOBA_BLOCK)], ps[u].astype(BF16)) for u in tiles]
        for u in tiles:
            m = ms[u] + slope * jnp.full((1, MOBA_BLOCK), ks[u], jnp.int32).astype(F32)
            stats = jnp.where(row == 0, m, jnp.where(row == 1, ls[u], 0.0))
            o_ref[u * MOBA_BLOCK:(u + 1) * MOBA_BLOCK, :] = jnp.concatenate([ots[u], stats], axis=0).T


def _routed(qs, ka, vt, slope_tab, tmap, nused, bsz, seq, tpb):
    steps = tpb // ROUTED_TILES
    rows = ROUTED_TILES * MOBA_BLOCK
    shift = ROUTED_TILES.bit_length() - 1

    def step_of(b, h, s, tm, nu):
        bh = b * ATT_HEADS + h
        used = lax.shift_right_logical(nu[bh] + (ROUTED_TILES - 1), shift)
        return bh * steps + jnp.minimum(s, jnp.maximum(used - 1, 0)), 0

    grid_spec = pltpu.PrefetchScalarGridSpec(
        num_scalar_prefetch=2,
        grid=(bsz, ATT_HEADS, steps),
        in_specs=[
            pl.BlockSpec((rows, HEAD_PAD), step_of),
            pl.BlockSpec(slope_tab.shape, lambda b, h, s, tm, nu: (0, 0)),
            pl.BlockSpec((seq, HEAD_PAD), lambda b, h, s, tm, nu: (b, h)),
            pl.BlockSpec((HEAD_DIM, seq), lambda b, h, s, tm, nu: (h, b)),
        ],
        out_specs=pl.BlockSpec((rows, HEAD_PAD), step_of),
    )
    return pl.pallas_call(
        _routed_kernel,
        grid_spec=grid_spec,
        out_shape=jax.ShapeDtypeStruct(qs.shape, F32),
        compiler_params=pltpu.CompilerParams(vmem_limit_bytes=VMEM_LIMIT),
        name="moba_routed",
    )(tmap, nused, qs, slope_tab, ka, vt)


def _combine_kernel(nb, q_ref, k_ref, vt_ref, sel_ref, g_ref, slope_ref, o_ref):
    il = pl.program_id(0) % nb
    kio = lax.broadcasted_iota(jnp.int32, (MOBA_BLOCK, MOBA_BLOCK), 0)
    qio = lax.broadcasted_iota(jnp.int32, (MOBA_BLOCK, MOBA_BLOCK), 1)
    causal = kio <= qio
    own_shift = jnp.full((1, MOBA_BLOCK), il * MOBA_BLOCK, jnp.int32).astype(F32)
    outs = []
    heads = range(ATT_HEADS)
    zs = [_dot_nt(k_ref[:, h * HEAD_PAD:(h + 1) * HEAD_PAD], q_ref[h].astype(BF16)) for h in heads]
    zs = [jnp.where(causal, z, NEG_BIG) for z in zs]
    ms = [jnp.max(z, axis=0, keepdims=True) for z in zs]
    ps = [jnp.exp(zs[h] - ms[h]) for h in heads]
    ls = [jnp.sum(p, axis=0, keepdims=True) for p in ps]
    os_ = [_dot(vt_ref[h * HEAD_DIM:(h + 1) * HEAD_DIM, :], ps[h].astype(BF16)) for h in heads]
    for h in heads:
        l0, o0 = ls[h], os_[h]
        m0 = ms[h] + slope_ref[h:h + 1, :] * own_shift
        parts = []
        for r in range(MOBA_TOPK):
            gt = g_ref[r, h].T
            valid = sel_ref[r, h:h + 1, :] >= 0
            parts.append((jnp.where(valid, gt[HEAD_DIM:HEAD_DIM + 1, :], NEG_BIG),
                          jnp.where(valid, gt[HEAD_DIM + 1:HEAD_DIM + 2, :], 0.0),
                          jnp.where(valid, gt[:HEAD_DIM, :], 0.0)))
        m = m0
        for mr, _, _ in parts:
            m = jnp.maximum(m, mr)
        w = jnp.exp(m0 - m)
        num, den = w * o0, w * l0
        for mr, lr, orr in parts:
            w = jnp.exp(mr - m)
            num, den = num + w * orr, den + w * lr
        outs.append(num / den)
    o_ref[...] = jnp.concatenate(outs, axis=0).T.astype(BF16)


def _combine(qhm, ka, vt, sel, g4, slope_tab, nb):
    n = ka.shape[0]
    return pl.pallas_call(
        functools.partial(_combine_kernel, nb),
        grid=(n // MOBA_BLOCK,),
        in_specs=[
            pl.BlockSpec((ATT_HEADS, MOBA_BLOCK, HEAD_PAD), lambda i: (0, i, 0)),
            pl.BlockSpec((MOBA_BLOCK, QK_PAD), lambda i: (i, 0)),
            pl.BlockSpec((ATT_WIDTH, MOBA_BLOCK), lambda i: (0, i)),
            pl.BlockSpec((MOBA_TOPK, ATT_HEADS, MOBA_BLOCK), lambda i: (0, 0, i)),
            pl.BlockSpec((MOBA_TOPK, ATT_HEADS, MOBA_BLOCK, HEAD_PAD), lambda i: (0, 0, i, 0)),
            pl.BlockSpec(slope_tab.shape, lambda i: (0, 0)),
        ],
        out_specs=pl.BlockSpec((MOBA_BLOCK, ATT_WIDTH), lambda i: (i, 0)),
        out_shape=jax.ShapeDtypeStruct((n, ATT_WIDTH), BF16),
        compiler_params=pltpu.CompilerParams(vmem_limit_bytes=VMEM_LIMIT),
        name="moba_own_combine",
    )(qhm, ka, vt, sel, g4, slope_tab)


def _merge_kernel(x_ref, bra_ref, ys_ref, brc_ref, wg_ref, wa_ref, wb_ref, wc_ref, gw_ref, gb_ref,
                  wo_ref, lg_ref, lb_ref, o_ref):
    x = x_ref[...]
    xb = x.astype(BF16)

    def gate(k):
        return jax.nn.sigmoid(_dot(xb, wg_ref[:, k * D_MODEL:(k + 1) * D_MODEL]))

    merged = gate(0) * _dot(bra_ref[...], wa_ref[...])
    ys = ys_ref[...]
    brb = ys * jax.nn.sigmoid(_dot(ys.astype(BF16), gw_ref[...]) + gb_ref[...])
    merged = merged + gate(1) * _dot(brb.astype(BF16), wb_ref[...])
    merged = merged + gate(2) * _dot(brc_ref[...], wc_ref[...])
    mix = _dot(merged.astype(BF16), wo_ref[...])
    o_ref[...] = _layer_norm(DN_ALPHA * x + mix, lg_ref[...], lb_ref[...])


def _merge(x2, bra, ys, brc, wg, wa, wb, wc, gw, gb, wo, lg, lb, tm):
    n = x2.shape[0]
    const = lambda i: (0, 0)
    tile = lambda w: pl.BlockSpec((tm, w), lambda i: (i, 0))
    full = lambda a: pl.BlockSpec(a.shape, const)
    return pl.pallas_call(
        _merge_kernel,
        grid=(n // tm,),
        in_specs=[tile(D_MODEL), tile(SGU_WIDTH), tile(SSM_WIDTH), tile(ATT_WIDTH),
                  full(wg), full(wa), full(wb), full(wc), full(gw), full(gb), full(wo), full(lg), full(lb)],
        out_specs=tile(D_MODEL),
        out_shape=jax.ShapeDtypeStruct((n, D_MODEL), F32),
        compiler_params=pltpu.CompilerParams(vmem_limit_bytes=VMEM_LIMIT),
        name="merge_ln",
    )(x2, bra, ys, brc, wg, wa, wb, wc, gw, gb, wo, lg, lb)


FF_CHUNK = D_FF // 2


def _ffn_kernel(x_ref, w1_ref, w3_ref, w2_ref, lg_ref, lb_ref, o_ref):
    x = x_ref[...]
    xb = x.astype(BF16)
    acc = None
    for c in range(D_FF // FF_CHUNK):
        cs = slice(c * FF_CHUNK, (c + 1) * FF_CHUNK)
        h = (jax.nn.silu(_dot(xb, w1_ref[:, cs])) * _dot(xb, w3_ref[:, cs])).astype(BF16)
        part = _dot(h, w2_ref[cs, :])
        acc = part if acc is None else acc + part
    o_ref[...] = _layer_norm(DN_ALPHA * x + acc, lg_ref[...], lb_ref[...])


def _ffn(x2, w1, w3, w2, lg, lb, tm):
    n = x2.shape[0]
    const = lambda i: (0, 0)
    full = lambda a: pl.BlockSpec(a.shape, const)
    return pl.pallas_call(
        _ffn_kernel,
        grid=(n // tm,),
        in_specs=[pl.BlockSpec((tm, D_MODEL), lambda i: (i, 0)),
                  full(w1), full(w3), full(w2), full(lg), full(lb)],
        out_specs=pl.BlockSpec((tm, D_MODEL), lambda i: (i, 0)),
        out_shape=jax.ShapeDtypeStruct((n, D_MODEL), F32),
        compiler_params=pltpu.CompilerParams(vmem_limit_bytes=VMEM_LIMIT),
        name="ffn_ln",
    )(x2, w1, w3, w2, lg, lb)


def _pad_heads(w):
    d = w.shape[0]
    w = w.reshape(d, ATT_HEADS, HEAD_DIM)
    return jnp.pad(w, ((0, 0), (0, 0), (0, HEAD_PAD - HEAD_DIM))).reshape(d, QK_PAD)


def _alibi_extras():
    slopes = 2.0 ** (-8.0 * jnp.arange(1, ATT_HEADS + 1, dtype=F32) / ATT_HEADS)
    row = jnp.arange(MOBA_BLOCK, dtype=F32)
    eq = jnp.zeros((MOBA_BLOCK, ATT_HEADS, HEAD_PAD), F32)
    eq = eq.at[:, :, Q_LANE_ONE].set(1.0).at[:, :, Q_LANE_ROW].set(row[:, None])
    ek = jnp.zeros((MOBA_BLOCK, ATT_HEADS, HEAD_PAD), F32)
    ek = ek.at[:, :, Q_LANE_ONE].set(row[:, None] * slopes[None, :])
    ek = ek.at[:, :, Q_LANE_BLK].set(-slopes[None, :] * MOBA_BLOCK).at[:, :, Q_LANE_ROW].set(-slopes[None, :])
    slope_tab = jnp.broadcast_to(slopes[:, None], (ATT_HEADS, MOBA_BLOCK))
    return eq.reshape(MOBA_BLOCK, QK_PAD), ek.reshape(MOBA_BLOCK, QK_PAD), slope_tab


def kernel(x, w_in, sgu_ln_g, sgu_ln_b, sgu_w, sgu_b, ssm_lambda_re, ssm_lambda_im, ssm_log_dt,
           ssm_b_re, ssm_b_im, ssm_c_re, ssm_c_im, ssm_d, glu_w, glu_b, w_branch_a, w_branch_b,
           w_branch_c, w_out, ln1_g, ln1_b, ffn_w1, ffn_w3, ffn_w2, ln2_g, ln2_b):
    bsz, seq, _ = x.shape
    n = bsz * seq
    nb = seq // MOBA_BLOCK
    cpb = seq // S5_CHUNK
    tm = 512 if n % 512 == 0 else MOBA_BLOCK
    eq, ek, slope_tab = _alibi_extras()
    scale = HEAD_DIM ** -0.5
    cap = (MOBA_TOPK + 1) * seq
    tpb = cap // MOBA_BLOCK
    trash = bsz * ATT_HEADS * cap
    o_q = 2 * SGU_WIDTH + SSM_WIDTH
    o_g = o_q + 3 * ATT_WIDTH

    x2 = x.reshape(n, D_MODEL)
    for l in range(DEPTH):
        wl = w_in[l]
        wz = wl[:, :o_q].astype(BF16)
        wq = _pad_heads(wl[:, o_q:o_q + ATT_WIDTH] * scale).astype(BF16)
        wk = _pad_heads(wl[:, o_q + ATT_WIDTH:o_q + 2 * ATT_WIDTH]).astype(BF16)
        wvt = wl[:, o_q + 2 * ATT_WIDTH:o_g].T.astype(BF16)
        wg = wl[:, o_g:].astype(BF16)

        bra, u, qhm, ka, vt, sel, rank, cnt = _inproj(
            x2, wz, wq, wk, wvt, eq, ek, sgu_ln_g[l][None, :], sgu_ln_b[l][None, :],
            sgu_w[l].astype(BF16), sgu_b[l].T, nb)

        tmat, wmat, mmat, amat = _s5_params(
            ssm_lambda_re[l], ssm_lambda_im[l], ssm_log_dt[l], ssm_b_re[l], ssm_b_im[l],
            ssm_c_re[l], ssm_c_im[l])
        ug = u.reshape(bsz * cpb, S5_CHUNK, SSM_GROUPS, SSM_GROUP).transpose(2, 0, 1, 3)
        ug = ug.reshape(SSM_GROUPS, bsz * cpb, S5_COLS)
        dtile = jnp.tile(ssm_d[l].reshape(SSM_GROUPS, 1, SSM_GROUP), (1, 1, S5_CHUNK))
        yg = _s5_scan(ug, tmat, wmat, mmat, amat, dtile, cpb)
        ys = yg.reshape(SSM_GROUPS, bsz * cpb, S5_CHUNK, SSM_GROUP).transpose(1, 2, 0, 3)
        ys = ys.reshape(n, SSM_WIDTH)

        start, tmap, nused = _plan(cnt, nb, tpb)
        dst = _pos(sel, rank, start, nb, cap, trash).reshape(MOBA_TOPK, ATT_HEADS * n)
        qs = _sc_scatter_rows(qhm.reshape(ATT_HEADS * n, HEAD_PAD), dst, trash + MOBA_BLOCK)
        part = _routed(qs, ka, vt, slope_tab, tmap[:, 0, :], nused[:, 0, 0], bsz, seq, tpb)
        g = _sc_gather_rows(part, dst.reshape(1, MOBA_TOPK * ATT_HEADS * n))
        brc = _combine(qhm, ka, vt, sel, g.reshape(MOBA_TOPK, ATT_HEADS, n, HEAD_PAD), slope_tab, nb)

        x2 = _merge(x2, bra, ys, brc, wg, w_branch_a[l].astype(BF16), w_branch_b[l].astype(BF16),
                    w_branch_c[l].astype(BF16), glu_w[l].astype(BF16), glu_b[l][None, :],
                    w_out[l].astype(BF16), ln1_g[l][None, :], ln1_b[l][None, :], tm)
        x2 = _ffn(x2, ffn_w1[l].astype(BF16), ffn_w3[l].astype(BF16), ffn_w2[l].astype(BF16),
                  ln2_g[l][None, :], ln2_b[l][None, :], tm)
    return x2.reshape(bsz, seq, D_MODEL)
```

```python
import functools

import jax
import jax.numpy as jnp
from jax import lax
from jax.experimental import pallas as pl
from jax.experimental.pallas import tpu as pltpu
from jax.experimental.pallas import tpu_sc as plsc

F32 = jnp.float32
BF16 = jnp.bfloat16

D_MODEL = 1024
SGU_CHUNK = 128
SGU_GROUPS = 4
SGU_WIDTH = 512
SSM_WIDTH = 512
SSM_GROUP = 16
SSM_GROUPS = 32
SSM_STATE = 64
ATT_HEADS = 8
HEAD_DIM = 64
ATT_WIDTH = 512
MOBA_BLOCK = 256
MOBA_TOPK = 3
D_FF = 2816
DEPTH = 2
DN_ALPHA = (2 * DEPTH) ** 0.25
LN_EPS = 1e-5
NEG_BIG = -1e30

HEAD_PAD = 128
QK_PAD = ATT_HEADS * HEAD_PAD
Q_LANE_ONE = HEAD_DIM
Q_LANE_BLK = HEAD_DIM + 1
Q_LANE_ROW = HEAD_DIM + 2
SC_WINDOW = 128
ROUTED_TILES = 8
S5_CHUNK = 128
S5_COLS = S5_CHUNK * SSM_GROUP
VMEM_LIMIT = 56 * 1024 * 1024

_HI = lax.Precision.HIGHEST


def _dot(a, b, precision=None):
    return jnp.dot(a, b, preferred_element_type=F32, precision=precision)


def _dot_nt(a, b, precision=None):
    return lax.dot_general(a, b, (((1,), (1,)), ((), ())),
                           preferred_element_type=F32, precision=precision)


def _layer_norm(x, g, b):
    mu = jnp.mean(x, axis=-1, keepdims=True)
    xc = x - mu
    var = jnp.mean(xc * xc, axis=-1, keepdims=True)
    return xc * lax.rsqrt(var + LN_EPS) * g + b


def _inproj_kernel(nb, x_ref, wz_ref, wst_ref, wq_ref, wk_ref, wvt_ref, eq_ref, ek_ref, lng_ref, lnb_ref,
                   sw_ref, sbt_ref, bra_ref, u_ref, q_ref, k_ref, vt_ref, sel_ref, rank_ref, cnt_ref,
                   kmean_ref, carry_ref):
    i = pl.program_id(0)
    il = i % nb

    @pl.when(i == 0)
    def _():
        kmean_ref[...] = jnp.zeros_like(kmean_ref)

    @pl.when(il == 0)
    def _():
        carry_ref[...] = jnp.zeros_like(carry_ref)

    xb = x_ref[...].astype(BF16)

    z = jax.nn.gelu(_dot(xb, wz_ref[...]))
    u = z[:, :SGU_WIDTH]
    vn = _layer_norm(z[:, SGU_WIDTH:], lng_ref[...], lnb_ref[...]).astype(BF16)
    r_io = lax.broadcasted_iota(jnp.int32, (SGU_CHUNK, SGU_CHUNK), 0)
    c_io = lax.broadcasted_iota(jnp.int32, (SGU_CHUNK, SGU_CHUNK), 1)
    tril = r_io >= c_io
    for g in range(SGU_GROUPS):
        w = jnp.where(tril, sw_ref[g], jnp.zeros((), BF16))
        bias = sbt_ref[:, g:g + 1]
        gs = slice(g * SGU_CHUNK, (g + 1) * SGU_CHUNK)
        for c in range(MOBA_BLOCK // SGU_CHUNK):
            rs = slice(c * SGU_CHUNK, (c + 1) * SGU_CHUNK)
            mixed = _dot(w, vn[rs, gs]) + bias
            bra_ref[rs, gs] = (u[rs, gs] * mixed).astype(BF16)

    ut = _dot_nt(wst_ref[...], xb)
    for c in range(MOBA_BLOCK // S5_CHUNK):
        u_ref[c] = ut[:, c * S5_CHUNK:(c + 1) * S5_CHUNK]

    lane = lax.broadcasted_iota(jnp.int32, (1, QK_PAD), 1) & (HEAD_PAD - 1)
    qa = _dot(xb, wq_ref[...]) + eq_ref[...] + jnp.where(lane == Q_LANE_BLK, il.astype(F32), 0.0)
    for h in range(ATT_HEADS):
        q_ref[h] = qa[:, h * HEAD_PAD:(h + 1) * HEAD_PAD]
    ka = _dot(xb, wk_ref[...])
    k_ref[...] = (ka + ek_ref[...]).astype(BF16)
    kmean_ref[pl.ds(il, 1), :] = jnp.mean(ka, axis=0, keepdims=True)
    vt_ref[...] = _dot_nt(wvt_ref[...], xb).astype(BF16)

    blk = lax.broadcasted_iota(jnp.int32, (nb, MOBA_BLOCK), 0)
    neg_inf = jnp.full((), -jnp.inf, F32)
    r_io = lax.broadcasted_iota(jnp.int32, (MOBA_BLOCK, MOBA_BLOCK), 0)
    c_io = lax.broadcasted_iota(jnp.int32, (MOBA_BLOCK, MOBA_BLOCK), 1)
    earlier = (r_io < c_io).astype(BF16)
    for h in range(ATT_HEADS):
        hs = slice(h * HEAD_PAD, h * HEAD_PAD + HEAD_DIM)
        gate = _dot_nt(kmean_ref[:, hs], qa[:, hs], precision=_HI)
        gate = jnp.where(blk < il, gate, neg_inf)
        sels = []
        for r in range(MOBA_TOPK):
            m = jnp.max(gate, axis=0, keepdims=True)
            idx = jnp.min(jnp.where(gate == m, blk, nb), axis=0, keepdims=True)
            sels.append(jnp.where(r < il, idx, -1))
            gate = jnp.where(blk == idx, neg_inf, gate)
        hit = [blk == s for s in sels]
        onehot = jnp.where(hit[0] | hit[1] | hit[2], 1.0, 0.0)
        before = carry_ref[h] + _dot(onehot.astype(BF16), earlier)
        carry_ref[h] = carry_ref[h] + jnp.sum(onehot, axis=1, keepdims=True)
        for r in range(MOBA_TOPK):
            sel_ref[r, h:h + 1, :] = sels[r]
            rank = jnp.sum(jnp.where(hit[r], before, 0.0), axis=0, keepdims=True)
            rank_ref[r, h:h + 1, :] = rank.astype(jnp.int32)

    @pl.when(il == nb - 1)
    def _():
        cnt_ref[0] = carry_ref[...]


def _inproj(x2, wz, wst, wq, wk, wvt, eq, ek, lng, lnb, sw, sbt, nb):
    n = x2.shape[0]
    grid = n // MOBA_BLOCK
    const = lambda i: (0, 0)
    return pl.pallas_call(
        functools.partial(_inproj_kernel, nb),
        grid=(grid,),
        in_specs=[
            pl.BlockSpec((MOBA_BLOCK, D_MODEL), lambda i: (i, 0)),
            pl.BlockSpec(wz.shape, const),
            pl.BlockSpec(wst.shape, const),
            pl.BlockSpec(wq.shape, const),
            pl.BlockSpec(wk.shape, const),
            pl.BlockSpec(wvt.shape, const),
            pl.BlockSpec(eq.shape, const),
            pl.BlockSpec(ek.shape, const),
            pl.BlockSpec(lng.shape, const),
            pl.BlockSpec(lnb.shape, const),
            pl.BlockSpec(sw.shape, lambda i: (0, 0, 0)),
            pl.BlockSpec(sbt.shape, const),
        ],
        out_specs=[
            pl.BlockSpec((MOBA_BLOCK, SGU_WIDTH), lambda i: (i, 0)),
            pl.BlockSpec((MOBA_BLOCK // S5_CHUNK, SSM_WIDTH, S5_CHUNK), lambda i: (i, 0, 0)),
            pl.BlockSpec((ATT_HEADS, MOBA_BLOCK, HEAD_PAD), lambda i: (0, i, 0)),
            pl.BlockSpec((MOBA_BLOCK, QK_PAD), lambda i: (i, 0)),
            pl.BlockSpec((ATT_WIDTH, MOBA_BLOCK), lambda i: (0, i)),
            pl.BlockSpec((MOBA_TOPK, ATT_HEADS, MOBA_BLOCK), lambda i: (0, 0, i)),
            pl.BlockSpec((MOBA_TOPK, ATT_HEADS, MOBA_BLOCK), lambda i: (0, 0, i)),
            pl.BlockSpec((1, ATT_HEADS, nb, MOBA_BLOCK), lambda i: (i // nb, 0, 0, 0)),
        ],
        out_shape=[
            jax.ShapeDtypeStruct((n, SGU_WIDTH), BF16),
            jax.ShapeDtypeStruct((n // S5_CHUNK, SSM_WIDTH, S5_CHUNK), F32),
            jax.ShapeDtypeStruct((ATT_HEADS, n, HEAD_PAD), F32),
            jax.ShapeDtypeStruct((n, QK_PAD), BF16),
            jax.ShapeDtypeStruct((ATT_WIDTH, n), BF16),
            jax.ShapeDtypeStruct((MOBA_TOPK, ATT_HEADS, n), jnp.int32),
            jax.ShapeDtypeStruct((MOBA_TOPK, ATT_HEADS, n), jnp.int32),
            jax.ShapeDtypeStruct((n // (nb * MOBA_BLOCK), ATT_HEADS, nb, MOBA_BLOCK), F32),
        ],
        scratch_shapes=[pltpu.VMEM((nb, QK_PAD), F32),
                        pltpu.VMEM((ATT_HEADS, nb, MOBA_BLOCK), F32)],
        compiler_params=pltpu.CompilerParams(
            dimension_semantics=("arbitrary",), vmem_limit_bytes=VMEM_LIMIT),
        name="inproj_sgu_gate",
    )(x2, wz, wst, wq, wk, wvt, eq, ek, lng, lnb, sw, sbt)


def _s5_param_kernel(lre_r, lim_r, lre_c, lim_c, ldt, bt_re, bt_im, ct_re, ct_im, cr_r, ci_r,
                     t_ref, w_ref, m_ref, a_ref):
    two_p = 2 * SSM_STATE
    dt = jnp.exp(ldt[0])

    def powers(ar, ai, e):
        mag = jnp.exp(ar * e)
        return mag * jnp.cos(ai * e), mag * jnp.sin(ai * e)

    lr, li = lre_r[0], lim_r[0]
    ar, ai = lr * dt, li * dt
    lbr, lbi = powers(ar, ai, 1.0)
    den = lr * lr + li * li
    cfr = ((lbr - 1.0) * lr + lbi * li) / den
    cfi = (lbi * lr - (lbr - 1.0) * li) / den
    bbr = cfr * bt_re[0] - cfi * bt_im[0]
    bbi = cfr * bt_im[0] + cfi * bt_re[0]

    chunk = S5_CHUNK
    s_col = lax.broadcasted_iota(jnp.int32, (chunk, two_p), 0).astype(F32)
    first = lax.broadcasted_iota(jnp.int32, (chunk, two_p), 1) < SSM_STATE
    rev_r, rev_i = powers(ar, ai, (chunk - 1.0) - s_col)

    for hq in range(SSM_GROUP):
        br, bi = bbr[hq:hq + 1, :], bbi[hq:hq + 1, :]
        w = jnp.where(first, rev_r * br - rev_i * bi, rev_r * bi + rev_i * br)
        w_ref[0, hq * chunk:(hq + 1) * chunk, :] = w.astype(BF16)

    lrc, lic = lre_c[0], lim_c[0]
    arc, aic = lrc * dt, lic * dt
    t_row = lax.broadcasted_iota(jnp.int32, (two_p, chunk), 1).astype(F32)
    top = lax.broadcasted_iota(jnp.int32, (two_p, chunk), 0) < SSM_STATE
    pw_r, pw_i = powers(arc, aic, t_row)
    er, ei = powers(arc, aic, t_row + 1.0)

    for h in range(SSM_GROUP):
        cr, ci = ct_re[0][:, h:h + 1], ct_im[0][:, h:h + 1]
        m = jnp.where(top, er * cr - ei * ci, -(er * ci + ei * cr))
        m_ref[0, :, h * chunk:(h + 1) * chunk] = m.astype(BF16)

    first16 = lax.broadcasted_iota(jnp.int32, (SSM_GROUP, two_p), 1) < SSM_STATE
    crr, cir = cr_r[0], ci_r[0]
    g2 = jnp.concatenate(
        [jnp.where(first16, crr * bbr[hq:hq + 1, :] - cir * bbi[hq:hq + 1, :],
                   -(crr * bbi[hq:hq + 1, :] + cir * bbr[hq:hq + 1, :])) for hq in range(SSM_GROUP)],
        axis=0)
    taps = _dot(g2, jnp.where(top, pw_r, pw_i), _HI)

    causal = (lax.broadcasted_iota(jnp.int32, (chunk, chunk), 0)
              <= lax.broadcasted_iota(jnp.int32, (chunk, chunk), 1))
    for hq in range(SSM_GROUP):
        for h in range(SSM_GROUP):
            row = hq * SSM_GROUP + h
            k_rows = jnp.broadcast_to(taps[row:row + 1, :], (chunk, chunk))
            toep = pltpu.roll(k_rows, 0, axis=1, stride=1, stride_axis=0)
            t_ref[0, hq * chunk:(hq + 1) * chunk, h * chunk:(h + 1) * chunk] = (
                jnp.where(causal, toep, 0.0).astype(BF16))

    pr, pi_ = lbr, lbi
    for _ in range(S5_CHUNK.bit_length() - 1):
        pr, pi_ = pr * pr - pi_ * pi_, 2.0 * pr * pi_
    a_ref[0, 0:1, :] = pr
    a_ref[0, 1:2, :] = pi_


def _s5_params(lre, lim, ldt, b_re, b_im, c_re, c_im):
    g, p = lre.shape
    dup = lambda a: jnp.concatenate([a, a], axis=-1)
    lre_r, lim_r = dup(lre)[:, None, :], dup(lim)[:, None, :]
    lre_c, lim_c = dup(lre)[:, :, None], dup(lim)[:, :, None]
    bt_re = dup(jnp.swapaxes(b_re, 1, 2))
    bt_im = dup(jnp.swapaxes(b_im, 1, 2))
    ct = lambda c: jnp.concatenate([jnp.swapaxes(c, 1, 2)] * 2, axis=1)
    ct_re, ct_im = ct(c_re), ct(c_im)
    ldt3 = ldt[:, None, None]
    args = (lre_r, lim_r, lre_c, lim_c, ldt3, bt_re, bt_im, ct_re, ct_im, dup(c_re), dup(c_im))
    in_specs = [pl.BlockSpec((1,) + a.shape[1:], lambda i: (i, 0, 0)) for a in args]
    two_p = 2 * p
    return pl.pallas_call(
        _s5_param_kernel,
        grid=(g,),
        in_specs=in_specs,
        out_specs=[
            pl.BlockSpec((1, S5_COLS, S5_COLS), lambda i: (i, 0, 0)),
            pl.BlockSpec((1, S5_COLS, two_p), lambda i: (i, 0, 0)),
            pl.BlockSpec((1, two_p, S5_COLS), lambda i: (i, 0, 0)),
            pl.BlockSpec((1, 2, two_p), lambda i: (i, 0, 0)),
        ],
        out_shape=[
            jax.ShapeDtypeStruct((g, S5_COLS, S5_COLS), BF16),
            jax.ShapeDtypeStruct((g, S5_COLS, two_p), BF16),
            jax.ShapeDtypeStruct((g, two_p, S5_COLS), BF16),
            jax.ShapeDtypeStruct((g, 2, two_p), F32),
        ],
        compiler_params=pltpu.CompilerParams(vmem_limit_bytes=VMEM_LIMIT),
        name="s5_params",
    )(*args)


def _s5_scan_kernel(cpb, u_ref, t_ref, w_ref, m_ref, a_ref, d_ref, y_ref):
    us = [u_ref[:, hq, :] for hq in range(SSM_GROUP)]
    ub = jnp.concatenate([u.astype(BF16) for u in us], axis=1)
    rows = ub.shape[0]
    two_p = 2 * SSM_STATE
    h = _dot(ub, w_ref[0])
    ar = a_ref[0, 0:1, :]
    ai = a_ref[0, 1:2, :]
    lane = lax.broadcasted_iota(jnp.int32, (1, two_p), 1)
    sign = jnp.where(lane < SSM_STATE, -1.0, 1.0)
    chunk = lax.broadcasted_iota(jnp.int32, (rows, two_p), 0) % cpb
    step = 1
    while step < cpb:
        prev = jnp.where(chunk >= step, pltpu.roll(h, step, axis=0), 0.0)
        swapped = pltpu.roll(prev, SSM_STATE, axis=1)
        h = h + prev * ar + swapped * (ai * sign)
        ar, ai = ar * ar - ai * ai, 2.0 * ar * ai
        step *= 2
    hprev = jnp.where(chunk >= 1, pltpu.roll(h, 1, axis=0), 0.0)
    y = _dot(ub, t_ref[0]) + _dot(hprev.astype(BF16), m_ref[0])
    for hq in range(SSM_GROUP):
        yh = y[:, hq * S5_CHUNK:(hq + 1) * S5_CHUNK] + d_ref[0, hq:hq + 1, :] * us[hq]
        y_ref[:, hq, :] = jax.nn.gelu(yh)


def _s5_scan(utc, tmat, wmat, mmat, amat, dtab, cpb):
    rows = utc.shape[0]
    two_p = 2 * SSM_STATE
    idx = lambda i: (i, 0, 0)
    chan = pl.BlockSpec((rows, SSM_GROUP, S5_CHUNK), lambda i: (0, i, 0))
    return pl.pallas_call(
        functools.partial(_s5_scan_kernel, cpb),
        grid=(SSM_GROUPS,),
        in_specs=[
            chan,
            pl.BlockSpec((1, S5_COLS, S5_COLS), idx),
            pl.BlockSpec((1, S5_COLS, two_p), idx),
            pl.BlockSpec((1, two_p, S5_COLS), idx),
            pl.BlockSpec((1, 2, two_p), idx),
            pl.BlockSpec((1, SSM_GROUP, S5_CHUNK), idx),
        ],
        out_specs=chan,
        out_shape=jax.ShapeDtypeStruct(utc.shape, F32),
        compiler_params=pltpu.CompilerParams(vmem_limit_bytes=VMEM_LIMIT),
        name="s5_scan",
    )(utc, tmat, wmat, mmat, amat, dtab)


def _plan_kernel(nb, tpb, cnt_ref, start_ref, tmap_ref, nused_ref):
    cnt = cnt_ref[0, 0]
    padded = jnp.floor((cnt + (MOBA_BLOCK - 1.0)) * (1.0 / MOBA_BLOCK)) * MOBA_BLOCK
    r_io = lax.broadcasted_iota(jnp.int32, (nb, nb), 0)
    c_io = lax.broadcasted_iota(jnp.int32, (nb, nb), 1)
    start = _dot((c_io < r_io).astype(F32), padded, _HI)
    start_ref[0, 0] = start
    end = (start + padded)[:, 0:1]
    tile_row = lax.broadcasted_iota(jnp.int32, (nb, tpb), 1).astype(F32) * MOBA_BLOCK
    blk_of_tile = jnp.sum(jnp.where(end <= tile_row, 1.0, 0.0), axis=0, keepdims=True)
    tmap_ref[0] = jnp.minimum(blk_of_tile, nb - 1.0).astype(jnp.int32)
    total = jnp.max(end, axis=0, keepdims=True)
    nused_ref[0] = jnp.broadcast_to(total * (1.0 / MOBA_BLOCK), (1, HEAD_PAD)).astype(jnp.int32)


def _plan(cnt, nb, tpb):
    bsz = cnt.shape[0]
    nbh = bsz * ATT_HEADS
    return pl.pallas_call(
        functools.partial(_plan_kernel, nb, tpb),
        grid=(bsz, ATT_HEADS),
        in_specs=[pl.BlockSpec((1, 1, nb, MOBA_BLOCK), lambda b, h: (b, h, 0, 0))],
        out_specs=[
            pl.BlockSpec((1, 1, nb, MOBA_BLOCK), lambda b, h: (b, h, 0, 0)),
            pl.BlockSpec((1, 1, tpb), lambda b, h: (b * ATT_HEADS + h, 0, 0)),
            pl.BlockSpec((1, 1, HEAD_PAD), lambda b, h: (b * ATT_HEADS + h, 0, 0)),
        ],
        out_shape=[
            jax.ShapeDtypeStruct((bsz, ATT_HEADS, nb, MOBA_BLOCK), F32),
            jax.ShapeDtypeStruct((nbh, 1, tpb), jnp.int32),
            jax.ShapeDtypeStruct((nbh, 1, HEAD_PAD), jnp.int32),
        ],
        name="route_plan",
    )(cnt)


def _pos_kernel(nb, cap, trash, sel_ref, rank_ref, start_ref, dst_ref):
    b = pl.program_id(0) // nb
    blk = lax.broadcasted_iota(jnp.int32, (nb, MOBA_BLOCK), 0)
    lane = lax.broadcasted_iota(jnp.int32, (1, MOBA_BLOCK), 1)
    for h in range(ATT_HEADS):
        start = start_ref[0, h]
        base = (b * ATT_HEADS + h) * cap
        for r in range(MOBA_TOPK):
            s = sel_ref[r, h:h + 1, :]
            first = jnp.sum(jnp.where(blk == s, start, 0.0), axis=0, keepdims=True).astype(jnp.int32)
            dst = base + first + rank_ref[r, h:h + 1, :]
            dst_ref[r, h:h + 1, :] = jnp.where(s >= 0, dst, trash + lane)


def _pos(sel, rank, start, nb, cap, trash):
    n = sel.shape[-1]
    blk3 = pl.BlockSpec((MOBA_TOPK, ATT_HEADS, MOBA_BLOCK), lambda i: (0, 0, i))
    return pl.pallas_call(
        functools.partial(_pos_kernel, nb, cap, trash),
        grid=(n // MOBA_BLOCK,),
        in_specs=[blk3, blk3,
                  pl.BlockSpec((1, ATT_HEADS, nb, MOBA_BLOCK), lambda i: (i // nb, 0, 0, 0))],
        out_specs=blk3,
        out_shape=jax.ShapeDtypeStruct(sel.shape, jnp.int32),
        name="route_pos",
    )(sel, rank, start)


def _sc_mesh():
    return plsc.VectorSubcoreMesh(core_axis_name="core", subcore_axis_name="subcore")


def _sc_scatter_rows(x, idx, rows_out):
    nrep, nin = idx.shape

    @pl.kernel(out_type=jax.ShapeDtypeStruct((rows_out, HEAD_PAD), x.dtype), mesh=_sc_mesh(),
               scratch_types=[])
    def scatter(x_hbm, i_hbm, o_hbm):
        def body(x_vmem, i_vmem):
            pltpu.sync_copy(x_vmem, o_hbm.at[i_vmem.at[0]])

        pltpu.emit_pipeline(
            body,
            grid=(nrep, nin // SC_WINDOW),
            in_specs=[pl.BlockSpec((SC_WINDOW, HEAD_PAD), lambda r, i: (i, 0)),
                      pl.BlockSpec((1, SC_WINDOW), lambda r, i: (r, i))],
            out_specs=[],
            core_axis_name=("core", "subcore"),
            dimension_semantics=(pltpu.PARALLEL, pltpu.PARALLEL),
        )(x_hbm, i_hbm)

    return scatter(x, idx)


def _sc_gather_rows(x, idx):
    nout = idx.shape[1]

    @pl.kernel(out_type=jax.ShapeDtypeStruct((nout, HEAD_PAD), x.dtype), mesh=_sc_mesh())
    def gather(x_hbm, i_hbm, o_hbm):
        def body(i_vmem, o_vmem):
            pltpu.sync_copy(x_hbm.at[i_vmem.at[0]], o_vmem)

        pltpu.emit_pipeline(
            body,
            grid=(nout // SC_WINDOW,),
            in_specs=[pl.BlockSpec((1, SC_WINDOW), lambda i: (0, i))],
            out_specs=[pl.BlockSpec((SC_WINDOW, HEAD_PAD), lambda i: (i, 0))],
            core_axis_name=("core", "subcore"),
            dimension_semantics=(pltpu.PARALLEL,),
        )(i_hbm, o_hbm)

    return gather(x, idx)


def _softmax_tile(z, vt):
    m = jnp.max(z, axis=0, keepdims=True)
    p = jnp.exp(z - m)
    l = jnp.sum(p, axis=0, keepdims=True)
    return m, l, _dot(vt, p.astype(BF16))


def _routed_kernel(tmap_ref, nused_ref, qs_ref, slope_ref, k_ref, vt_ref, o_ref):
    h = pl.program_id(1)
    bh = pl.program_id(0) * ATT_HEADS + h
    s = pl.program_id(2)

    @pl.when(s * ROUTED_TILES < nused_ref[bh])
    def _():
        slope = slope_ref[pl.ds(h, 1), :]
        row = lax.broadcasted_iota(jnp.int32, (HEAD_DIM, MOBA_BLOCK), 0)
        tiles = range(ROUTED_TILES)
        js = [tmap_ref[bh, s * ROUTED_TILES + u] for u in tiles]
        ks = [pl.multiple_of(j * MOBA_BLOCK, MOBA_BLOCK) for j in js]
        zs = [_dot_nt(k_ref[pl.ds(ks[u], MOBA_BLOCK), :],
                      qs_ref[u * MOBA_BLOCK:(u + 1) * MOBA_BLOCK, :].astype(BF16)) for u in tiles]
        ms = [jnp.max(z, axis=0, keepdims=True) for z in zs]
        ps = [jnp.exp(zs[u] - ms[u]) for u in tiles]
        ls = [jnp.sum(p, axis=0, keepdims=True) for p in ps]
        ots = [_dot(vt_ref[:, pl.ds(ks[u], MOBA_BLOCK)], ps[u].astype(BF16)) for u in tiles]
        for u in tiles:
            m = ms[u] + slope * jnp.full((1, MOBA_BLOCK), ks[u], jnp.int32).astype(F32)
            stats = jnp.where(row == 0, m, jnp.where(row == 1, ls[u], 0.0))
            o_ref[u * MOBA_BLOCK:(u + 1) * MOBA_BLOCK, :] = jnp.concatenate([ots[u], stats], axis=0).T


def _routed(qs, ka, vt, slope_tab, tmap, nused, bsz, seq, tpb):
    steps = tpb // ROUTED_TILES
    rows = ROUTED_TILES * MOBA_BLOCK
    shift = ROUTED_TILES.bit_length() - 1

    def step_of(b, h, s, tm, nu):
        bh = b * ATT_HEADS + h
        used = lax.shift_right_logical(nu[bh] + (ROUTED_TILES - 1), shift)
        return bh * steps + jnp.minimum(s, jnp.maximum(used - 1, 0)), 0

    grid_spec = pltpu.PrefetchScalarGridSpec(
        num_scalar_prefetch=2,
        grid=(bsz, ATT_HEADS, steps),
        in_specs=[
            pl.BlockSpec((rows, HEAD_PAD), step_of),
            pl.BlockSpec(slope_tab.shape, lambda b, h, s, tm, nu: (0, 0)),
            pl.BlockSpec((seq, HEAD_PAD), lambda b, h, s, tm, nu: (b, h)),
            pl.BlockSpec((HEAD_DIM, seq), lambda b, h, s, tm, nu: (h, b)),
        ],
        out_specs=pl.BlockSpec((rows, HEAD_PAD), step_of),
    )
    return pl.pallas_call(
        _routed_kernel,
        grid_spec=grid_spec,
        out_shape=jax.ShapeDtypeStruct(qs.shape, F32),
        compiler_params=pltpu.CompilerParams(vmem_limit_bytes=VMEM_LIMIT),
        name="moba_routed",
    )(tmap, nused, qs, slope_tab, ka, vt)


def _combine_kernel(nb, q_ref, k_ref, vt_ref, sel_ref, g_ref, slope_ref, o_ref):
    il = pl.program_id(0) % nb
    kio = lax.broadcasted_iota(jnp.int32, (MOBA_BLOCK, MOBA_BLOCK), 0)
    qio = lax.broadcasted_iota(jnp.int32, (MOBA_BLOCK, MOBA_BLOCK), 1)
    causal = kio <= qio
    own_shift = jnp.full((1, MOBA_BLOCK), il * MOBA_BLOCK, jnp.int32).astype(F32)
    outs = []
    heads = range(ATT_HEADS)
    zs = [_dot_nt(k_ref[:, h * HEAD_PAD:(h + 1) * HEAD_PAD], q_ref[h].astype(BF16)) for h in heads]
    zs = [jnp.where(causal, z, NEG_BIG) for z in zs]
    ms = [jnp.max(z, axis=0, keepdims=True) for z in zs]
    ps = [jnp.exp(zs[h] - ms[h]) for h in heads]
    ls = [jnp.sum(p, axis=0, keepdims=True) for p in ps]
    os_ = [_dot(vt_ref[h * HEAD_DIM:(h + 1) * HEAD_DIM, :], ps[h].astype(BF16)) for h in heads]
    for h in heads:
        l0, o0 = ls[h], os_[h]
        m0 = ms[h] + slope_ref[h:h + 1, :] * own_shift
        parts = []
        for r in range(MOBA_TOPK):
            gt = g_ref[r, h].T
            valid = sel_ref[r, h:h + 1, :] >= 0
            parts.append((jnp.where(valid, gt[HEAD_DIM:HEAD_DIM + 1, :], NEG_BIG),
                          jnp.where(valid, gt[HEAD_DIM + 1:HEAD_DIM + 2, :], 0.0),
                          jnp.where(valid, gt[:HEAD_DIM, :], 0.0)))
        m = m0
        for mr, _, _ in parts:
            m = jnp.maximum(m, mr)
        w = jnp.exp(m0 - m)
        num, den = w * o0, w * l0
        for mr, lr, orr in parts:
            w = jnp.exp(mr - m)
            num, den = num + w * orr, den + w * lr
        outs.append(num / den)
    o_ref[...] = jnp.concatenate(outs, axis=0).T.astype(BF16)


def _combine(qhm, ka, vt, sel, g4, slope_tab, nb):
    n = ka.shape[0]
    return pl.pallas_call(
        functools.partial(_combine_kernel, nb),
        grid=(n // MOBA_BLOCK,),
        in_specs=[
            pl.BlockSpec((ATT_HEADS, MOBA_BLOCK, HEAD_PAD), lambda i: (0, i, 0)),
            pl.BlockSpec((MOBA_BLOCK, QK_PAD), lambda i: (i, 0)),
            pl.BlockSpec((ATT_WIDTH, MOBA_BLOCK), lambda i: (0, i)),
            pl.BlockSpec((MOBA_TOPK, ATT_HEADS, MOBA_BLOCK), lambda i: (0, 0, i)),
            pl.BlockSpec((MOBA_TOPK, ATT_HEADS, MOBA_BLOCK, HEAD_PAD), lambda i: (0, 0, i, 0)),
            pl.BlockSpec(slope_tab.shape, lambda i: (0, 0)),
        ],
        out_specs=pl.BlockSpec((MOBA_BLOCK, ATT_WIDTH), lambda i: (i, 0)),
        out_shape=jax.ShapeDtypeStruct((n, ATT_WIDTH), BF16),
        compiler_params=pltpu.CompilerParams(vmem_limit_bytes=VMEM_LIMIT),
        name="moba_own_combine",
    )(qhm, ka, vt, sel, g4, slope_tab)


def _merge_kernel(x_ref, bra_ref, ys_ref, brc_ref, wg_ref, wa_ref, wb_ref, wc_ref, gw_ref, gb_ref,
                  wo_ref, lg_ref, lb_ref, o_ref):
    x = x_ref[...]
    xb = x.astype(BF16)

    def gate(k):
        return jax.nn.sigmoid(_dot(xb, wg_ref[:, k * D_MODEL:(k + 1) * D_MODEL]))

    merged = gate(0) * _dot(bra_ref[...], wa_ref[...])
    ys = jnp.concatenate([ys_ref[c].T for c in range(ys_ref.shape[0])], axis=0)
    brb = ys * jax.nn.sigmoid(_dot(ys.astype(BF16), gw_ref[...]) + gb_ref[...])
    merged = merged + gate(1) * _dot(brb.astype(BF16), wb_ref[...])
    merged = merged + gate(2) * _dot(brc_ref[...], wc_ref[...])
    mix = _dot(merged.astype(BF16), wo_ref[...])
    o_ref[...] = _layer_norm(DN_ALPHA * x + mix, lg_ref[...], lb_ref[...])


def _merge(x2, bra, ys, brc, wg, wa, wb, wc, gw, gb, wo, lg, lb, tm):
    n = x2.shape[0]
    const = lambda i: (0, 0)
    tile = lambda w: pl.BlockSpec((tm, w), lambda i: (i, 0))
    full = lambda a: pl.BlockSpec(a.shape, const)
    return pl.pallas_call(
        _merge_kernel,
        grid=(n // tm,),
        in_specs=[tile(D_MODEL), tile(SGU_WIDTH),
                  pl.BlockSpec((tm // S5_CHUNK, SSM_WIDTH, S5_CHUNK), lambda i: (i, 0, 0)),
                  tile(ATT_WIDTH),
                  full(wg), full(wa), full(wb), full(wc), full(gw), full(gb), full(wo), full(lg), full(lb)],
        out_specs=tile(D_MODEL),
        out_shape=jax.ShapeDtypeStruct((n, D_MODEL), F32),
        compiler_params=pltpu.CompilerParams(vmem_limit_bytes=VMEM_LIMIT),
        name="merge_ln",
    )(x2, bra, ys, brc, wg, wa, wb, wc, gw, gb, wo, lg, lb)


FF_CHUNK = D_FF // 2


def _ffn_kernel(x_ref, w1_ref, w3_ref, w2_ref, lg_ref, lb_ref, o_ref):
    x = x_ref[...]
    xb = x.astype(BF16)
    acc = None
    for c in range(D_FF // FF_CHUNK):
        cs = slice(c * FF_CHUNK, (c + 1) * FF_CHUNK)
        h = (jax.nn.silu(_dot(xb, w1_ref[:, cs])) * _dot(xb, w3_ref[:, cs])).astype(BF16)
        part = _dot(h, w2_ref[cs, :])
        acc = part if acc is None else acc + part
    o_ref[...] = _layer_norm(DN_ALPHA * x + acc, lg_ref[...], lb_ref[...])


def _ffn(x2, w1, w3, w2, lg, lb, tm):
    n = x2.shape[0]
    const = lambda i: (0, 0)
    full = lambda a: pl.BlockSpec(a.shape, const)
    return pl.pallas_call(
        _ffn_kernel,
        grid=(n // tm,),
        in_specs=[pl.BlockSpec((tm, D_MODEL), lambda i: (i, 0)),
                  full(w1), full(w3), full(w2), full(lg), full(lb)],
        out_specs=pl.BlockSpec((tm, D_MODEL), lambda i: (i, 0)),
        out_shape=jax.ShapeDtypeStruct((n, D_MODEL), F32),
        compiler_params=pltpu.CompilerParams(vmem_limit_bytes=VMEM_LIMIT),
        name="ffn_ln",
    )(x2, w1, w3, w2, lg, lb)


def _pad_heads(w):
    d = w.shape[0]
    w = w.reshape(d, ATT_HEADS, HEAD_DIM)
    return jnp.pad(w, ((0, 0), (0, 0), (0, HEAD_PAD - HEAD_DIM))).reshape(d, QK_PAD)


def _alibi_extras():
    slopes = 2.0 ** (-8.0 * jnp.arange(1, ATT_HEADS + 1, dtype=F32) / ATT_HEADS)
    row = jnp.arange(MOBA_BLOCK, dtype=F32)
    eq = jnp.zeros((MOBA_BLOCK, ATT_HEADS, HEAD_PAD), F32)
    eq = eq.at[:, :, Q_LANE_ONE].set(1.0).at[:, :, Q_LANE_ROW].set(row[:, None])
    ek = jnp.zeros((MOBA_BLOCK, ATT_HEADS, HEAD_PAD), F32)
    ek = ek.at[:, :, Q_LANE_ONE].set(row[:, None] * slopes[None, :])
    ek = ek.at[:, :, Q_LANE_BLK].set(-slopes[None, :] * MOBA_BLOCK).at[:, :, Q_LANE_ROW].set(-slopes[None, :])
    slope_tab = jnp.broadcast_to(slopes[:, None], (ATT_HEADS, MOBA_BLOCK))
    return eq.reshape(MOBA_BLOCK, QK_PAD), ek.reshape(MOBA_BLOCK, QK_PAD), slope_tab


def kernel(x, w_in, sgu_ln_g, sgu_ln_b, sgu_w, sgu_b, ssm_lambda_re, ssm_lambda_im, ssm_log_dt,
           ssm_b_re, ssm_b_im, ssm_c_re, ssm_c_im, ssm_d, glu_w, glu_b, w_branch_a, w_branch_b,
           w_branch_c, w_out, ln1_g, ln1_b, ffn_w1, ffn_w3, ffn_w2, ln2_g, ln2_b):
    bsz, seq, _ = x.shape
    n = bsz * seq
    nb = seq // MOBA_BLOCK
    cpb = seq // S5_CHUNK
    tm = 512 if n % 512 == 0 else MOBA_BLOCK
    eq, ek, slope_tab = _alibi_extras()
    scale = HEAD_DIM ** -0.5
    cap = (MOBA_TOPK + 1) * seq
    tpb = cap // MOBA_BLOCK
    trash = bsz * ATT_HEADS * cap
    o_q = 2 * SGU_WIDTH + SSM_WIDTH
    o_g = o_q + 3 * ATT_WIDTH

    x2 = x.reshape(n, D_MODEL)
    for l in range(DEPTH):
        wl = w_in[l]
        wz = wl[:, :2 * SGU_WIDTH].astype(BF16)
        wst = wl[:, 2 * SGU_WIDTH:o_q].T.astype(BF16)
        wq = _pad_heads(wl[:, o_q:o_q + ATT_WIDTH] * scale).astype(BF16)
        wk = _pad_heads(wl[:, o_q + ATT_WIDTH:o_q + 2 * ATT_WIDTH]).astype(BF16)
        wvt = wl[:, o_q + 2 * ATT_WIDTH:o_g].T.astype(BF16)
        wg = wl[:, o_g:].astype(BF16)

        bra, utc, qhm, ka, vt, sel, rank, cnt = _inproj(
            x2, wz, wst, wq, wk, wvt, eq, ek, sgu_ln_g[l][None, :], sgu_ln_b[l][None, :],
            sgu_w[l].astype(BF16), sgu_b[l].T, nb)

        tmat, wmat, mmat, amat = _s5_params(
            ssm_lambda_re[l], ssm_lambda_im[l], ssm_log_dt[l], ssm_b_re[l], ssm_b_im[l],
            ssm_c_re[l], ssm_c_im[l])
        dtab = jnp.broadcast_to(ssm_d[l].reshape(SSM_GROUPS, SSM_GROUP, 1),
                                (SSM_GROUPS, SSM_GROUP, S5_CHUNK))
        ys = _s5_scan(utc, tmat, wmat, mmat, amat, dtab, cpb)

        start, tmap, nused = _plan(cnt, nb, tpb)
        dst = _pos(sel, rank, start, nb, cap, trash).reshape(MOBA_TOPK, ATT_HEADS * n)
        qs = _sc_scatter_rows(qhm.reshape(ATT_HEADS * n, HEAD_PAD), dst, trash + MOBA_BLOCK)
        part = _routed(qs, ka, vt, slope_tab, tmap[:, 0, :], nused[:, 0, 0], bsz, seq, tpb)
        g = _sc_gather_rows(part, dst.reshape(1, MOBA_TOPK * ATT_HEADS * n))
        brc = _combine(qhm, ka, vt, sel, g.reshape(MOBA_TOPK, ATT_HEADS, n, HEAD_PAD), slope_tab, nb)

        x2 = _merge(x2, bra, ys, brc, wg, w_branch_a[l].astype(BF16), w_branch_b[l].astype(BF16),
                    w_branch_c[l].astype(BF16), glu_w[l].astype(BF16), glu_b[l][None, :],
                    w_out[l].astype(BF16), ln1_g[l][None, :], ln1_b[l][None, :], tm)
        x2 = _ffn(x2, ffn_w1[l].astype(BF16), ffn_w3[l].astype(BF16), ffn_w2[l].astype(BF16),
                  ln2_g[l][None, :], ln2_b[l][None, :], tm)
    return x2.reshape(bsz, seq, D_MODEL)
```

```python
import functools

import jax
import jax.numpy as jnp
from jax import lax
from jax.experimental import pallas as pl
from jax.experimental.pallas import tpu as pltpu
from jax.experimental.pallas import tpu_sc as plsc

F32 = jnp.float32
BF16 = jnp.bfloat16

D_MODEL = 1024
SGU_CHUNK = 128
SGU_GROUPS = 4
SGU_WIDTH = 512
SSM_WIDTH = 512
SSM_GROUP = 16
SSM_GROUPS = 32
SSM_STATE = 64
ATT_HEADS = 8
HEAD_DIM = 64
ATT_WIDTH = 512
MOBA_BLOCK = 256
MOBA_TOPK = 3
D_FF = 2816
DEPTH = 2
DN_ALPHA = (2 * DEPTH) ** 0.25
LN_EPS = 1e-5
NEG_BIG = -1e30

HEAD_PAD = 128
QK_PAD = ATT_HEADS * HEAD_PAD
Q_LANE_ONE = HEAD_DIM
Q_LANE_BLK = HEAD_DIM + 1
Q_LANE_ROW = HEAD_DIM + 2
SC_WINDOW = 128
ROUTED_TILES = 8
S5_CHUNK = 128
S5_COLS = S5_CHUNK * SSM_GROUP
VMEM_LIMIT = 56 * 1024 * 1024

_HI = lax.Precision.HIGHEST


def _dot(a, b, precision=None):
    return jnp.dot(a, b, preferred_element_type=F32, precision=precision)


def _dot_nt(a, b, precision=None):
    return lax.dot_general(a, b, (((1,), (1,)), ((), ())),
                           preferred_element_type=F32, precision=precision)


def _layer_norm(x, g, b):
    mu = jnp.mean(x, axis=-1, keepdims=True)
    xc = x - mu
    var = jnp.mean(xc * xc, axis=-1, keepdims=True)
    return xc * lax.rsqrt(var + LN_EPS) * g + b


def _inproj_kernel(nb, x_ref, wz_ref, wst_ref, wq_ref, wk_ref, wvt_ref, eq_ref, ek_ref, lng_ref, lnb_ref,
                   sw_ref, sbt_ref, bra_ref, u_ref, q_ref, k_ref, vt_ref, sel_ref, rank_ref, cnt_ref,
                   kmean_ref, carry_ref):
    i = pl.program_id(0)
    il = i % nb

    @pl.when(i == 0)
    def _():
        kmean_ref[...] = jnp.zeros_like(kmean_ref)

    @pl.when(il == 0)
    def _():
        carry_ref[...] = jnp.zeros_like(carry_ref)

    xb = x_ref[...].astype(BF16)

    z = jax.nn.gelu(_dot(xb, wz_ref[...]))
    u = z[:, :SGU_WIDTH]
    vn = _layer_norm(z[:, SGU_WIDTH:], lng_ref[...], lnb_ref[...]).astype(BF16)
    r_io = lax.broadcasted_iota(jnp.int32, (SGU_CHUNK, SGU_CHUNK), 0)
    c_io = lax.broadcasted_iota(jnp.int32, (SGU_CHUNK, SGU_CHUNK), 1)
    tril = r_io >= c_io
    for g in range(SGU_GROUPS):
        w = jnp.where(tril, sw_ref[g], jnp.zeros((), BF16))
        bias = sbt_ref[:, g:g + 1]
        gs = slice(g * SGU_CHUNK, (g + 1) * SGU_CHUNK)
        for c in range(MOBA_BLOCK // SGU_CHUNK):
            rs = slice(c * SGU_CHUNK, (c + 1) * SGU_CHUNK)
            mixed = _dot(w, vn[rs, gs]) + bias
            bra_ref[rs, gs] = (u[rs, gs] * mixed).astype(BF16)

    ut = _dot_nt(wst_ref[...], xb)
    for c in range(MOBA_BLOCK // S5_CHUNK):
        u_ref[c] = ut[:, c * S5_CHUNK:(c + 1) * S5_CHUNK]

    lane = lax.broadcasted_iota(jnp.int32, (1, QK_PAD), 1) & (HEAD_PAD - 1)
    qa = _dot(xb, wq_ref[...]) + eq_ref[...] + jnp.where(lane == Q_LANE_BLK, il.astype(F32), 0.0)
    for h in range(ATT_HEADS):
        q_ref[h] = qa[:, h * HEAD_PAD:(h + 1) * HEAD_PAD]
    ka = _dot(xb, wk_ref[...])
    k_ref[...] = (ka + ek_ref[...]).astype(BF16)
    kmean_ref[pl.ds(il, 1), :] = jnp.mean(ka, axis=0, keepdims=True)
    vt_ref[...] = _dot_nt(wvt_ref[...], xb).astype(BF16)

    blk = lax.broadcasted_iota(jnp.int32, (nb, MOBA_BLOCK), 0)
    neg_inf = jnp.full((), -jnp.inf, F32)
    r_io = lax.broadcasted_iota(jnp.int32, (MOBA_BLOCK, MOBA_BLOCK), 0)
    c_io = lax.broadcasted_iota(jnp.int32, (MOBA_BLOCK, MOBA_BLOCK), 1)
    earlier = (r_io < c_io).astype(BF16)
    heads = range(ATT_HEADS)
    past = blk < il
    gates = []
    for h in heads:
        hs = slice(h * HEAD_PAD, h * HEAD_PAD + HEAD_DIM)
        gates.append(jnp.where(past, _dot_nt(kmean_ref[:, hs], qa[:, hs], precision=_HI), neg_inf))
    sels = [[] for _ in heads]
    for r in range(MOBA_TOPK):
        ms = [jnp.max(g, axis=0, keepdims=True) for g in gates]
        idxs = [jnp.min(jnp.where(gates[h] == ms[h], blk, nb), axis=0, keepdims=True) for h in heads]
        for h in heads:
            sels[h].append(jnp.where(r < il, idxs[h], -1))
        gates = [jnp.where(blk == idxs[h], neg_inf, gates[h]) for h in heads]
    hits = [[blk == s for s in sels[h]] for h in heads]
    onehots = [jnp.where(hits[h][0] | hits[h][1] | hits[h][2], 1.0, 0.0) for h in heads]
    befores = [carry_ref[h] + _dot(onehots[h].astype(BF16), earlier) for h in heads]
    for h in heads:
        carry_ref[h] = carry_ref[h] + jnp.sum(onehots[h], axis=1, keepdims=True)
        for r in range(MOBA_TOPK):
            sel_ref[r, h:h + 1, :] = sels[h][r]
            rank = jnp.sum(jnp.where(hits[h][r], befores[h], 0.0), axis=0, keepdims=True)
            rank_ref[r, h:h + 1, :] = rank.astype(jnp.int32)

    @pl.when(il == nb - 1)
    def _():
        cnt_ref[0] = carry_ref[...]


def _inproj(x2, wz, wst, wq, wk, wvt, eq, ek, lng, lnb, sw, sbt, nb):
    n = x2.shape[0]
    grid = n // MOBA_BLOCK
    const = lambda i: (0, 0)
    return pl.pallas_call(
        functools.partial(_inproj_kernel, nb),
        grid=(grid,),
        in_specs=[
            pl.BlockSpec((MOBA_BLOCK, D_MODEL), lambda i: (i, 0)),
            pl.BlockSpec(wz.shape, const),
            pl.BlockSpec(wst.shape, const),
            pl.BlockSpec(wq.shape, const),
            pl.BlockSpec(wk.shape, const),
            pl.BlockSpec(wvt.shape, const),
            pl.BlockSpec(eq.shape, const),
            pl.BlockSpec(ek.shape, const),
            pl.BlockSpec(lng.shape, const),
            pl.BlockSpec(lnb.shape, const),
            pl.BlockSpec(sw.shape, lambda i: (0, 0, 0)),
            pl.BlockSpec(sbt.shape, const),
        ],
        out_specs=[
            pl.BlockSpec((MOBA_BLOCK, SGU_WIDTH), lambda i: (i, 0)),
            pl.BlockSpec((MOBA_BLOCK // S5_CHUNK, SSM_WIDTH, S5_CHUNK), lambda i: (i, 0, 0)),
            pl.BlockSpec((ATT_HEADS, MOBA_BLOCK, HEAD_PAD), lambda i: (0, i, 0)),
            pl.BlockSpec((MOBA_BLOCK, QK_PAD), lambda i: (i, 0)),
            pl.BlockSpec((ATT_WIDTH, MOBA_BLOCK), lambda i: (0, i)),
            pl.BlockSpec((MOBA_TOPK, ATT_HEADS, MOBA_BLOCK), lambda i: (0, 0, i)),
            pl.BlockSpec((MOBA_TOPK, ATT_HEADS, MOBA_BLOCK), lambda i: (0, 0, i)),
            pl.BlockSpec((1, ATT_HEADS, nb, MOBA_BLOCK), lambda i: (i // nb, 0, 0, 0)),
        ],
        out_shape=[
            jax.ShapeDtypeStruct((n, SGU_WIDTH), BF16),
            jax.ShapeDtypeStruct((n // S5_CHUNK, SSM_WIDTH, S5_CHUNK), F32),
            jax.ShapeDtypeStruct((ATT_HEADS, n, HEAD_PAD), F32),
            jax.ShapeDtypeStruct((n, QK_PAD), BF16),
            jax.ShapeDtypeStruct((ATT_WIDTH, n), BF16),
            jax.ShapeDtypeStruct((MOBA_TOPK, ATT_HEADS, n), jnp.int32),
            jax.ShapeDtypeStruct((MOBA_TOPK, ATT_HEADS, n), jnp.int32),
            jax.ShapeDtypeStruct((n // (nb * MOBA_BLOCK), ATT_HEADS, nb, MOBA_BLOCK), F32),
        ],
        scratch_shapes=[pltpu.VMEM((nb, QK_PAD), F32),
                        pltpu.VMEM((ATT_HEADS, nb, MOBA_BLOCK), F32)],
        compiler_params=pltpu.CompilerParams(
            dimension_semantics=("arbitrary",), vmem_limit_bytes=VMEM_LIMIT),
        name="inproj_sgu_gate",
    )(x2, wz, wst, wq, wk, wvt, eq, ek, lng, lnb, sw, sbt)


def _s5_param_kernel(lre_r, lim_r, lre_c, lim_c, ldt, bt_re, bt_im, ct_re, ct_im, cr_r, ci_r,
                     t_ref, w_ref, m_ref, a_ref):
    two_p = 2 * SSM_STATE
    dt = jnp.exp(ldt[0])

    def powers(ar, ai, e):
        mag = jnp.exp(ar * e)
        return mag * jnp.cos(ai * e), mag * jnp.sin(ai * e)

    lr, li = lre_r[0], lim_r[0]
    ar, ai = lr * dt, li * dt
    lbr, lbi = powers(ar, ai, 1.0)
    den = lr * lr + li * li
    cfr = ((lbr - 1.0) * lr + lbi * li) / den
    cfi = (lbi * lr - (lbr - 1.0) * li) / den
    bbr = cfr * bt_re[0] - cfi * bt_im[0]
    bbi = cfr * bt_im[0] + cfi * bt_re[0]

    chunk = S5_CHUNK
    s_col = lax.broadcasted_iota(jnp.int32, (chunk, two_p), 0).astype(F32)
    first = lax.broadcasted_iota(jnp.int32, (chunk, two_p), 1) < SSM_STATE
    rev_r, rev_i = powers(ar, ai, (chunk - 1.0) - s_col)

    for hq in range(SSM_GROUP):
        br, bi = bbr[hq:hq + 1, :], bbi[hq:hq + 1, :]
        w = jnp.where(first, rev_r * br - rev_i * bi, rev_r * bi + rev_i * br)
        w_ref[0, hq * chunk:(hq + 1) * chunk, :] = w.astype(BF16)

    lrc, lic = lre_c[0], lim_c[0]
    arc, aic = lrc * dt, lic * dt
    t_row = lax.broadcasted_iota(jnp.int32, (two_p, chunk), 1).astype(F32)
    top = lax.broadcasted_iota(jnp.int32, (two_p, chunk), 0) < SSM_STATE
    pw_r, pw_i = powers(arc, aic, t_row)
    er, ei = powers(arc, aic, t_row + 1.0)

    for h in range(SSM_GROUP):
        cr, ci = ct_re[0][:, h:h + 1], ct_im[0][:, h:h + 1]
        m = jnp.where(top, er * cr - ei * ci, -(er * ci + ei * cr))
        m_ref[0, :, h * chunk:(h + 1) * chunk] = m.astype(BF16)

    first16 = lax.broadcasted_iota(jnp.int32, (SSM_GROUP, two_p), 1) < SSM_STATE
    crr, cir = cr_r[0], ci_r[0]
    g2 = jnp.concatenate(
        [jnp.where(first16, crr * bbr[hq:hq + 1, :] - cir * bbi[hq:hq + 1, :],
                   -(crr * bbi[hq:hq + 1, :] + cir * bbr[hq:hq + 1, :])) for hq in range(SSM_GROUP)],
        axis=0)
    taps = _dot(g2, jnp.where(top, pw_r, pw_i), _HI)

    causal = (lax.broadcasted_iota(jnp.int32, (chunk, chunk), 0)
              <= lax.broadcasted_iota(jnp.int32, (chunk, chunk), 1))
    for hq in range(SSM_GROUP):
        for h in range(SSM_GROUP):
            row = hq * SSM_GROUP + h
            k_rows = jnp.broadcast_to(taps[row:row + 1, :], (chunk, chunk))
            toep = pltpu.roll(k_rows, 0, axis=1, stride=1, stride_axis=0)
            t_ref[0, hq * chunk:(hq + 1) * chunk, h * chunk:(h + 1) * chunk] = (
                jnp.where(causal, toep, 0.0).astype(BF16))

    pr, pi_ = lbr, lbi
    for _ in range(S5_CHUNK.bit_length() - 1):
        pr, pi_ = pr * pr - pi_ * pi_, 2.0 * pr * pi_
    a_ref[0, 0:1, :] = pr
    a_ref[0, 1:2, :] = pi_


def _s5_params(lre, lim, ldt, b_re, b_im, c_re, c_im):
    g, p = lre.shape
    dup = lambda a: jnp.concatenate([a, a], axis=-1)
    lre_r, lim_r = dup(lre)[:, None, :], dup(lim)[:, None, :]
    lre_c, lim_c = dup(lre)[:, :, None], dup(lim)[:, :, None]
    bt_re = dup(jnp.swapaxes(b_re, 1, 2))
    bt_im = dup(jnp.swapaxes(b_im, 1, 2))
    ct = lambda c: jnp.concatenate([jnp.swapaxes(c, 1, 2)] * 2, axis=1)
    ct_re, ct_im = ct(c_re), ct(c_im)
    ldt3 = ldt[:, None, None]
    return (lre_r, lim_r, lre_c, lim_c, ldt3, bt_re, bt_im, ct_re, ct_im, dup(c_re), dup(c_im))


N_S5_PARAMS = 11


def _s5_kernel(cpb, *refs):
    params = refs[:N_S5_PARAMS]
    u_ref, d_ref, y_ref, t_scr, w_scr, m_scr, a_scr = refs[N_S5_PARAMS:]
    _s5_param_kernel(*params, t_scr, w_scr, m_scr, a_scr)
    _s5_scan_kernel(cpb, u_ref, t_scr, w_scr, m_scr, a_scr, d_ref, y_ref)


def _s5_scan_kernel(cpb, u_ref, t_ref, w_ref, m_ref, a_ref, d_ref, y_ref):
    us = [u_ref[:, hq, :] for hq in range(SSM_GROUP)]
    ub = jnp.concatenate([u.astype(BF16) for u in us], axis=1)
    rows = ub.shape[0]
    two_p = 2 * SSM_STATE
    h = _dot(ub, w_ref[0])
    ar = a_ref[0, 0:1, :]
    ai = a_ref[0, 1:2, :]
    lane = lax.broadcasted_iota(jnp.int32, (1, two_p), 1)
    sign = jnp.where(lane < SSM_STATE, -1.0, 1.0)
    chunk = lax.broadcasted_iota(jnp.int32, (rows, two_p), 0) % cpb
    step = 1
    while step < cpb:
        prev = jnp.where(chunk >= step, pltpu.roll(h, step, axis=0), 0.0)
        swapped = pltpu.roll(prev, SSM_STATE, axis=1)
        h = h + prev * ar + swapped * (ai * sign)
        ar, ai = ar * ar - ai * ai, 2.0 * ar * ai
        step *= 2
    hprev = jnp.where(chunk >= 1, pltpu.roll(h, 1, axis=0), 0.0)
    y = _dot(ub, t_ref[0]) + _dot(hprev.astype(BF16), m_ref[0])
    for hq in range(SSM_GROUP):
        yh = y[:, hq * S5_CHUNK:(hq + 1) * S5_CHUNK] + d_ref[0, hq:hq + 1, :] * us[hq]
        y_ref[:, hq, :] = jax.nn.gelu(yh)


def _s5(utc, params, dtab, cpb):
    rows = utc.shape[0]
    two_p = 2 * SSM_STATE
    idx = lambda i: (i, 0, 0)
    chan = pl.BlockSpec((rows, SSM_GROUP, S5_CHUNK), lambda i: (0, i, 0))
    return pl.pallas_call(
        functools.partial(_s5_kernel, cpb),
        grid=(SSM_GROUPS,),
        in_specs=([pl.BlockSpec((1,) + a.shape[1:], idx) for a in params]
                  + [chan, pl.BlockSpec((1, SSM_GROUP, S5_CHUNK), idx)]),
        out_specs=chan,
        out_shape=jax.ShapeDtypeStruct(utc.shape, F32),
        scratch_shapes=[pltpu.VMEM((1, S5_COLS, S5_COLS), BF16),
                        pltpu.VMEM((1, S5_COLS, two_p), BF16),
                        pltpu.VMEM((1, two_p, S5_COLS), BF16),
                        pltpu.VMEM((1, 2, two_p), F32)],
        compiler_params=pltpu.CompilerParams(vmem_limit_bytes=VMEM_LIMIT),
        name="s5_scan",
    )(*params, utc, dtab)


def _plan_kernel(nb, tpb, cnt_ref, start_ref, tmap_ref, nused_ref):
    cnt = cnt_ref[0, 0]
    padded = jnp.floor((cnt + (MOBA_BLOCK - 1.0)) * (1.0 / MOBA_BLOCK)) * MOBA_BLOCK
    r_io = lax.broadcasted_iota(jnp.int32, (nb, nb), 0)
    c_io = lax.broadcasted_iota(jnp.int32, (nb, nb), 1)
    start = _dot((c_io < r_io).astype(F32), padded, _HI)
    start_ref[0, 0] = start
    end = (start + padded)[:, 0:1]
    tile_row = lax.broadcasted_iota(jnp.int32, (nb, tpb), 1).astype(F32) * MOBA_BLOCK
    blk_of_tile = jnp.sum(jnp.where(end <= tile_row, 1.0, 0.0), axis=0, keepdims=True)
    tmap_ref[0] = jnp.minimum(blk_of_tile, nb - 1.0).astype(jnp.int32)
    total = jnp.max(end, axis=0, keepdims=True)
    nused_ref[0] = jnp.broadcast_to(total * (1.0 / MOBA_BLOCK), (1, HEAD_PAD)).astype(jnp.int32)


def _plan(cnt, nb, tpb):
    bsz = cnt.shape[0]
    nbh = bsz * ATT_HEADS
    return pl.pallas_call(
        functools.partial(_plan_kernel, nb, tpb),
        grid=(bsz, ATT_HEADS),
        in_specs=[pl.BlockSpec((1, 1, nb, MOBA_BLOCK), lambda b, h: (b, h, 0, 0))],
        out_specs=[
            pl.BlockSpec((1, 1, nb, MOBA_BLOCK), lambda b, h: (b, h, 0, 0)),
            pl.BlockSpec((1, 1, tpb), lambda b, h: (b * ATT_HEADS + h, 0, 0)),
            pl.BlockSpec((1, 1, HEAD_PAD), lambda b, h: (b * ATT_HEADS + h, 0, 0)),
        ],
        out_shape=[
            jax.ShapeDtypeStruct((bsz, ATT_HEADS, nb, MOBA_BLOCK), F32),
            jax.ShapeDtypeStruct((nbh, 1, tpb), jnp.int32),
            jax.ShapeDtypeStruct((nbh, 1, HEAD_PAD), jnp.int32),
        ],
        name="route_plan",
    )(cnt)


def _pos_kernel(nb, cap, trash, sel_ref, rank_ref, start_ref, dst_ref):
    b = pl.program_id(0) // nb
    blk = lax.broadcasted_iota(jnp.int32, (nb, MOBA_BLOCK), 0)
    lane = lax.broadcasted_iota(jnp.int32, (1, MOBA_BLOCK), 1)
    for h in range(ATT_HEADS):
        start = start_ref[0, h]
        base = (b * ATT_HEADS + h) * cap
        for r in range(MOBA_TOPK):
            s = sel_ref[r, h:h + 1, :]
            first = jnp.sum(jnp.where(blk == s, start, 0.0), axis=0, keepdims=True).astype(jnp.int32)
            dst = base + first + rank_ref[r, h:h + 1, :]
            dst_ref[r, h:h + 1, :] = jnp.where(s >= 0, dst, trash + lane)


def _pos(sel, rank, start, nb, cap, trash):
    n = sel.shape[-1]
    blk3 = pl.BlockSpec((MOBA_TOPK, ATT_HEADS, MOBA_BLOCK), lambda i: (0, 0, i))
    return pl.pallas_call(
        functools.partial(_pos_kernel, nb, cap, trash),
        grid=(n // MOBA_BLOCK,),
        in_specs=[blk3, blk3,
                  pl.BlockSpec((1, ATT_HEADS, nb, MOBA_BLOCK), lambda i: (i // nb, 0, 0, 0))],
        out_specs=blk3,
        out_shape=jax.ShapeDtypeStruct(sel.shape, jnp.int32),
        name="route_pos",
    )(sel, rank, start)


def _sc_mesh():
    return plsc.VectorSubcoreMesh(core_axis_name="core", subcore_axis_name="subcore")


def _sc_scatter_rows(x, idx, rows_out):
    nrep, nin = idx.shape

    @pl.kernel(out_type=jax.ShapeDtypeStruct((rows_out, HEAD_PAD), x.dtype), mesh=_sc_mesh(),
               scratch_types=[])
    def scatter(x_hbm, i_hbm, o_hbm):
        def body(x_vmem, *i_vmems):
            for i_vmem in i_vmems:
                pltpu.sync_copy(x_vmem, o_hbm.at[i_vmem.at[0]])

        idx_spec = lambda r: pl.BlockSpec((1, SC_WINDOW), lambda i: (r, i))
        pltpu.emit_pipeline(
            body,
            grid=(nin // SC_WINDOW,),
            in_specs=[pl.BlockSpec((SC_WINDOW, HEAD_PAD), lambda i: (i, 0))]
                     + [idx_spec(r) for r in range(nrep)],
            out_specs=[],
            core_axis_name=("core", "subcore"),
            dimension_semantics=(pltpu.PARALLEL,),
        )(x_hbm, *([i_hbm] * nrep))

    return scatter(x, idx)


def _sc_gather_rows(x, idx):
    nout = idx.shape[1]

    @pl.kernel(out_type=jax.ShapeDtypeStruct((nout, HEAD_PAD), x.dtype), mesh=_sc_mesh())
    def gather(x_hbm, i_hbm, o_hbm):
        def body(i_vmem, o_vmem):
            pltpu.sync_copy(x_hbm.at[i_vmem.at[0]], o_vmem)

        pltpu.emit_pipeline(
            body,
            grid=(nout // SC_WINDOW,),
            in_specs=[pl.BlockSpec((1, SC_WINDOW), lambda i: (0, i))],
            out_specs=[pl.BlockSpec((SC_WINDOW, HEAD_PAD), lambda i: (i, 0))],
            core_axis_name=("core", "subcore"),
            dimension_semantics=(pltpu.PARALLEL,),
        )(i_hbm, o_hbm)

    return gather(x, idx)


def _routed_kernel(tmap_ref, nused_ref, qs_ref, slope_ref, k_ref, vt_ref, o_ref):
    h = pl.program_id(1)
    bh = pl.program_id(0) * ATT_HEADS + h
    s = pl.program_id(2)

    @pl.when(s * ROUTED_TILES < nused_ref[bh])
    def _():
        slope = slope_ref[pl.ds(h, 1), :]
        row = lax.broadcasted_iota(jnp.int32, (HEAD_DIM, MOBA_BLOCK), 0)
        tiles = range(ROUTED_TILES)
        js = [tmap_ref[bh, s * ROUTED_TILES + u] for u in tiles]
        ks = [pl.multiple_of(j * MOBA_BLOCK, MOBA_BLOCK) for j in js]
        zs = [_dot_nt(k_ref[pl.ds(ks[u], MOBA_BLOCK), :],
                      qs_ref[u * MOBA_BLOCK:(u + 1) * MOBA_BLOCK, :].astype(BF16)) for u in tiles]
        ms = [jnp.max(z, axis=0, keepdims=True) for z in zs]
        ps = [jnp.exp(zs[u] - ms[u]) for u in tiles]
        ls = [jnp.sum(p, axis=0, keepdims=True) for p in ps]
        ots = [_dot(vt_ref[:, pl.ds(ks[u], MOBA_BLOCK)], ps[u].astype(BF16)) for u in tiles]
        for u in tiles:
            m = ms[u] + slope * jnp.full((1, MOBA_BLOCK), ks[u], jnp.int32).astype(F32)
            stats = jnp.where(row == 0, m, jnp.where(row == 1, ls[u], 0.0))
            o_ref[u * MOBA_BLOCK:(u + 1) * MOBA_BLOCK, :] = jnp.concatenate([ots[u], stats], axis=0).T


def _routed(qs, ka, vt, slope_tab, tmap, nused, bsz, seq, tpb):
    steps = tpb // ROUTED_TILES
    rows = ROUTED_TILES * MOBA_BLOCK
    shift = ROUTED_TILES.bit_length() - 1

    def step_of(b, h, s, tm, nu):
        bh = b * ATT_HEADS + h
        used = lax.shift_right_logical(nu[bh] + (ROUTED_TILES - 1), shift)
        return bh * steps + jnp.minimum(s, jnp.maximum(used - 1, 0)), 0

    grid_spec = pltpu.PrefetchScalarGridSpec(
        num_scalar_prefetch=2,
        grid=(bsz, ATT_HEADS, steps),
        in_specs=[
            pl.BlockSpec((rows, HEAD_PAD), step_of),
            pl.BlockSpec(slope_tab.shape, lambda b, h, s, tm, nu: (0, 0)),
            pl.BlockSpec((seq, HEAD_PAD), lambda b, h, s, tm, nu: (b, h)),
            pl.BlockSpec((HEAD_DIM, seq), lambda b, h, s, tm, nu: (h, b)),
        ],
        out_specs=pl.BlockSpec((rows, HEAD_PAD), step_of),
    )
    return pl.pallas_call(
        _routed_kernel,
        grid_spec=grid_spec,
        out_shape=jax.ShapeDtypeStruct(qs.shape, F32),
        compiler_params=pltpu.CompilerParams(vmem_limit_bytes=VMEM_LIMIT),
        name="moba_routed",
    )(tmap, nused, qs, slope_tab, ka, vt)


def _combine_kernel(nb, q_ref, k_ref, vt_ref, sel_ref, g_ref, slope_ref, o_ref):
    il = pl.program_id(0) % nb
    kio = lax.broadcasted_iota(jnp.int32, (MOBA_BLOCK, MOBA_BLOCK), 0)
    qio = lax.broadcasted_iota(jnp.int32, (MOBA_BLOCK, MOBA_BLOCK), 1)
    causal = kio <= qio
    own_shift = jnp.full((1, MOBA_BLOCK), il * MOBA_BLOCK, jnp.int32).astype(F32)
    outs = []
    heads = range(ATT_HEADS)
    zs = [_dot_nt(k_ref[:, h * HEAD_PAD:(h + 1) * HEAD_PAD], q_ref[h].astype(BF16)) for h in heads]
    zs = [jnp.where(causal, z, NEG_BIG) for z in zs]
    ms = [jnp.max(z, axis=0, keepdims=True) for z in zs]
    ps = [jnp.exp(zs[h] - ms[h]) for h in heads]
    ls = [jnp.sum(p, axis=0, keepdims=True) for p in ps]
    os_ = [_dot(vt_ref[h * HEAD_DIM:(h + 1) * HEAD_DIM, :], ps[h].astype(BF16)) for h in heads]
    for h in heads:
        l0, o0 = ls[h], os_[h]
        m0 = ms[h] + slope_ref[h:h + 1, :] * own_shift
        parts = []
        for r in range(MOBA_TOPK):
            gt = g_ref[r, h].T
            valid = sel_ref[r, h:h + 1, :] >= 0
            parts.append((jnp.where(valid, gt[HEAD_DIM:HEAD_DIM + 1, :], NEG_BIG),
                          jnp.where(valid, gt[HEAD_DIM + 1:HEAD_DIM + 2, :], 0.0),
                          jnp.where(valid, gt[:HEAD_DIM, :], 0.0)))
        m = m0
        for mr, _, _ in parts:
            m = jnp.maximum(m, mr)
        w = jnp.exp(m0 - m)
        num, den = w * o0, w * l0
        for mr, lr, orr in parts:
            w = jnp.exp(mr - m)
            num, den = num + w * orr, den + w * lr
        outs.append(num / den)
    o_ref[...] = jnp.concatenate(outs, axis=0).T.astype(BF16)


def _combine(qhm, ka, vt, sel, g4, slope_tab, nb):
    n = ka.shape[0]
    return pl.pallas_call(
        functools.partial(_combine_kernel, nb),
        grid=(n // MOBA_BLOCK,),
        in_specs=[
            pl.BlockSpec((ATT_HEADS, MOBA_BLOCK, HEAD_PAD), lambda i: (0, i, 0)),
            pl.BlockSpec((MOBA_BLOCK, QK_PAD), lambda i: (i, 0)),
            pl.BlockSpec((ATT_WIDTH, MOBA_BLOCK), lambda i: (0, i)),
            pl.BlockSpec((MOBA_TOPK, ATT_HEADS, MOBA_BLOCK), lambda i: (0, 0, i)),
            pl.BlockSpec((MOBA_TOPK, ATT_HEADS, MOBA_BLOCK, HEAD_PAD), lambda i: (0, 0, i, 0)),
            pl.BlockSpec(slope_tab.shape, lambda i: (0, 0)),
        ],
        out_specs=pl.BlockSpec((MOBA_BLOCK, ATT_WIDTH), lambda i: (i, 0)),
        out_shape=jax.ShapeDtypeStruct((n, ATT_WIDTH), BF16),
        compiler_params=pltpu.CompilerParams(vmem_limit_bytes=VMEM_LIMIT),
        name="moba_own_combine",
    )(qhm, ka, vt, sel, g4, slope_tab)


def _merge_kernel(x_ref, bra_ref, ys_ref, brc_ref, wg_ref, wa_ref, wb_ref, wc_ref, gw_ref, gb_ref,
                  wo_ref, lg_ref, lb_ref, o_ref):
    x = x_ref[...]
    xb = x.astype(BF16)

    def gate(k):
        return jax.nn.sigmoid(_dot(xb, wg_ref[:, k * D_MODEL:(k + 1) * D_MODEL]))

    merged = gate(0) * _dot(bra_ref[...], wa_ref[...])
    ys = jnp.concatenate([ys_ref[c].T for c in range(ys_ref.shape[0])], axis=0)
    brb = ys * jax.nn.sigmoid(_dot(ys.astype(BF16), gw_ref[...]) + gb_ref[...])
    merged = merged + gate(1) * _dot(brb.astype(BF16), wb_ref[...])
    merged = merged + gate(2) * _dot(brc_ref[...], wc_ref[...])
    mix = _dot(merged.astype(BF16), wo_ref[...])
    o_ref[...] = _layer_norm(DN_ALPHA * x + mix, lg_ref[...], lb_ref[...])


def _merge(x2, bra, ys, brc, wg, wa, wb, wc, gw, gb, wo, lg, lb, tm):
    n = x2.shape[0]
    const = lambda i: (0, 0)
    tile = lambda w: pl.BlockSpec((tm, w), lambda i: (i, 0))
    full = lambda a: pl.BlockSpec(a.shape, const)
    return pl.pallas_call(
        _merge_kernel,
        grid=(n // tm,),
        in_specs=[tile(D_MODEL), tile(SGU_WIDTH),
                  pl.BlockSpec((tm // S5_CHUNK, SSM_WIDTH, S5_CHUNK), lambda i: (i, 0, 0)),
                  tile(ATT_WIDTH),
                  full(wg), full(wa), full(wb), full(wc), full(gw), full(gb), full(wo), full(lg), full(lb)],
        out_specs=tile(D_MODEL),
        out_shape=jax.ShapeDtypeStruct((n, D_MODEL), F32),
        compiler_params=pltpu.CompilerParams(vmem_limit_bytes=VMEM_LIMIT),
        name="merge_ln",
    )(x2, bra, ys, brc, wg, wa, wb, wc, gw, gb, wo, lg, lb)


FF_CHUNK = D_FF // 2


def _ffn_kernel(x_ref, w1_ref, w3_ref, w2_ref, lg_ref, lb_ref, o_ref):
    x = x_ref[...]
    xb = x.astype(BF16)
    acc = None
    for c in range(D_FF // FF_CHUNK):
        cs = slice(c * FF_CHUNK, (c + 1) * FF_CHUNK)
        h = (jax.nn.silu(_dot(xb, w1_ref[:, cs])) * _dot(xb, w3_ref[:, cs])).astype(BF16)
        part = _dot(h, w2_ref[cs, :])
        acc = part if acc is None else acc + part
    o_ref[...] = _layer_norm(DN_ALPHA * x + acc, lg_ref[...], lb_ref[...])


def _ffn(x2, w1, w3, w2, lg, lb, tm):
    n = x2.shape[0]
    const = lambda i: (0, 0)
    full = lambda a: pl.BlockSpec(a.shape, const)
    return pl.pallas_call(
        _ffn_kernel,
        grid=(n // tm,),
        in_specs=[pl.BlockSpec((tm, D_MODEL), lambda i: (i, 0)),
                  full(w1), full(w3), full(w2), full(lg), full(lb)],
        out_specs=pl.BlockSpec((tm, D_MODEL), lambda i: (i, 0)),
        out_shape=jax.ShapeDtypeStruct((n, D_MODEL), F32),
        compiler_params=pltpu.CompilerParams(vmem_limit_bytes=VMEM_LIMIT),
        name="ffn_ln",
    )(x2, w1, w3, w2, lg, lb)


def _pad_heads(w):
    d = w.shape[0]
    w = w.reshape(d, ATT_HEADS, HEAD_DIM)
    return jnp.pad(w, ((0, 0), (0, 0), (0, HEAD_PAD - HEAD_DIM))).reshape(d, QK_PAD)


def _alibi_extras():
    slopes = 2.0 ** (-8.0 * jnp.arange(1, ATT_HEADS + 1, dtype=F32) / ATT_HEADS)
    row = jnp.arange(MOBA_BLOCK, dtype=F32)
    eq = jnp.zeros((MOBA_BLOCK, ATT_HEADS, HEAD_PAD), F32)
    eq = eq.at[:, :, Q_LANE_ONE].set(1.0).at[:, :, Q_LANE_ROW].set(row[:, None])
    ek = jnp.zeros((MOBA_BLOCK, ATT_HEADS, HEAD_PAD), F32)
    ek = ek.at[:, :, Q_LANE_ONE].set(row[:, None] * slopes[None, :])
    ek = ek.at[:, :, Q_LANE_BLK].set(-slopes[None, :] * MOBA_BLOCK).at[:, :, Q_LANE_ROW].set(-slopes[None, :])
    slope_tab = jnp.broadcast_to(slopes[:, None], (ATT_HEADS, MOBA_BLOCK))
    return eq.reshape(MOBA_BLOCK, QK_PAD), ek.reshape(MOBA_BLOCK, QK_PAD), slope_tab


def kernel(x, w_in, sgu_ln_g, sgu_ln_b, sgu_w, sgu_b, ssm_lambda_re, ssm_lambda_im, ssm_log_dt,
           ssm_b_re, ssm_b_im, ssm_c_re, ssm_c_im, ssm_d, glu_w, glu_b, w_branch_a, w_branch_b,
           w_branch_c, w_out, ln1_g, ln1_b, ffn_w1, ffn_w3, ffn_w2, ln2_g, ln2_b):
    bsz, seq, _ = x.shape
    n = bsz * seq
    nb = seq // MOBA_BLOCK
    cpb = seq // S5_CHUNK
    tm = 512 if n % 512 == 0 else MOBA_BLOCK
    eq, ek, slope_tab = _alibi_extras()
    scale = HEAD_DIM ** -0.5
    cap = (MOBA_TOPK + 1) * seq
    tpb = cap // MOBA_BLOCK
    trash = bsz * ATT_HEADS * cap
    o_q = 2 * SGU_WIDTH + SSM_WIDTH
    o_g = o_q + 3 * ATT_WIDTH

    x2 = x.reshape(n, D_MODEL)
    for l in range(DEPTH):
        wl = w_in[l]
        wz = wl[:, :2 * SGU_WIDTH].astype(BF16)
        wst = wl[:, 2 * SGU_WIDTH:o_q].T.astype(BF16)
        wq = _pad_heads(wl[:, o_q:o_q + ATT_WIDTH] * scale).astype(BF16)
        wk = _pad_heads(wl[:, o_q + ATT_WIDTH:o_q + 2 * ATT_WIDTH]).astype(BF16)
        wvt = wl[:, o_q + 2 * ATT_WIDTH:o_g].T.astype(BF16)
        wg = wl[:, o_g:].astype(BF16)

        bra, utc, qhm, ka, vt, sel, rank, cnt = _inproj(
            x2, wz, wst, wq, wk, wvt, eq, ek, sgu_ln_g[l][None, :], sgu_ln_b[l][None, :],
            sgu_w[l].astype(BF16), sgu_b[l].T, nb)

        s5_params = _s5_params(
            ssm_lambda_re[l], ssm_lambda_im[l], ssm_log_dt[l], ssm_b_re[l], ssm_b_im[l],
            ssm_c_re[l], ssm_c_im[l])
        dtab = jnp.broadcast_to(ssm_d[l].reshape(SSM_GROUPS, SSM_GROUP, 1),
                                (SSM_GROUPS, SSM_GROUP, S5_CHUNK))
        ys = _s5(utc, s5_params, dtab, cpb)

        start, tmap, nused = _plan(cnt, nb, tpb)
        dst = _pos(sel, rank, start, nb, cap, trash).reshape(MOBA_TOPK, ATT_HEADS * n)
        qs = _sc_scatter_rows(qhm.reshape(ATT_HEADS * n, HEAD_PAD), dst, trash + MOBA_BLOCK)
        part = _routed(qs, ka, vt, slope_tab, tmap[:, 0, :], nused[:, 0, 0], bsz, seq, tpb)
        g = _sc_gather_rows(part, dst.reshape(1, MOBA_TOPK * ATT_HEADS * n))
        brc = _combine(qhm, ka, vt, sel, g.reshape(MOBA_TOPK, ATT_HEADS, n, HEAD_PAD), slope_tab, nb)

        x2 = _merge(x2, bra, ys, brc, wg, w_branch_a[l].astype(BF16), w_branch_b[l].astype(BF16),
                    w_branch_c[l].astype(BF16), glu_w[l].astype(BF16), glu_b[l][None, :],
                    w_out[l].astype(BF16), ln1_g[l][None, :], ln1_b[l][None, :], tm)
        x2 = _ffn(x2, ffn_w1[l].astype(BF16), ffn_w3[l].astype(BF16), ffn_w2[l].astype(BF16),
                  ln2_g[l][None, :], ln2_b[l][None, :], tm)
    return x2.reshape(bsz, seq, D_MODEL)
```

```python
import functools

import jax
import jax.numpy as jnp
from jax import lax
from jax.experimental import pallas as pl
from jax.experimental.pallas import tpu as pltpu
from jax.experimental.pallas import tpu_sc as plsc

F32 = jnp.float32
BF16 = jnp.bfloat16

D_MODEL = 1024
SGU_CHUNK = 128
SGU_GROUPS = 4
SGU_WIDTH = 512
SSM_WIDTH = 512
SSM_GROUP = 16
SSM_GROUPS = 32
SSM_STATE = 64
ATT_HEADS = 8
HEAD_DIM = 64
ATT_WIDTH = 512
MOBA_BLOCK = 256
MOBA_TOPK = 3
D_FF = 2816
DEPTH = 2
DN_ALPHA = (2 * DEPTH) ** 0.25
LN_EPS = 1e-5
NEG_BIG = -1e30

HEAD_PAD = 128
QK_PAD = ATT_HEADS * HEAD_PAD
Q_LANE_ONE = HEAD_DIM
Q_LANE_BLK = HEAD_DIM + 1
Q_LANE_ROW = HEAD_DIM + 2
SC_WINDOW = 128
ROUTED_TILES = 8
S5_CHUNK = 128
S5_COLS = S5_CHUNK * SSM_GROUP
VMEM_LIMIT = 56 * 1024 * 1024

_HI = lax.Precision.HIGHEST


def _dot(a, b, precision=None):
    return jnp.dot(a, b, preferred_element_type=F32, precision=precision)


def _dot_nt(a, b, precision=None):
    return lax.dot_general(a, b, (((1,), (1,)), ((), ())),
                           preferred_element_type=F32, precision=precision)


def _layer_norm(x, g, b):
    mu = jnp.mean(x, axis=-1, keepdims=True)
    xc = x - mu
    var = jnp.mean(xc * xc, axis=-1, keepdims=True)
    return xc * lax.rsqrt(var + LN_EPS) * g + b


def _inproj_kernel(nb, x_ref, wz_ref, wst_ref, wq_ref, wk_ref, wvt_ref, eq_ref, ek_ref, lng_ref, lnb_ref,
                   sw_ref, sbt_ref, bra_ref, u_ref, q_ref, k_ref, vt_ref, sel_ref, rank_ref, cnt_ref,
                   kmean_ref, carry_ref):
    i = pl.program_id(0)
    il = i % nb

    @pl.when(i == 0)
    def _():
        kmean_ref[...] = jnp.zeros_like(kmean_ref)

    @pl.when(il == 0)
    def _():
        carry_ref[...] = jnp.zeros_like(carry_ref)

    xb = x_ref[...].astype(BF16)

    z = jax.nn.gelu(_dot(xb, wz_ref[...]))
    u = z[:, :SGU_WIDTH]
    vn = _layer_norm(z[:, SGU_WIDTH:], lng_ref[...], lnb_ref[...]).astype(BF16)
    r_io = lax.broadcasted_iota(jnp.int32, (SGU_CHUNK, SGU_CHUNK), 0)
    c_io = lax.broadcasted_iota(jnp.int32, (SGU_CHUNK, SGU_CHUNK), 1)
    tril = r_io >= c_io
    for g in range(SGU_GROUPS):
        w = jnp.where(tril, sw_ref[g], jnp.zeros((), BF16))
        bias = sbt_ref[:, g:g + 1]
        gs = slice(g * SGU_CHUNK, (g + 1) * SGU_CHUNK)
        for c in range(MOBA_BLOCK // SGU_CHUNK):
            rs = slice(c * SGU_CHUNK, (c + 1) * SGU_CHUNK)
            mixed = _dot(w, vn[rs, gs]) + bias
            bra_ref[rs, gs] = (u[rs, gs] * mixed).astype(BF16)

    ut = _dot_nt(wst_ref[...], xb)
    for c in range(MOBA_BLOCK // S5_CHUNK):
        u_ref[c] = ut[:, c * S5_CHUNK:(c + 1) * S5_CHUNK]

    lane = lax.broadcasted_iota(jnp.int32, (1, QK_PAD), 1) & (HEAD_PAD - 1)
    qa = _dot(xb, wq_ref[...]) + eq_ref[...] + jnp.where(lane == Q_LANE_BLK, il.astype(F32), 0.0)
    for h in range(ATT_HEADS):
        q_ref[0, h] = qa[:, h * HEAD_PAD:(h + 1) * HEAD_PAD]
    ka = _dot(xb, wk_ref[...])
    k_ref[...] = (ka + ek_ref[...]).astype(BF16)
    kmean_ref[pl.ds(il, 1), :] = jnp.mean(ka, axis=0, keepdims=True)
    vt_ref[0] = _dot_nt(wvt_ref[...], xb).astype(BF16)

    blk = lax.broadcasted_iota(jnp.int32, (nb, MOBA_BLOCK), 0)
    neg_inf = jnp.full((), -jnp.inf, F32)
    r_io = lax.broadcasted_iota(jnp.int32, (MOBA_BLOCK, MOBA_BLOCK), 0)
    c_io = lax.broadcasted_iota(jnp.int32, (MOBA_BLOCK, MOBA_BLOCK), 1)
    earlier = (r_io < c_io).astype(BF16)
    heads = range(ATT_HEADS)
    past = blk < il
    gates = []
    for h in heads:
        hs = slice(h * HEAD_PAD, h * HEAD_PAD + HEAD_DIM)
        gates.append(jnp.where(past, _dot_nt(kmean_ref[:, hs], qa[:, hs], precision=_HI), neg_inf))
    sels = [[] for _ in heads]
    for r in range(MOBA_TOPK):
        ms = [jnp.max(g, axis=0, keepdims=True) for g in gates]
        idxs = [jnp.min(jnp.where(gates[h] == ms[h], blk, nb), axis=0, keepdims=True) for h in heads]
        for h in heads:
            sels[h].append(jnp.where(r < il, idxs[h], -1))
        gates = [jnp.where(blk == idxs[h], neg_inf, gates[h]) for h in heads]
    hits = [[blk == s for s in sels[h]] for h in heads]
    onehots = [jnp.where(hits[h][0] | hits[h][1] | hits[h][2], 1.0, 0.0) for h in heads]
    befores = [carry_ref[h] + _dot(onehots[h].astype(BF16), earlier) for h in heads]
    for h in heads:
        carry_ref[h] = carry_ref[h] + jnp.sum(onehots[h], axis=1, keepdims=True)
        for r in range(MOBA_TOPK):
            sel_ref[0, r, h:h + 1, :] = sels[h][r]
            rank = jnp.sum(jnp.where(hits[h][r], befores[h], 0.0), axis=0, keepdims=True)
            rank_ref[0, r, h:h + 1, :] = rank.astype(jnp.int32)

    @pl.when(il == nb - 1)
    def _():
        cnt_ref[0] = carry_ref[...]


def _inproj(x2, wz, wst, wq, wk, wvt, eq, ek, lng, lnb, sw, sbt, nb):
    n = x2.shape[0]
    grid = n // MOBA_BLOCK
    seq = nb * MOBA_BLOCK
    bsz = n // seq
    const = lambda i: (0, 0)
    return pl.pallas_call(
        functools.partial(_inproj_kernel, nb),
        grid=(grid,),
        in_specs=[
            pl.BlockSpec((MOBA_BLOCK, D_MODEL), lambda i: (i, 0)),
            pl.BlockSpec(wz.shape, const),
            pl.BlockSpec(wst.shape, const),
            pl.BlockSpec(wq.shape, const),
            pl.BlockSpec(wk.shape, const),
            pl.BlockSpec(wvt.shape, const),
            pl.BlockSpec(eq.shape, const),
            pl.BlockSpec(ek.shape, const),
            pl.BlockSpec(lng.shape, const),
            pl.BlockSpec(lnb.shape, const),
            pl.BlockSpec(sw.shape, lambda i: (0, 0, 0)),
            pl.BlockSpec(sbt.shape, const),
        ],
        out_specs=[
            pl.BlockSpec((MOBA_BLOCK, SGU_WIDTH), lambda i: (i, 0)),
            pl.BlockSpec((MOBA_BLOCK // S5_CHUNK, SSM_WIDTH, S5_CHUNK), lambda i: (i, 0, 0)),
            pl.BlockSpec((1, ATT_HEADS, MOBA_BLOCK, HEAD_PAD), lambda i: (i // nb, 0, i % nb, 0)),
            pl.BlockSpec((MOBA_BLOCK, QK_PAD), lambda i: (i, 0)),
            pl.BlockSpec((1, ATT_WIDTH, MOBA_BLOCK), lambda i: (i // nb, 0, i % nb)),
            pl.BlockSpec((1, MOBA_TOPK, ATT_HEADS, MOBA_BLOCK), lambda i: (i // nb, 0, 0, i % nb)),
            pl.BlockSpec((1, MOBA_TOPK, ATT_HEADS, MOBA_BLOCK), lambda i: (i // nb, 0, 0, i % nb)),
            pl.BlockSpec((1, ATT_HEADS, nb, MOBA_BLOCK), lambda i: (i // nb, 0, 0, 0)),
        ],
        out_shape=[
            jax.ShapeDtypeStruct((n, SGU_WIDTH), BF16),
            jax.ShapeDtypeStruct((n // S5_CHUNK, SSM_WIDTH, S5_CHUNK), F32),
            jax.ShapeDtypeStruct((bsz, ATT_HEADS, seq, HEAD_PAD), F32),
            jax.ShapeDtypeStruct((n, QK_PAD), BF16),
            jax.ShapeDtypeStruct((bsz, ATT_WIDTH, seq), BF16),
            jax.ShapeDtypeStruct((bsz, MOBA_TOPK, ATT_HEADS, seq), jnp.int32),
            jax.ShapeDtypeStruct((bsz, MOBA_TOPK, ATT_HEADS, seq), jnp.int32),
            jax.ShapeDtypeStruct((bsz, ATT_HEADS, nb, MOBA_BLOCK), F32),
        ],
        scratch_shapes=[pltpu.VMEM((nb, QK_PAD), F32),
                        pltpu.VMEM((ATT_HEADS, nb, MOBA_BLOCK), F32)],
        compiler_params=pltpu.CompilerParams(
            dimension_semantics=("arbitrary",), vmem_limit_bytes=VMEM_LIMIT),
        name="inproj_sgu_gate",
    )(x2, wz, wst, wq, wk, wvt, eq, ek, lng, lnb, sw, sbt)


def _s5_param_kernel(lre_r, lim_r, lre_c, lim_c, ldt, bt_re, bt_im, ct_re, ct_im, cr_r, ci_r,
                     t_ref, w_ref, m_ref, a_ref):
    two_p = 2 * SSM_STATE
    dt = jnp.exp(ldt[0])

    def powers(ar, ai, e):
        mag = jnp.exp(ar * e)
        return mag * jnp.cos(ai * e), mag * jnp.sin(ai * e)

    lr, li = lre_r[0], lim_r[0]
    ar, ai = lr * dt, li * dt
    lbr, lbi = powers(ar, ai, 1.0)
    den = lr * lr + li * li
    cfr = ((lbr - 1.0) * lr + lbi * li) / den
    cfi = (lbi * lr - (lbr - 1.0) * li) / den
    bbr = cfr * bt_re[0] - cfi * bt_im[0]
    bbi = cfr * bt_im[0] + cfi * bt_re[0]

    chunk = S5_CHUNK
    s_col = lax.broadcasted_iota(jnp.int32, (chunk, two_p), 0).astype(F32)
    first = lax.broadcasted_iota(jnp.int32, (chunk, two_p), 1) < SSM_STATE
    rev_r, rev_i = powers(ar, ai, (chunk - 1.0) - s_col)

    for hq in range(SSM_GROUP):
        br, bi = bbr[hq:hq + 1, :], bbi[hq:hq + 1, :]
        w = jnp.where(first, rev_r * br - rev_i * bi, rev_r * bi + rev_i * br)
        w_ref[0, hq * chunk:(hq + 1) * chunk, :] = w.astype(BF16)

    lrc, lic = lre_c[0], lim_c[0]
    arc, aic = lrc * dt, lic * dt
    t_row = lax.broadcasted_iota(jnp.int32, (two_p, chunk), 1).astype(F32)
    top = lax.broadcasted_iota(jnp.int32, (two_p, chunk), 0) < SSM_STATE
    pw_r, pw_i = powers(arc, aic, t_row)
    er, ei = powers(arc, aic, t_row + 1.0)

    for h in range(SSM_GROUP):
        cr, ci = ct_re[0][:, h:h + 1], ct_im[0][:, h:h + 1]
        m = jnp.where(top, er * cr - ei * ci, -(er * ci + ei * cr))
        m_ref[0, :, h * chunk:(h + 1) * chunk] = m.astype(BF16)

    first16 = lax.broadcasted_iota(jnp.int32, (SSM_GROUP, two_p), 1) < SSM_STATE
    crr, cir = cr_r[0], ci_r[0]
    g2 = jnp.concatenate(
        [jnp.where(first16, crr * bbr[hq:hq + 1, :] - cir * bbi[hq:hq + 1, :],
                   -(crr * bbi[hq:hq + 1, :] + cir * bbr[hq:hq + 1, :])) for hq in range(SSM_GROUP)],
        axis=0)
    taps = _dot(g2, jnp.where(top, pw_r, pw_i), _HI)

    causal = (lax.broadcasted_iota(jnp.int32, (chunk, chunk), 0)
              <= lax.broadcasted_iota(jnp.int32, (chunk, chunk), 1))
    for hq in range(SSM_GROUP):
        for h in range(SSM_GROUP):
            row = hq * SSM_GROUP + h
            k_rows = jnp.broadcast_to(taps[row:row + 1, :], (chunk, chunk))
            toep = pltpu.roll(k_rows, 0, axis=1, stride=1, stride_axis=0)
            t_ref[0, hq * chunk:(hq + 1) * chunk, h * chunk:(h + 1) * chunk] = (
                jnp.where(causal, toep, 0.0).astype(BF16))

    pr, pi_ = lbr, lbi
    for _ in range(S5_CHUNK.bit_length() - 1):
        pr, pi_ = pr * pr - pi_ * pi_, 2.0 * pr * pi_
    a_ref[0, 0:1, :] = pr
    a_ref[0, 1:2, :] = pi_


def _s5_params(lre, lim, ldt, b_re, b_im, c_re, c_im):
    g, p = lre.shape
    dup = lambda a: jnp.concatenate([a, a], axis=-1)
    lre_r, lim_r = dup(lre)[:, None, :], dup(lim)[:, None, :]
    lre_c, lim_c = dup(lre)[:, :, None], dup(lim)[:, :, None]
    bt_re = dup(jnp.swapaxes(b_re, 1, 2))
    bt_im = dup(jnp.swapaxes(b_im, 1, 2))
    ct = lambda c: jnp.concatenate([jnp.swapaxes(c, 1, 2)] * 2, axis=1)
    ct_re, ct_im = ct(c_re), ct(c_im)
    ldt3 = ldt[:, None, None]
    return (lre_r, lim_r, lre_c, lim_c, ldt3, bt_re, bt_im, ct_re, ct_im, dup(c_re), dup(c_im))


N_S5_PARAMS = 11


def _s5_kernel(cpb, *refs):
    params = refs[:N_S5_PARAMS]
    u_ref, d_ref, y_ref, t_scr, w_scr, m_scr, a_scr = refs[N_S5_PARAMS:]
    _s5_param_kernel(*params, t_scr, w_scr, m_scr, a_scr)
    _s5_scan_kernel(cpb, u_ref, t_scr, w_scr, m_scr, a_scr, d_ref, y_ref)


def _s5_scan_kernel(cpb, u_ref, t_ref, w_ref, m_ref, a_ref, d_ref, y_ref):
    us = [u_ref[:, hq, :] for hq in range(SSM_GROUP)]
    ub = jnp.concatenate([u.astype(BF16) for u in us], axis=1)
    rows = ub.shape[0]
    two_p = 2 * SSM_STATE
    h = _dot(ub, w_ref[0])
    ar = a_ref[0, 0:1, :]
    ai = a_ref[0, 1:2, :]
    lane = lax.broadcasted_iota(jnp.int32, (1, two_p), 1)
    sign = jnp.where(lane < SSM_STATE, -1.0, 1.0)
    chunk = lax.broadcasted_iota(jnp.int32, (rows, two_p), 0) % cpb
    step = 1
    while step < cpb:
        prev = jnp.where(chunk >= step, pltpu.roll(h, step, axis=0), 0.0)
        swapped = pltpu.roll(prev, SSM_STATE, axis=1)
        h = h + prev * ar + swapped * (ai * sign)
        ar, ai = ar * ar - ai * ai, 2.0 * ar * ai
        step *= 2
    hprev = jnp.where(chunk >= 1, pltpu.roll(h, 1, axis=0), 0.0)
    y = _dot(ub, t_ref[0]) + _dot(hprev.astype(BF16), m_ref[0])
    for hq in range(SSM_GROUP):
        yh = y[:, hq * S5_CHUNK:(hq + 1) * S5_CHUNK] + d_ref[0, hq:hq + 1, :] * us[hq]
        y_ref[:, hq, :] = jax.nn.gelu(yh)


def _s5(utc, params, dtab, cpb):
    rows = utc.shape[0]
    two_p = 2 * SSM_STATE
    idx = lambda i: (i, 0, 0)
    chan = pl.BlockSpec((rows, SSM_GROUP, S5_CHUNK), lambda i: (0, i, 0))
    return pl.pallas_call(
        functools.partial(_s5_kernel, cpb),
        grid=(SSM_GROUPS,),
        in_specs=([pl.BlockSpec((1,) + a.shape[1:], idx) for a in params]
                  + [chan, pl.BlockSpec((1, SSM_GROUP, S5_CHUNK), idx)]),
        out_specs=chan,
        out_shape=jax.ShapeDtypeStruct(utc.shape, F32),
        scratch_shapes=[pltpu.VMEM((1, S5_COLS, S5_COLS), BF16),
                        pltpu.VMEM((1, S5_COLS, two_p), BF16),
                        pltpu.VMEM((1, two_p, S5_COLS), BF16),
                        pltpu.VMEM((1, 2, two_p), F32)],
        compiler_params=pltpu.CompilerParams(vmem_limit_bytes=VMEM_LIMIT),
        name="s5_scan",
    )(*params, utc, dtab)


def _plan_kernel(nb, tpb, cnt_ref, start_ref, tmap_ref, nused_ref):
    cnt = cnt_ref[0, 0]
    padded = jnp.floor((cnt + (MOBA_BLOCK - 1.0)) * (1.0 / MOBA_BLOCK)) * MOBA_BLOCK
    r_io = lax.broadcasted_iota(jnp.int32, (nb, nb), 0)
    c_io = lax.broadcasted_iota(jnp.int32, (nb, nb), 1)
    start = _dot((c_io < r_io).astype(F32), padded, _HI)
    start_ref[0, 0] = start
    end = (start + padded)[:, 0:1]
    tile_row = lax.broadcasted_iota(jnp.int32, (nb, tpb), 1).astype(F32) * MOBA_BLOCK
    blk_of_tile = jnp.sum(jnp.where(end <= tile_row, 1.0, 0.0), axis=0, keepdims=True)
    tmap_ref[0] = jnp.minimum(blk_of_tile, nb - 1.0).astype(jnp.int32)
    total = jnp.max(end, axis=0, keepdims=True)
    nused_ref[0] = jnp.broadcast_to(total * (1.0 / MOBA_BLOCK), (1, HEAD_PAD)).astype(jnp.int32)


def _plan(cnt, nb, tpb):
    bsz = cnt.shape[0]
    nbh = bsz * ATT_HEADS
    return pl.pallas_call(
        functools.partial(_plan_kernel, nb, tpb),
        grid=(bsz, ATT_HEADS),
        in_specs=[pl.BlockSpec((1, 1, nb, MOBA_BLOCK), lambda b, h: (b, h, 0, 0))],
        out_specs=[
            pl.BlockSpec((1, 1, nb, MOBA_BLOCK), lambda b, h: (b, h, 0, 0)),
            pl.BlockSpec((1, 1, tpb), lambda b, h: (b * ATT_HEADS + h, 0, 0)),
            pl.BlockSpec((1, 1, HEAD_PAD), lambda b, h: (b * ATT_HEADS + h, 0, 0)),
        ],
        out_shape=[
            jax.ShapeDtypeStruct((bsz, ATT_HEADS, nb, MOBA_BLOCK), F32),
            jax.ShapeDtypeStruct((nbh, 1, tpb), jnp.int32),
            jax.ShapeDtypeStruct((nbh, 1, HEAD_PAD), jnp.int32),
        ],
        name="route_plan",
    )(cnt)


def _pos_kernel(nb, cap, trash, sel_ref, rank_ref, start_ref, dst_ref):
    b = pl.program_id(0) // nb
    blk = lax.broadcasted_iota(jnp.int32, (nb, MOBA_BLOCK), 0)
    lane = lax.broadcasted_iota(jnp.int32, (1, MOBA_BLOCK), 1)
    for h in range(ATT_HEADS):
        start = start_ref[0, h]
        base = (b * ATT_HEADS + h) * cap
        for r in range(MOBA_TOPK):
            s = sel_ref[r, h:h + 1, :]
            first = jnp.sum(jnp.where(blk == s, start, 0.0), axis=0, keepdims=True).astype(jnp.int32)
            dst = base + first + rank_ref[r, h:h + 1, :]
            dst_ref[r, h:h + 1, :] = jnp.where(s >= 0, dst, trash + lane)


def _pos(sel, rank, start, nb, cap, trash):
    n = sel.shape[-1]
    blk3 = pl.BlockSpec((MOBA_TOPK, ATT_HEADS, MOBA_BLOCK), lambda i: (0, 0, i))
    return pl.pallas_call(
        functools.partial(_pos_kernel, nb, cap, trash),
        grid=(n // MOBA_BLOCK,),
        in_specs=[blk3, blk3,
                  pl.BlockSpec((1, ATT_HEADS, nb, MOBA_BLOCK), lambda i: (i // nb, 0, 0, 0))],
        out_specs=blk3,
        out_shape=jax.ShapeDtypeStruct(sel.shape, jnp.int32),
        name="route_pos",
    )(sel, rank, start)


def _sc_mesh():
    return plsc.VectorSubcoreMesh(core_axis_name="core", subcore_axis_name="subcore")


def _sc_scatter_rows(x, idx, rows_out):
    nrep, nin = idx.shape

    @pl.kernel(out_type=jax.ShapeDtypeStruct((rows_out, HEAD_PAD), x.dtype), mesh=_sc_mesh(),
               scratch_types=[])
    def scatter(x_hbm, i_hbm, o_hbm):
        def body(x_vmem, *i_vmems):
            for i_vmem in i_vmems:
                pltpu.sync_copy(x_vmem, o_hbm.at[i_vmem.at[0]])

        idx_spec = lambda r: pl.BlockSpec((1, SC_WINDOW), lambda i: (r, i))
        pltpu.emit_pipeline(
            body,
            grid=(nin // SC_WINDOW,),
            in_specs=[pl.BlockSpec((SC_WINDOW, HEAD_PAD), lambda i: (i, 0))]
                     + [idx_spec(r) for r in range(nrep)],
            out_specs=[],
            core_axis_name=("core", "subcore"),
            dimension_semantics=(pltpu.PARALLEL,),
        )(x_hbm, *([i_hbm] * nrep))

    return scatter(x, idx)


def _sc_gather_rows(x, idx):
    nout = idx.shape[1]

    @pl.kernel(out_type=jax.ShapeDtypeStruct((nout, HEAD_PAD), x.dtype), mesh=_sc_mesh())
    def gather(x_hbm, i_hbm, o_hbm):
        def body(i_vmem, o_vmem):
            pltpu.sync_copy(x_hbm.at[i_vmem.at[0]], o_vmem)

        pltpu.emit_pipeline(
            body,
            grid=(nout // SC_WINDOW,),
            in_specs=[pl.BlockSpec((1, SC_WINDOW), lambda i: (0, i))],
            out_specs=[pl.BlockSpec((SC_WINDOW, HEAD_PAD), lambda i: (i, 0))],
            core_axis_name=("core", "subcore"),
            dimension_semantics=(pltpu.PARALLEL,),
        )(i_hbm, o_hbm)

    return gather(x, idx)


def _routed_kernel(tmap_ref, nused_ref, qs_ref, slope_ref, k_ref, vt_ref, o_ref):
    h = pl.program_id(1)
    bh = pl.program_id(0) * ATT_HEADS + h
    s = pl.program_id(2)

    @pl.when(s * ROUTED_TILES < nused_ref[bh])
    def _():
        slope = slope_ref[pl.ds(h, 1), :]
        row = lax.broadcasted_iota(jnp.int32, (HEAD_DIM, MOBA_BLOCK), 0)
        tiles = range(ROUTED_TILES)
        js = [tmap_ref[bh, s * ROUTED_TILES + u] for u in tiles]
        ks = [pl.multiple_of(j * MOBA_BLOCK, MOBA_BLOCK) for j in js]
        zs = [_dot_nt(k_ref[pl.ds(ks[u], MOBA_BLOCK), :],
                      qs_ref[u * MOBA_BLOCK:(u + 1) * MOBA_BLOCK, :].astype(BF16)) for u in tiles]
        ms = [jnp.max(z, axis=0, keepdims=True) for z in zs]
        ps = [jnp.exp(zs[u] - ms[u]) for u in tiles]
        ls = [jnp.sum(p, axis=0, keepdims=True) for p in ps]
        ots = [_dot(vt_ref[:, pl.ds(ks[u], MOBA_BLOCK)], ps[u].astype(BF16)) for u in tiles]
        for u in tiles:
            m = ms[u] + slope * jnp.full((1, MOBA_BLOCK), ks[u], jnp.int32).astype(F32)
            stats = jnp.where(row == 0, m, jnp.where(row == 1, ls[u], 0.0))
            o_ref[u * MOBA_BLOCK:(u + 1) * MOBA_BLOCK, :] = jnp.concatenate([ots[u], stats], axis=0).T


def _routed(qs, ka, vt, slope_tab, tmap, nused, bsz, seq, tpb):
    steps = tpb // ROUTED_TILES
    rows = ROUTED_TILES * MOBA_BLOCK
    shift = ROUTED_TILES.bit_length() - 1

    def step_of(b, h, s, tm, nu):
        bh = b * ATT_HEADS + h
        used = lax.shift_right_logical(nu[bh] + (ROUTED_TILES - 1), shift)
        return bh * steps + jnp.minimum(s, jnp.maximum(used - 1, 0)), 0

    grid_spec = pltpu.PrefetchScalarGridSpec(
        num_scalar_prefetch=2,
        grid=(bsz, ATT_HEADS, steps),
        in_specs=[
            pl.BlockSpec((rows, HEAD_PAD), step_of),
            pl.BlockSpec(slope_tab.shape, lambda b, h, s, tm, nu: (0, 0)),
            pl.BlockSpec((seq, HEAD_PAD), lambda b, h, s, tm, nu: (b, h)),
            pl.BlockSpec((HEAD_DIM, seq), lambda b, h, s, tm, nu: (h, b)),
        ],
        out_specs=pl.BlockSpec((rows, HEAD_PAD), step_of),
    )
    return pl.pallas_call(
        _routed_kernel,
        grid_spec=grid_spec,
        out_shape=jax.ShapeDtypeStruct(qs.shape, F32),
        compiler_params=pltpu.CompilerParams(vmem_limit_bytes=VMEM_LIMIT),
        name="moba_routed",
    )(tmap, nused, qs, slope_tab, ka, vt)


def _combine_kernel(nb, q_ref, k_ref, vt_ref, sel_ref, g_ref, slope_ref, o_ref):
    il = pl.program_id(0) % nb
    kio = lax.broadcasted_iota(jnp.int32, (MOBA_BLOCK, MOBA_BLOCK), 0)
    qio = lax.broadcasted_iota(jnp.int32, (MOBA_BLOCK, MOBA_BLOCK), 1)
    causal = kio <= qio
    own_shift = jnp.full((1, MOBA_BLOCK), il * MOBA_BLOCK, jnp.int32).astype(F32)
    outs = []
    heads = range(ATT_HEADS)
    zs = [_dot_nt(k_ref[:, h * HEAD_PAD:(h + 1) * HEAD_PAD], q_ref[h].astype(BF16)) for h in heads]
    zs = [jnp.where(causal, z, NEG_BIG) for z in zs]
    ms = [jnp.max(z, axis=0, keepdims=True) for z in zs]
    ps = [jnp.exp(zs[h] - ms[h]) for h in heads]
    ls = [jnp.sum(p, axis=0, keepdims=True) for p in ps]
    os_ = [_dot(vt_ref[h * HEAD_DIM:(h + 1) * HEAD_DIM, :], ps[h].astype(BF16)) for h in heads]
    for h in heads:
        l0, o0 = ls[h], os_[h]
        m0 = ms[h] + slope_ref[h:h + 1, :] * own_shift
        parts = []
        for r in range(MOBA_TOPK):
            gt = g_ref[r, h].T
            valid = sel_ref[r, h:h + 1, :] >= 0
            parts.append((jnp.where(valid, gt[HEAD_DIM:HEAD_DIM + 1, :], NEG_BIG),
                          jnp.where(valid, gt[HEAD_DIM + 1:HEAD_DIM + 2, :], 0.0),
                          jnp.where(valid, gt[:HEAD_DIM, :], 0.0)))
        m = m0
        for mr, _, _ in parts:
            m = jnp.maximum(m, mr)
        w = jnp.exp(m0 - m)
        num, den = w * o0, w * l0
        for mr, lr, orr in parts:
            w = jnp.exp(mr - m)
            num, den = num + w * orr, den + w * lr
        outs.append(num / den)
    o_ref[...] = jnp.concatenate(outs, axis=0).T.astype(BF16)


def _combine(qhm, ka, vt, sel, g4, slope_tab, nb):
    n = ka.shape[0]
    return pl.pallas_call(
        functools.partial(_combine_kernel, nb),
        grid=(n // MOBA_BLOCK,),
        in_specs=[
            pl.BlockSpec((ATT_HEADS, MOBA_BLOCK, HEAD_PAD), lambda i: (0, i, 0)),
            pl.BlockSpec((MOBA_BLOCK, QK_PAD), lambda i: (i, 0)),
            pl.BlockSpec((ATT_WIDTH, MOBA_BLOCK), lambda i: (0, i)),
            pl.BlockSpec((MOBA_TOPK, ATT_HEADS, MOBA_BLOCK), lambda i: (0, 0, i)),
            pl.BlockSpec((MOBA_TOPK, ATT_HEADS, MOBA_BLOCK, HEAD_PAD), lambda i: (0, 0, i, 0)),
            pl.BlockSpec(slope_tab.shape, lambda i: (0, 0)),
        ],
        out_specs=pl.BlockSpec((MOBA_BLOCK, ATT_WIDTH), lambda i: (i, 0)),
        out_shape=jax.ShapeDtypeStruct((n, ATT_WIDTH), BF16),
        compiler_params=pltpu.CompilerParams(vmem_limit_bytes=VMEM_LIMIT),
        name="moba_own_combine",
    )(qhm, ka, vt, sel, g4, slope_tab)


def _merge_kernel(x_ref, bra_ref, ys_ref, brc_ref, wg_ref, wa_ref, wb_ref, wc_ref, gw_ref, gb_ref,
                  wo_ref, lg_ref, lb_ref, o_ref):
    x = x_ref[...]
    xb = x.astype(BF16)

    def gate(k):
        return jax.nn.sigmoid(_dot(xb, wg_ref[:, k * D_MODEL:(k + 1) * D_MODEL]))

    merged = gate(0) * _dot(bra_ref[...], wa_ref[...])
    ys = jnp.concatenate([ys_ref[c].T for c in range(ys_ref.shape[0])], axis=0)
    brb = ys * jax.nn.sigmoid(_dot(ys.astype(BF16), gw_ref[...]) + gb_ref[...])
    merged = merged + gate(1) * _dot(brb.astype(BF16), wb_ref[...])
    merged = merged + gate(2) * _dot(brc_ref[...], wc_ref[...])
    mix = _dot(merged.astype(BF16), wo_ref[...])
    o_ref[...] = _layer_norm(DN_ALPHA * x + mix, lg_ref[...], lb_ref[...])


def _merge(x2, bra, ys, brc, wg, wa, wb, wc, gw, gb, wo, lg, lb, tm):
    n = x2.shape[0]
    const = lambda i: (0, 0)
    tile = lambda w: pl.BlockSpec((tm, w), lambda i: (i, 0))
    full = lambda a: pl.BlockSpec(a.shape, const)
    return pl.pallas_call(
        _merge_kernel,
        grid=(n // tm,),
        in_specs=[tile(D_MODEL), tile(SGU_WIDTH),
                  pl.BlockSpec((tm // S5_CHUNK, SSM_WIDTH, S5_CHUNK), lambda i: (i, 0, 0)),
                  tile(ATT_WIDTH),
                  full(wg), full(wa), full(wb), full(wc), full(gw), full(gb), full(wo), full(lg), full(lb)],
        out_specs=tile(D_MODEL),
        out_shape=jax.ShapeDtypeStruct((n, D_MODEL), F32),
        compiler_params=pltpu.CompilerParams(vmem_limit_bytes=VMEM_LIMIT),
        name="merge_ln",
    )(x2, bra, ys, brc, wg, wa, wb, wc, gw, gb, wo, lg, lb)


FF_CHUNK = D_FF // 2


def _ffn_kernel(x_ref, w1_ref, w3_ref, w2_ref, lg_ref, lb_ref, o_ref):
    x = x_ref[...]
    xb = x.astype(BF16)
    acc = None
    for c in range(D_FF // FF_CHUNK):
        cs = slice(c * FF_CHUNK, (c + 1) * FF_CHUNK)
        h = (jax.nn.silu(_dot(xb, w1_ref[:, cs])) * _dot(xb, w3_ref[:, cs])).astype(BF16)
        part = _dot(h, w2_ref[cs, :])
        acc = part if acc is None else acc + part
    o_ref[...] = _layer_norm(DN_ALPHA * x + acc, lg_ref[...], lb_ref[...])


def _ffn(x2, w1, w3, w2, lg, lb, tm):
    n = x2.shape[0]
    const = lambda i: (0, 0)
    full = lambda a: pl.BlockSpec(a.shape, const)
    return pl.pallas_call(
        _ffn_kernel,
        grid=(n // tm,),
        in_specs=[pl.BlockSpec((tm, D_MODEL), lambda i: (i, 0)),
                  full(w1), full(w3), full(w2), full(lg), full(lb)],
        out_specs=pl.BlockSpec((tm, D_MODEL), lambda i: (i, 0)),
        out_shape=jax.ShapeDtypeStruct((n, D_MODEL), F32),
        compiler_params=pltpu.CompilerParams(vmem_limit_bytes=VMEM_LIMIT),
        name="ffn_ln",
    )(x2, w1, w3, w2, lg, lb)


def _pad_heads(w):
    d = w.shape[0]
    w = w.reshape(d, ATT_HEADS, HEAD_DIM)
    return jnp.pad(w, ((0, 0), (0, 0), (0, HEAD_PAD - HEAD_DIM))).reshape(d, QK_PAD)


def _alibi_extras():
    slopes = 2.0 ** (-8.0 * jnp.arange(1, ATT_HEADS + 1, dtype=F32) / ATT_HEADS)
    row = jnp.arange(MOBA_BLOCK, dtype=F32)
    eq = jnp.zeros((MOBA_BLOCK, ATT_HEADS, HEAD_PAD), F32)
    eq = eq.at[:, :, Q_LANE_ONE].set(1.0).at[:, :, Q_LANE_ROW].set(row[:, None])
    ek = jnp.zeros((MOBA_BLOCK, ATT_HEADS, HEAD_PAD), F32)
    ek = ek.at[:, :, Q_LANE_ONE].set(row[:, None] * slopes[None, :])
    ek = ek.at[:, :, Q_LANE_BLK].set(-slopes[None, :] * MOBA_BLOCK).at[:, :, Q_LANE_ROW].set(-slopes[None, :])
    slope_tab = jnp.broadcast_to(slopes[:, None], (ATT_HEADS, MOBA_BLOCK))
    return eq.reshape(MOBA_BLOCK, QK_PAD), ek.reshape(MOBA_BLOCK, QK_PAD), slope_tab


def kernel(x, w_in, sgu_ln_g, sgu_ln_b, sgu_w, sgu_b, ssm_lambda_re, ssm_lambda_im, ssm_log_dt,
           ssm_b_re, ssm_b_im, ssm_c_re, ssm_c_im, ssm_d, glu_w, glu_b, w_branch_a, w_branch_b,
           w_branch_c, w_out, ln1_g, ln1_b, ffn_w1, ffn_w3, ffn_w2, ln2_g, ln2_b):
    bsz, seq, _ = x.shape
    n = bsz * seq
    nb = seq // MOBA_BLOCK
    cpb = seq // S5_CHUNK
    tm = 512 if n % 512 == 0 else MOBA_BLOCK
    eq, ek, slope_tab = _alibi_extras()
    scale = HEAD_DIM ** -0.5
    cap = (MOBA_TOPK + 1) * seq
    tpb = cap // MOBA_BLOCK
    trash = ATT_HEADS * cap
    o_q = 2 * SGU_WIDTH + SSM_WIDTH
    o_g = o_q + 3 * ATT_WIDTH

    x2 = x.reshape(n, D_MODEL)
    for l in range(DEPTH):
        wl = w_in[l]
        wz = wl[:, :2 * SGU_WIDTH].astype(BF16)
        wst = wl[:, 2 * SGU_WIDTH:o_q].T.astype(BF16)
        wq = _pad_heads(wl[:, o_q:o_q + ATT_WIDTH] * scale).astype(BF16)
        wk = _pad_heads(wl[:, o_q + ATT_WIDTH:o_q + 2 * ATT_WIDTH]).astype(BF16)
        wvt = wl[:, o_q + 2 * ATT_WIDTH:o_g].T.astype(BF16)
        wg = wl[:, o_g:].astype(BF16)

        bra, utc, qhm, ka, vt, sel, rank, cnt = _inproj(
            x2, wz, wst, wq, wk, wvt, eq, ek, sgu_ln_g[l][None, :], sgu_ln_b[l][None, :],
            sgu_w[l].astype(BF16), sgu_b[l].T, nb)

        s5_params = _s5_params(
            ssm_lambda_re[l], ssm_lambda_im[l], ssm_log_dt[l], ssm_b_re[l], ssm_b_im[l],
            ssm_c_re[l], ssm_c_im[l])
        dtab = jnp.broadcast_to(ssm_d[l].reshape(SSM_GROUPS, SSM_GROUP, 1),
                                (SSM_GROUPS, SSM_GROUP, S5_CHUNK))
        ys = _s5(utc, s5_params, dtab, cpb)

        brcs = []
        for b in range(bsz):
            ka_b = ka[b * seq:(b + 1) * seq]
            start, tmap, nused = _plan(cnt[b:b + 1], nb, tpb)
            dst = _pos(sel[b], rank[b], start, nb, cap, trash).reshape(MOBA_TOPK, ATT_HEADS * seq)
            qs = _sc_scatter_rows(qhm[b].reshape(ATT_HEADS * seq, HEAD_PAD), dst, trash + MOBA_BLOCK)
            part = _routed(qs, ka_b, vt[b], slope_tab, tmap[:, 0, :], nused[:, 0, 0], 1, seq, tpb)
            g = _sc_gather_rows(part, dst.reshape(1, MOBA_TOPK * ATT_HEADS * seq))
            brcs.append(_combine(qhm[b], ka_b, vt[b], sel[b],
                                 g.reshape(MOBA_TOPK, ATT_HEADS, seq, HEAD_PAD), slope_tab, nb))
        brc = jnp.concatenate(brcs, axis=0)

        x2 = _merge(x2, bra, ys, brc, wg, w_branch_a[l].astype(BF16), w_branch_b[l].astype(BF16),
                    w_branch_c[l].astype(BF16), glu_w[l].astype(BF16), glu_b[l][None, :],
                    w_out[l].astype(BF16), ln1_g[l][None, :], ln1_b[l][None, :], tm)
        x2 = _ffn(x2, ffn_w1[l].astype(BF16), ffn_w3[l].astype(BF16), ffn_w2[l].astype(BF16),
                  ln2_g[l][None, :], ln2_b[l][None, :], tm)
    return x2.reshape(bsz, seq, D_MODEL)
```

```python
import functools

import jax
import jax.numpy as jnp
from jax import lax
from jax.experimental import pallas as pl
from jax.experimental.pallas import tpu as pltpu
from jax.experimental.pallas import tpu_sc as plsc

F32 = jnp.float32
BF16 = jnp.bfloat16

D_MODEL = 1024
SGU_CHUNK = 128
SGU_GROUPS = 4
SGU_WIDTH = 512
SSM_WIDTH = 512
SSM_GROUP = 16
SSM_GROUPS = 32
SSM_STATE = 64
ATT_HEADS = 8
HEAD_DIM = 64
ATT_WIDTH = 512
MOBA_BLOCK = 256
MOBA_TOPK = 3
D_FF = 2816
DEPTH = 2
DN_ALPHA = (2 * DEPTH) ** 0.25
LN_EPS = 1e-5
NEG_BIG = -1e30

HEAD_PAD = 128
QK_PAD = ATT_HEADS * HEAD_PAD
Q_LANE_ONE = HEAD_DIM
Q_LANE_BLK = HEAD_DIM + 1
Q_LANE_ROW = HEAD_DIM + 2
SC_WINDOW = 128
ROUTED_TILES = 8
S5_CHUNK = 128
S5_COLS = S5_CHUNK * SSM_GROUP
VMEM_LIMIT = 56 * 1024 * 1024

_HI = lax.Precision.HIGHEST


def _dot(a, b, precision=None):
    return jnp.dot(a, b, preferred_element_type=F32, precision=precision)


def _dot_nt(a, b, precision=None):
    return lax.dot_general(a, b, (((1,), (1,)), ((), ())),
                           preferred_element_type=F32, precision=precision)


def _layer_norm(x, g, b):
    mu = jnp.mean(x, axis=-1, keepdims=True)
    xc = x - mu
    var = jnp.mean(xc * xc, axis=-1, keepdims=True)
    return xc * lax.rsqrt(var + LN_EPS) * g + b


def _inproj_kernel(nb, x_ref, wz_ref, wst_ref, wq_ref, wk_ref, wvt_ref, eq_ref, ek_ref, lng_ref, lnb_ref,
                   sw_ref, sbt_ref, bra_ref, u_ref, q_ref, k_ref, vt_ref, sel_ref, rank_ref, cnt_ref,
                   kmean_ref, carry_ref):
    i = pl.program_id(0)
    il = i % nb

    @pl.when(i == 0)
    def _():
        kmean_ref[...] = jnp.zeros_like(kmean_ref)

    @pl.when(il == 0)
    def _():
        carry_ref[...] = jnp.zeros_like(carry_ref)

    xb = x_ref[...].astype(BF16)

    z = jax.nn.gelu(_dot(xb, wz_ref[...]))
    u = z[:, :SGU_WIDTH]
    vn = _layer_norm(z[:, SGU_WIDTH:], lng_ref[...], lnb_ref[...]).astype(BF16)
    r_io = lax.broadcasted_iota(jnp.int32, (SGU_CHUNK, SGU_CHUNK), 0)
    c_io = lax.broadcasted_iota(jnp.int32, (SGU_CHUNK, SGU_CHUNK), 1)
    tril = r_io >= c_io
    for g in range(SGU_GROUPS):
        w = jnp.where(tril, sw_ref[g], jnp.zeros((), BF16))
        bias = sbt_ref[:, g:g + 1]
        gs = slice(g * SGU_CHUNK, (g + 1) * SGU_CHUNK)
        for c in range(MOBA_BLOCK // SGU_CHUNK):
            rs = slice(c * SGU_CHUNK, (c + 1) * SGU_CHUNK)
            mixed = _dot(w, vn[rs, gs]) + bias
            bra_ref[rs, gs] = (u[rs, gs] * mixed).astype(BF16)

    ut = _dot_nt(wst_ref[...], xb)
    for c in range(MOBA_BLOCK // S5_CHUNK):
        u_ref[c] = ut[:, c * S5_CHUNK:(c + 1) * S5_CHUNK]

    lane = lax.broadcasted_iota(jnp.int32, (1, QK_PAD), 1) & (HEAD_PAD - 1)
    qa = _dot(xb, wq_ref[...]) + eq_ref[...] + jnp.where(lane == Q_LANE_BLK, il.astype(F32), 0.0)
    for h in range(ATT_HEADS):
        q_ref[0, h] = qa[:, h * HEAD_PAD:(h + 1) * HEAD_PAD]
    ka = _dot(xb, wk_ref[...])
    k_ref[...] = (ka + ek_ref[...]).astype(BF16)
    kmean_ref[pl.ds(il, 1), :] = jnp.mean(ka, axis=0, keepdims=True)
    vt_ref[0] = _dot_nt(wvt_ref[...], xb).astype(BF16)

    blk = lax.broadcasted_iota(jnp.int32, (nb, MOBA_BLOCK), 0)
    neg_inf = jnp.full((), -jnp.inf, F32)
    r_io = lax.broadcasted_iota(jnp.int32, (MOBA_BLOCK, MOBA_BLOCK), 0)
    c_io = lax.broadcasted_iota(jnp.int32, (MOBA_BLOCK, MOBA_BLOCK), 1)
    earlier = (r_io < c_io).astype(BF16)
    heads = range(ATT_HEADS)
    past = blk < il
    gates = []
    for h in heads:
        hs = slice(h * HEAD_PAD, h * HEAD_PAD + HEAD_DIM)
        gates.append(jnp.where(past, _dot_nt(kmean_ref[:, hs], qa[:, hs], precision=_HI), neg_inf))
    sels = [[] for _ in heads]
    for r in range(MOBA_TOPK):
        ms = [jnp.max(g, axis=0, keepdims=True) for g in gates]
        idxs = [jnp.min(jnp.where(gates[h] == ms[h], blk, nb), axis=0, keepdims=True) for h in heads]
        for h in heads:
            sels[h].append(jnp.where(r < il, idxs[h], -1))
        gates = [jnp.where(blk == idxs[h], neg_inf, gates[h]) for h in heads]
    hits = [[blk == s for s in sels[h]] for h in heads]
    onehots = [jnp.where(hits[h][0] | hits[h][1] | hits[h][2], 1.0, 0.0) for h in heads]
    befores = [carry_ref[h] + _dot(onehots[h].astype(BF16), earlier) for h in heads]
    for h in heads:
        carry_ref[h] = carry_ref[h] + jnp.sum(onehots[h], axis=1, keepdims=True)
        for r in range(MOBA_TOPK):
            sel_ref[0, r, h:h + 1, :] = sels[h][r]
            rank = jnp.sum(jnp.where(hits[h][r], befores[h], 0.0), axis=0, keepdims=True)
            rank_ref[0, r, h:h + 1, :] = rank.astype(jnp.int32)

    @pl.when(il == nb - 1)
    def _():
        cnt_ref[0] = carry_ref[...]


def _inproj(x2, wz, wst, wq, wk, wvt, eq, ek, lng, lnb, sw, sbt, nb):
    n = x2.shape[0]
    grid = n // MOBA_BLOCK
    seq = nb * MOBA_BLOCK
    bsz = n // seq
    const = lambda i: (0, 0)
    return pl.pallas_call(
        functools.partial(_inproj_kernel, nb),
        grid=(grid,),
        in_specs=[
            pl.BlockSpec((MOBA_BLOCK, D_MODEL), lambda i: (i, 0)),
            pl.BlockSpec(wz.shape, const),
            pl.BlockSpec(wst.shape, const),
            pl.BlockSpec(wq.shape, const),
            pl.BlockSpec(wk.shape, const),
            pl.BlockSpec(wvt.shape, const),
            pl.BlockSpec(eq.shape, const),
            pl.BlockSpec(ek.shape, const),
            pl.BlockSpec(lng.shape, const),
            pl.BlockSpec(lnb.shape, const),
            pl.BlockSpec(sw.shape, lambda i: (0, 0, 0)),
            pl.BlockSpec(sbt.shape, const),
        ],
        out_specs=[
            pl.BlockSpec((MOBA_BLOCK, SGU_WIDTH), lambda i: (i, 0)),
            pl.BlockSpec((MOBA_BLOCK // S5_CHUNK, SSM_WIDTH, S5_CHUNK), lambda i: (i, 0, 0)),
            pl.BlockSpec((1, ATT_HEADS, MOBA_BLOCK, HEAD_PAD), lambda i: (i // nb, 0, i % nb, 0)),
            pl.BlockSpec((MOBA_BLOCK, QK_PAD), lambda i: (i, 0)),
            pl.BlockSpec((1, ATT_WIDTH, MOBA_BLOCK), lambda i: (i // nb, 0, i % nb)),
            pl.BlockSpec((1, MOBA_TOPK, ATT_HEADS, MOBA_BLOCK), lambda i: (i // nb, 0, 0, i % nb)),
            pl.BlockSpec((1, MOBA_TOPK, ATT_HEADS, MOBA_BLOCK), lambda i: (i // nb, 0, 0, i % nb)),
            pl.BlockSpec((1, ATT_HEADS, nb, MOBA_BLOCK), lambda i: (i // nb, 0, 0, 0)),
        ],
        out_shape=[
            jax.ShapeDtypeStruct((n, SGU_WIDTH), BF16),
            jax.ShapeDtypeStruct((n // S5_CHUNK, SSM_WIDTH, S5_CHUNK), F32),
            jax.ShapeDtypeStruct((bsz, ATT_HEADS, seq, HEAD_PAD), F32),
            jax.ShapeDtypeStruct((n, QK_PAD), BF16),
            jax.ShapeDtypeStruct((bsz, ATT_WIDTH, seq), BF16),
            jax.ShapeDtypeStruct((bsz, MOBA_TOPK, ATT_HEADS, seq), jnp.int32),
            jax.ShapeDtypeStruct((bsz, MOBA_TOPK, ATT_HEADS, seq), jnp.int32),
            jax.ShapeDtypeStruct((bsz, ATT_HEADS, nb, MOBA_BLOCK), F32),
        ],
        scratch_shapes=[pltpu.VMEM((nb, QK_PAD), F32),
                        pltpu.VMEM((ATT_HEADS, nb, MOBA_BLOCK), F32)],
        compiler_params=pltpu.CompilerParams(
            dimension_semantics=("arbitrary",), vmem_limit_bytes=VMEM_LIMIT),
        name="inproj_sgu_gate",
    )(x2, wz, wst, wq, wk, wvt, eq, ek, lng, lnb, sw, sbt)


def _s5_param_kernel(lre_r, lim_r, lre_c, lim_c, ldt, bt_re, bt_im, ct_re, ct_im, cr_r, ci_r,
                     t_ref, w_ref, m_ref, a_ref):
    two_p = 2 * SSM_STATE
    dt = jnp.exp(ldt[0])

    def powers(ar, ai, e):
        mag = jnp.exp(ar * e)
        return mag * jnp.cos(ai * e), mag * jnp.sin(ai * e)

    lr, li = lre_r[0], lim_r[0]
    ar, ai = lr * dt, li * dt
    lbr, lbi = powers(ar, ai, 1.0)
    den = lr * lr + li * li
    cfr = ((lbr - 1.0) * lr + lbi * li) / den
    cfi = (lbi * lr - (lbr - 1.0) * li) / den
    bbr = cfr * bt_re[0] - cfi * bt_im[0]
    bbi = cfr * bt_im[0] + cfi * bt_re[0]

    chunk = S5_CHUNK
    s_col = lax.broadcasted_iota(jnp.int32, (chunk, two_p), 0).astype(F32)
    first = lax.broadcasted_iota(jnp.int32, (chunk, two_p), 1) < SSM_STATE
    rev_r, rev_i = powers(ar, ai, (chunk - 1.0) - s_col)

    for hq in range(SSM_GROUP):
        br, bi = bbr[hq:hq + 1, :], bbi[hq:hq + 1, :]
        w = jnp.where(first, rev_r * br - rev_i * bi, rev_r * bi + rev_i * br)
        w_ref[0, hq * chunk:(hq + 1) * chunk, :] = w.astype(BF16)

    lrc, lic = lre_c[0], lim_c[0]
    arc, aic = lrc * dt, lic * dt
    t_row = lax.broadcasted_iota(jnp.int32, (two_p, chunk), 1).astype(F32)
    top = lax.broadcasted_iota(jnp.int32, (two_p, chunk), 0) < SSM_STATE
    pw_r, pw_i = powers(arc, aic, t_row)
    er, ei = powers(arc, aic, t_row + 1.0)

    for h in range(SSM_GROUP):
        cr, ci = ct_re[0][:, h:h + 1], ct_im[0][:, h:h + 1]
        m = jnp.where(top, er * cr - ei * ci, -(er * ci + ei * cr))
        m_ref[0, :, h * chunk:(h + 1) * chunk] = m.astype(BF16)

    first16 = lax.broadcasted_iota(jnp.int32, (SSM_GROUP, two_p), 1) < SSM_STATE
    crr, cir = cr_r[0], ci_r[0]
    g2 = jnp.concatenate(
        [jnp.where(first16, crr * bbr[hq:hq + 1, :] - cir * bbi[hq:hq + 1, :],
                   -(crr * bbi[hq:hq + 1, :] + cir * bbr[hq:hq + 1, :])) for hq in range(SSM_GROUP)],
        axis=0)
    taps = _dot(g2, jnp.where(top, pw_r, pw_i), _HI)

    causal = (lax.broadcasted_iota(jnp.int32, (chunk, chunk), 0)
              <= lax.broadcasted_iota(jnp.int32, (chunk, chunk), 1))
    for hq in range(SSM_GROUP):
        for h in range(SSM_GROUP):
            row = hq * SSM_GROUP + h
            k_rows = jnp.broadcast_to(taps[row:row + 1, :], (chunk, chunk))
            toep = pltpu.roll(k_rows, 0, axis=1, stride=1, stride_axis=0)
            t_ref[0, hq * chunk:(hq + 1) * chunk, h * chunk:(h + 1) * chunk] = (
                jnp.where(causal, toep, 0.0).astype(BF16))

    pr, pi_ = lbr, lbi
    for _ in range(S5_CHUNK.bit_length() - 1):
        pr, pi_ = pr * pr - pi_ * pi_, 2.0 * pr * pi_
    a_ref[0, 0:1, :] = pr
    a_ref[0, 1:2, :] = pi_


def _s5_params(lre, lim, ldt, b_re, b_im, c_re, c_im):
    g, p = lre.shape
    dup = lambda a: jnp.concatenate([a, a], axis=-1)
    lre_r, lim_r = dup(lre)[:, None, :], dup(lim)[:, None, :]
    lre_c, lim_c = dup(lre)[:, :, None], dup(lim)[:, :, None]
    bt_re = dup(jnp.swapaxes(b_re, 1, 2))
    bt_im = dup(jnp.swapaxes(b_im, 1, 2))
    ct = lambda c: jnp.concatenate([jnp.swapaxes(c, 1, 2)] * 2, axis=1)
    ct_re, ct_im = ct(c_re), ct(c_im)
    ldt3 = ldt[:, None, None]
    return (lre_r, lim_r, lre_c, lim_c, ldt3, bt_re, bt_im, ct_re, ct_im, dup(c_re), dup(c_im))


N_S5_PARAMS = 11


def _s5_kernel(cpb, nseq, *refs):
    params = refs[:N_S5_PARAMS]
    u_refs = refs[N_S5_PARAMS:N_S5_PARAMS + nseq]
    d_ref = refs[N_S5_PARAMS + nseq]
    y_refs = refs[N_S5_PARAMS + nseq + 1:N_S5_PARAMS + 2 * nseq + 1]
    t_scr, w_scr, m_scr, a_scr = refs[N_S5_PARAMS + 2 * nseq + 1:]
    _s5_param_kernel(*params, t_scr, w_scr, m_scr, a_scr)
    _s5_scan_kernel(cpb, u_refs, t_scr, w_scr, m_scr, a_scr, d_ref, y_refs)


def _s5_scan_kernel(cpb, u_refs, t_ref, w_ref, m_ref, a_ref, d_ref, y_refs):
    us = [jnp.concatenate([u_ref[:, hq, :] for u_ref in u_refs], axis=0)
          for hq in range(SSM_GROUP)]
    ub = jnp.concatenate([u.astype(BF16) for u in us], axis=1)
    rows = ub.shape[0]
    two_p = 2 * SSM_STATE
    h = _dot(ub, w_ref[0])
    ar = a_ref[0, 0:1, :]
    ai = a_ref[0, 1:2, :]
    lane = lax.broadcasted_iota(jnp.int32, (1, two_p), 1)
    sign = jnp.where(lane < SSM_STATE, -1.0, 1.0)
    chunk = lax.broadcasted_iota(jnp.int32, (rows, two_p), 0) % cpb
    step = 1
    while step < cpb:
        prev = jnp.where(chunk >= step, pltpu.roll(h, step, axis=0), 0.0)
        swapped = pltpu.roll(prev, SSM_STATE, axis=1)
        h = h + prev * ar + swapped * (ai * sign)
        ar, ai = ar * ar - ai * ai, 2.0 * ar * ai
        step *= 2
    hprev = jnp.where(chunk >= 1, pltpu.roll(h, 1, axis=0), 0.0)
    y = _dot(ub, t_ref[0]) + _dot(hprev.astype(BF16), m_ref[0])
    for hq in range(SSM_GROUP):
        yh = jax.nn.gelu(y[:, hq * S5_CHUNK:(hq + 1) * S5_CHUNK] + d_ref[0, hq:hq + 1, :] * us[hq])
        for b, y_ref in enumerate(y_refs):
            y_ref[:, hq, :] = yh[b * cpb:(b + 1) * cpb]


def _s5(utcs, params, dtab, cpb):
    nseq = len(utcs)
    two_p = 2 * SSM_STATE
    idx = lambda i: (i, 0, 0)
    chan = pl.BlockSpec((cpb, SSM_GROUP, S5_CHUNK), lambda i: (0, i, 0))
    return pl.pallas_call(
        functools.partial(_s5_kernel, cpb, nseq),
        grid=(SSM_GROUPS,),
        in_specs=([pl.BlockSpec((1,) + a.shape[1:], idx) for a in params]
                  + [chan] * nseq + [pl.BlockSpec((1, SSM_GROUP, S5_CHUNK), idx)]),
        out_specs=[chan] * nseq,
        out_shape=[jax.ShapeDtypeStruct(u.shape, F32) for u in utcs],
        scratch_shapes=[pltpu.VMEM((1, S5_COLS, S5_COLS), BF16),
                        pltpu.VMEM((1, S5_COLS, two_p), BF16),
                        pltpu.VMEM((1, two_p, S5_COLS), BF16),
                        pltpu.VMEM((1, 2, two_p), F32)],
        compiler_params=pltpu.CompilerParams(vmem_limit_bytes=VMEM_LIMIT),
        name="s5_scan",
    )(*params, *utcs, dtab)


def _plan_kernel(nb, tpb, cnt_ref, start_ref, tmap_ref, nused_ref):
    cnt = cnt_ref[0, 0]
    padded = jnp.floor((cnt + (MOBA_BLOCK - 1.0)) * (1.0 / MOBA_BLOCK)) * MOBA_BLOCK
    r_io = lax.broadcasted_iota(jnp.int32, (nb, nb), 0)
    c_io = lax.broadcasted_iota(jnp.int32, (nb, nb), 1)
    start = _dot((c_io < r_io).astype(F32), padded, _HI)
    start_ref[0, 0] = start
    end = (start + padded)[:, 0:1]
    tile_row = lax.broadcasted_iota(jnp.int32, (nb, tpb), 1).astype(F32) * MOBA_BLOCK
    blk_of_tile = jnp.sum(jnp.where(end <= tile_row, 1.0, 0.0), axis=0, keepdims=True)
    tmap_ref[0] = jnp.minimum(blk_of_tile, nb - 1.0).astype(jnp.int32)
    total = jnp.max(end, axis=0, keepdims=True)
    nused_ref[0] = jnp.broadcast_to(total * (1.0 / MOBA_BLOCK), (1, HEAD_PAD)).astype(jnp.int32)


def _plan(cnt, nb, tpb):
    bsz = cnt.shape[0]
    nbh = bsz * ATT_HEADS
    return pl.pallas_call(
        functools.partial(_plan_kernel, nb, tpb),
        grid=(bsz, ATT_HEADS),
        in_specs=[pl.BlockSpec((1, 1, nb, MOBA_BLOCK), lambda b, h: (b, h, 0, 0))],
        out_specs=[
            pl.BlockSpec((1, 1, nb, MOBA_BLOCK), lambda b, h: (b, h, 0, 0)),
            pl.BlockSpec((1, 1, tpb), lambda b, h: (b * ATT_HEADS + h, 0, 0)),
            pl.BlockSpec((1, 1, HEAD_PAD), lambda b, h: (b * ATT_HEADS + h, 0, 0)),
        ],
        out_shape=[
            jax.ShapeDtypeStruct((bsz, ATT_HEADS, nb, MOBA_BLOCK), F32),
            jax.ShapeDtypeStruct((nbh, 1, tpb), jnp.int32),
            jax.ShapeDtypeStruct((nbh, 1, HEAD_PAD), jnp.int32),
        ],
        name="route_plan",
    )(cnt)


def _pos_kernel(nb, cap, trash, sel_ref, rank_ref, start_ref, dst_ref):
    b = pl.program_id(0) // nb
    blk = lax.broadcasted_iota(jnp.int32, (nb, MOBA_BLOCK), 0)
    lane = lax.broadcasted_iota(jnp.int32, (1, MOBA_BLOCK), 1)
    for h in range(ATT_HEADS):
        start = start_ref[0, h]
        base = (b * ATT_HEADS + h) * cap
        for r in range(MOBA_TOPK):
            s = sel_ref[r, h:h + 1, :]
            first = jnp.sum(jnp.where(blk == s, start, 0.0), axis=0, keepdims=True).astype(jnp.int32)
            dst = base + first + rank_ref[r, h:h + 1, :]
            dst_ref[r, h:h + 1, :] = jnp.where(s >= 0, dst, trash + lane)


def _pos(sel, rank, start, nb, cap, trash):
    n = sel.shape[-1]
    blk3 = pl.BlockSpec((MOBA_TOPK, ATT_HEADS, MOBA_BLOCK), lambda i: (0, 0, i))
    return pl.pallas_call(
        functools.partial(_pos_kernel, nb, cap, trash),
        grid=(n // MOBA_BLOCK,),
        in_specs=[blk3, blk3,
                  pl.BlockSpec((1, ATT_HEADS, nb, MOBA_BLOCK), lambda i: (i // nb, 0, 0, 0))],
        out_specs=blk3,
        out_shape=jax.ShapeDtypeStruct(sel.shape, jnp.int32),
        name="route_pos",
    )(sel, rank, start)


def _sc_mesh():
    return plsc.VectorSubcoreMesh(core_axis_name="core", subcore_axis_name="subcore")


def _sc_scatter_rows(x, idx, rows_out):
    nrep, nin = idx.shape

    @pl.kernel(out_type=jax.ShapeDtypeStruct((rows_out, HEAD_PAD), x.dtype), mesh=_sc_mesh(),
               scratch_types=[])
    def scatter(x_hbm, i_hbm, o_hbm):
        def body(x_vmem, *i_vmems):
            for i_vmem in i_vmems:
                pltpu.sync_copy(x_vmem, o_hbm.at[i_vmem.at[0]])

        idx_spec = lambda r: pl.BlockSpec((1, SC_WINDOW), lambda i: (r, i))
        pltpu.emit_pipeline(
            body,
            grid=(nin // SC_WINDOW,),
            in_specs=[pl.BlockSpec((SC_WINDOW, HEAD_PAD), lambda i: (i, 0))]
                     + [idx_spec(r) for r in range(nrep)],
            out_specs=[],
            core_axis_name=("core", "subcore"),
            dimension_semantics=(pltpu.PARALLEL,),
        )(x_hbm, *([i_hbm] * nrep))

    return scatter(x, idx)


def _sc_gather_rows(x, idx):
    nout = idx.shape[1]

    @pl.kernel(out_type=jax.ShapeDtypeStruct((nout, HEAD_PAD), x.dtype), mesh=_sc_mesh())
    def gather(x_hbm, i_hbm, o_hbm):
        def body(i_vmem, o_vmem):
            pltpu.sync_copy(x_hbm.at[i_vmem.at[0]], o_vmem)

        pltpu.emit_pipeline(
            body,
            grid=(nout // SC_WINDOW,),
            in_specs=[pl.BlockSpec((1, SC_WINDOW), lambda i: (0, i))],
            out_specs=[pl.BlockSpec((SC_WINDOW, HEAD_PAD), lambda i: (i, 0))],
            core_axis_name=("core", "subcore"),
            dimension_semantics=(pltpu.PARALLEL,),
        )(i_hbm, o_hbm)

    return gather(x, idx)


def _routed_kernel(tmap_ref, nused_ref, qs_ref, slope_ref, k_ref, vt_ref, o_ref):
    h = pl.program_id(1)
    bh = pl.program_id(0) * ATT_HEADS + h
    s = pl.program_id(2)

    @pl.when(s * ROUTED_TILES < nused_ref[bh])
    def _():
        slope = slope_ref[pl.ds(h, 1), :]
        row = lax.broadcasted_iota(jnp.int32, (HEAD_DIM, MOBA_BLOCK), 0)
        tiles = range(ROUTED_TILES)
        js = [tmap_ref[bh, s * ROUTED_TILES + u] for u in tiles]
        ks = [pl.multiple_of(j * MOBA_BLOCK, MOBA_BLOCK) for j in js]
        zs = [_dot_nt(k_ref[pl.ds(ks[u], MOBA_BLOCK), :],
                      qs_ref[u * MOBA_BLOCK:(u + 1) * MOBA_BLOCK, :].astype(BF16)) for u in tiles]
        ms = [jnp.max(z, axis=0, keepdims=True) for z in zs]
        ps = [jnp.exp(zs[u] - ms[u]) for u in tiles]
        ls = [jnp.sum(p, axis=0, keepdims=True) for p in ps]
        ots = [_dot(vt_ref[:, pl.ds(ks[u], MOBA_BLOCK)], ps[u].astype(BF16)) for u in tiles]
        for u in tiles:
            m = ms[u] + slope * jnp.full((1, MOBA_BLOCK), ks[u], jnp.int32).astype(F32)
            stats = jnp.where(row == 0, m, jnp.where(row == 1, ls[u], 0.0))
            o_ref[u * MOBA_BLOCK:(u + 1) * MOBA_BLOCK, :] = jnp.concatenate([ots[u], stats], axis=0).T


def _routed(qs, ka, vt, slope_tab, tmap, nused, bsz, seq, tpb):
    steps = tpb // ROUTED_TILES
    rows = ROUTED_TILES * MOBA_BLOCK
    shift = ROUTED_TILES.bit_length() - 1

    def step_of(b, h, s, tm, nu):
        bh = b * ATT_HEADS + h
        used = lax.shift_right_logical(nu[bh] + (ROUTED_TILES - 1), shift)
        return bh * steps + jnp.minimum(s, jnp.maximum(used - 1, 0)), 0

    grid_spec = pltpu.PrefetchScalarGridSpec(
        num_scalar_prefetch=2,
        grid=(bsz, ATT_HEADS, steps),
        in_specs=[
            pl.BlockSpec((rows, HEAD_PAD), step_of),
            pl.BlockSpec(slope_tab.shape, lambda b, h, s, tm, nu: (0, 0)),
            pl.BlockSpec((seq, HEAD_PAD), lambda b, h, s, tm, nu: (b, h)),
            pl.BlockSpec((HEAD_DIM, seq), lambda b, h, s, tm, nu: (h, b)),
        ],
        out_specs=pl.BlockSpec((rows, HEAD_PAD), step_of),
    )
    return pl.pallas_call(
        _routed_kernel,
        grid_spec=grid_spec,
        out_shape=jax.ShapeDtypeStruct(qs.shape, F32),
        compiler_params=pltpu.CompilerParams(vmem_limit_bytes=VMEM_LIMIT),
        name="moba_routed",
    )(tmap, nused, qs, slope_tab, ka, vt)


def _combine_kernel(nb, q_ref, k_ref, vt_ref, sel_ref, g_ref, slope_ref, o_ref):
    il = pl.program_id(0) % nb
    kio = lax.broadcasted_iota(jnp.int32, (MOBA_BLOCK, MOBA_BLOCK), 0)
    qio = lax.broadcasted_iota(jnp.int32, (MOBA_BLOCK, MOBA_BLOCK), 1)
    causal = kio <= qio
    own_shift = jnp.full((1, MOBA_BLOCK), il * MOBA_BLOCK, jnp.int32).astype(F32)
    outs = []
    heads = range(ATT_HEADS)
    zs = [_dot_nt(k_ref[:, h * HEAD_PAD:(h + 1) * HEAD_PAD], q_ref[h].astype(BF16)) for h in heads]
    zs = [jnp.where(causal, z, NEG_BIG) for z in zs]
    ms = [jnp.max(z, axis=0, keepdims=True) for z in zs]
    ps = [jnp.exp(zs[h] - ms[h]) for h in heads]
    ls = [jnp.sum(p, axis=0, keepdims=True) for p in ps]
    os_ = [_dot(vt_ref[h * HEAD_DIM:(h + 1) * HEAD_DIM, :], ps[h].astype(BF16)) for h in heads]
    for h in heads:
        l0, o0 = ls[h], os_[h]
        m0 = ms[h] + slope_ref[h:h + 1, :] * own_shift
        parts = []
        for r in range(MOBA_TOPK):
            gt = g_ref[r, h].T
            valid = sel_ref[r, h:h + 1, :] >= 0
            parts.append((jnp.where(valid, gt[HEAD_DIM:HEAD_DIM + 1, :], NEG_BIG),
                          jnp.where(valid, gt[HEAD_DIM + 1:HEAD_DIM + 2, :], 0.0),
                          jnp.where(valid, gt[:HEAD_DIM, :], 0.0)))
        m = m0
        for mr, _, _ in parts:
            m = jnp.maximum(m, mr)
        w = jnp.exp(m0 - m)
        num, den = w * o0, w * l0
        for mr, lr, orr in parts:
            w = jnp.exp(mr - m)
            num, den = num + w * orr, den + w * lr
        outs.append(num / den)
    o_ref[...] = jnp.concatenate(outs, axis=0).T.astype(BF16)


def _combine(qhm, ka, vt, sel, g4, slope_tab, nb):
    n = ka.shape[0]
    return pl.pallas_call(
        functools.partial(_combine_kernel, nb),
        grid=(n // MOBA_BLOCK,),
        in_specs=[
            pl.BlockSpec((ATT_HEADS, MOBA_BLOCK, HEAD_PAD), lambda i: (0, i, 0)),
            pl.BlockSpec((MOBA_BLOCK, QK_PAD), lambda i: (i, 0)),
            pl.BlockSpec((ATT_WIDTH, MOBA_BLOCK), lambda i: (0, i)),
            pl.BlockSpec((MOBA_TOPK, ATT_HEADS, MOBA_BLOCK), lambda i: (0, 0, i)),
            pl.BlockSpec((MOBA_TOPK, ATT_HEADS, MOBA_BLOCK, HEAD_PAD), lambda i: (0, 0, i, 0)),
            pl.BlockSpec(slope_tab.shape, lambda i: (0, 0)),
        ],
        out_specs=pl.BlockSpec((MOBA_BLOCK, ATT_WIDTH), lambda i: (i, 0)),
        out_shape=jax.ShapeDtypeStruct((n, ATT_WIDTH), BF16),
        compiler_params=pltpu.CompilerParams(vmem_limit_bytes=VMEM_LIMIT),
        name="moba_own_combine",
    )(qhm, ka, vt, sel, g4, slope_tab)


def _merge_kernel(x_ref, bra_ref, ys_ref, brc_ref, wg_ref, wa_ref, wb_ref, wc_ref, gw_ref, gb_ref,
                  wo_ref, lg_ref, lb_ref, o_ref):
    x = x_ref[...]
    xb = x.astype(BF16)

    def gate(k):
        return jax.nn.sigmoid(_dot(xb, wg_ref[:, k * D_MODEL:(k + 1) * D_MODEL]))

    merged = gate(0) * _dot(bra_ref[...], wa_ref[...])
    ys = jnp.concatenate([ys_ref[c].T for c in range(ys_ref.shape[0])], axis=0)
    brb = ys * jax.nn.sigmoid(_dot(ys.astype(BF16), gw_ref[...]) + gb_ref[...])
    merged = merged + gate(1) * _dot(brb.astype(BF16), wb_ref[...])
    merged = merged + gate(2) * _dot(brc_ref[...], wc_ref[...])
    mix = _dot(merged.astype(BF16), wo_ref[...])
    o_ref[...] = _layer_norm(DN_ALPHA * x + mix, lg_ref[...], lb_ref[...])


def _merge(x2, bra, ys, brc, wg, wa, wb, wc, gw, gb, wo, lg, lb, tm):
    n = x2.shape[0]
    const = lambda i: (0, 0)
    tile = lambda w: pl.BlockSpec((tm, w), lambda i: (i, 0))
    full = lambda a: pl.BlockSpec(a.shape, const)
    return pl.pallas_call(
        _merge_kernel,
        grid=(n // tm,),
        in_specs=[tile(D_MODEL), tile(SGU_WIDTH),
                  pl.BlockSpec((tm // S5_CHUNK, SSM_WIDTH, S5_CHUNK), lambda i: (i, 0, 0)),
                  tile(ATT_WIDTH),
                  full(wg), full(wa), full(wb), full(wc), full(gw), full(gb), full(wo), full(lg), full(lb)],
        out_specs=tile(D_MODEL),
        out_shape=jax.ShapeDtypeStruct((n, D_MODEL), F32),
        compiler_params=pltpu.CompilerParams(vmem_limit_bytes=VMEM_LIMIT),
        name="merge_ln",
    )(x2, bra, ys, brc, wg, wa, wb, wc, gw, gb, wo, lg, lb)


FF_CHUNK = D_FF // 2


def _ffn_kernel(x_ref, w1_ref, w3_ref, w2_ref, lg_ref, lb_ref, o_ref):
    x = x_ref[...]
    xb = x.astype(BF16)
    acc = None
    for c in range(D_FF // FF_CHUNK):
        cs = slice(c * FF_CHUNK, (c + 1) * FF_CHUNK)
        h = (jax.nn.silu(_dot(xb, w1_ref[:, cs])) * _dot(xb, w3_ref[:, cs])).astype(BF16)
        part = _dot(h, w2_ref[cs, :])
        acc = part if acc is None else acc + part
    o_ref[...] = _layer_norm(DN_ALPHA * x + acc, lg_ref[...], lb_ref[...])


def _ffn(x2, w1, w3, w2, lg, lb, tm):
    n = x2.shape[0]
    const = lambda i: (0, 0)
    full = lambda a: pl.BlockSpec(a.shape, const)
    return pl.pallas_call(
        _ffn_kernel,
        grid=(n // tm,),
        in_specs=[pl.BlockSpec((tm, D_MODEL), lambda i: (i, 0)),
                  full(w1), full(w3), full(w2), full(lg), full(lb)],
        out_specs=pl.BlockSpec((tm, D_MODEL), lambda i: (i, 0)),
        out_shape=jax.ShapeDtypeStruct((n, D_MODEL), F32),
        compiler_params=pltpu.CompilerParams(vmem_limit_bytes=VMEM_LIMIT),
        name="ffn_ln",
    )(x2, w1, w3, w2, lg, lb)


def _pad_heads(w):
    d = w.shape[0]
    w = w.reshape(d, ATT_HEADS, HEAD_DIM)
    return jnp.pad(w, ((0, 0), (0, 0), (0, HEAD_PAD - HEAD_DIM))).reshape(d, QK_PAD)


def _alibi_extras():
    slopes = 2.0 ** (-8.0 * jnp.arange(1, ATT_HEADS + 1, dtype=F32) / ATT_HEADS)
    row = jnp.arange(MOBA_BLOCK, dtype=F32)
    eq = jnp.zeros((MOBA_BLOCK, ATT_HEADS, HEAD_PAD), F32)
    eq = eq.at[:, :, Q_LANE_ONE].set(1.0).at[:, :, Q_LANE_ROW].set(row[:, None])
    ek = jnp.zeros((MOBA_BLOCK, ATT_HEADS, HEAD_PAD), F32)
    ek = ek.at[:, :, Q_LANE_ONE].set(row[:, None] * slopes[None, :])
    ek = ek.at[:, :, Q_LANE_BLK].set(-slopes[None, :] * MOBA_BLOCK).at[:, :, Q_LANE_ROW].set(-slopes[None, :])
    slope_tab = jnp.broadcast_to(slopes[:, None], (ATT_HEADS, MOBA_BLOCK))
    return eq.reshape(MOBA_BLOCK, QK_PAD), ek.reshape(MOBA_BLOCK, QK_PAD), slope_tab


def kernel(x, w_in, sgu_ln_g, sgu_ln_b, sgu_w, sgu_b, ssm_lambda_re, ssm_lambda_im, ssm_log_dt,
           ssm_b_re, ssm_b_im, ssm_c_re, ssm_c_im, ssm_d, glu_w, glu_b, w_branch_a, w_branch_b,
           w_branch_c, w_out, ln1_g, ln1_b, ffn_w1, ffn_w3, ffn_w2, ln2_g, ln2_b):
    bsz, seq, _ = x.shape
    n = bsz * seq
    nb = seq // MOBA_BLOCK
    cpb = seq // S5_CHUNK
    tm = 512 if n % 512 == 0 else MOBA_BLOCK
    eq, ek, slope_tab = _alibi_extras()
    scale = HEAD_DIM ** -0.5
    cap = (MOBA_TOPK + 1) * seq
    tpb = cap // MOBA_BLOCK
    trash = ATT_HEADS * cap
    o_q = 2 * SGU_WIDTH + SSM_WIDTH
    o_g = o_q + 3 * ATT_WIDTH

    xs = [x[b] for b in range(bsz)]
    for l in range(DEPTH):
        wl = w_in[l]
        wz = wl[:, :2 * SGU_WIDTH].astype(BF16)
        wst = wl[:, 2 * SGU_WIDTH:o_q].T.astype(BF16)
        wq = _pad_heads(wl[:, o_q:o_q + ATT_WIDTH] * scale).astype(BF16)
        wk = _pad_heads(wl[:, o_q + ATT_WIDTH:o_q + 2 * ATT_WIDTH]).astype(BF16)
        wvt = wl[:, o_q + 2 * ATT_WIDTH:o_g].T.astype(BF16)
        wg = wl[:, o_g:].astype(BF16)

        proj = [_inproj(xs[b], wz, wst, wq, wk, wvt, eq, ek, sgu_ln_g[l][None, :], sgu_ln_b[l][None, :],
                        sgu_w[l].astype(BF16), sgu_b[l].T, nb) for b in range(bsz)]

        s5_params = _s5_params(
            ssm_lambda_re[l], ssm_lambda_im[l], ssm_log_dt[l], ssm_b_re[l], ssm_b_im[l],
            ssm_c_re[l], ssm_c_im[l])
        dtab = jnp.broadcast_to(ssm_d[l].reshape(SSM_GROUPS, SSM_GROUP, 1),
                                (SSM_GROUPS, SSM_GROUP, S5_CHUNK))
        yss = _s5([p[1] for p in proj], s5_params, dtab, cpb)

        merge_w = (wg, w_branch_a[l].astype(BF16), w_branch_b[l].astype(BF16), w_branch_c[l].astype(BF16),
                   glu_w[l].astype(BF16), glu_b[l][None, :], w_out[l].astype(BF16),
                   ln1_g[l][None, :], ln1_b[l][None, :])
        ffn_w = (ffn_w1[l].astype(BF16), ffn_w3[l].astype(BF16), ffn_w2[l].astype(BF16),
                 ln2_g[l][None, :], ln2_b[l][None, :])
        for b in range(bsz):
            bra, _, qhm, ka, vt, sel, rank, cnt = proj[b]
            start, tmap, nused = _plan(cnt, nb, tpb)
            dst = _pos(sel[0], rank[0], start, nb, cap, trash).reshape(MOBA_TOPK, ATT_HEADS * seq)
            qs = _sc_scatter_rows(qhm.reshape(ATT_HEADS * seq, HEAD_PAD), dst, trash + MOBA_BLOCK)
            part = _routed(qs, ka, vt[0], slope_tab, tmap[:, 0, :], nused[:, 0, 0], 1, seq, tpb)
            g = _sc_gather_rows(part, dst.reshape(1, MOBA_TOPK * ATT_HEADS * seq))
            brc = _combine(qhm[0], ka, vt[0], sel[0],
                           g.reshape(MOBA_TOPK, ATT_HEADS, seq, HEAD_PAD), slope_tab, nb)
            x1 = _merge(xs[b], bra, yss[b], brc, *merge_w, tm)
            xs[b] = _ffn(x1, *ffn_w, tm)
    return jnp.stack(xs, axis=0)
```

```python
import functools

import jax
import jax.numpy as jnp
from jax import lax
from jax.experimental import pallas as pl
from jax.experimental.pallas import tpu as pltpu
from jax.experimental.pallas import tpu_sc as plsc

F32 = jnp.float32
BF16 = jnp.bfloat16

D_MODEL = 1024
SGU_CHUNK = 128
SGU_GROUPS = 4
SGU_WIDTH = 512
SSM_WIDTH = 512
SSM_GROUP = 16
SSM_GROUPS = 32
SSM_STATE = 64
ATT_HEADS = 8
HEAD_DIM = 64
ATT_WIDTH = 512
MOBA_BLOCK = 256
MOBA_TOPK = 3
D_FF = 2816
DEPTH = 2
DN_ALPHA = (2 * DEPTH) ** 0.25
LN_EPS = 1e-5
NEG_BIG = -1e30

HEAD_PAD = 128
QK_PAD = ATT_HEADS * HEAD_PAD
Q_LANE_ONE = HEAD_DIM
Q_LANE_BLK = HEAD_DIM + 1
Q_LANE_ROW = HEAD_DIM + 2
SC_WINDOW = 128
ROUTED_TILES = 8
S5_CHUNK = 128
S5_COLS = S5_CHUNK * SSM_GROUP
VMEM_LIMIT = 56 * 1024 * 1024

_HI = lax.Precision.HIGHEST


def _dot(a, b, precision=None):
    return jnp.dot(a, b, preferred_element_type=F32, precision=precision)


def _dot_nt(a, b, precision=None):
    return lax.dot_general(a, b, (((1,), (1,)), ((), ())),
                           preferred_element_type=F32, precision=precision)


def _layer_norm(x, g, b):
    mu = jnp.mean(x, axis=-1, keepdims=True)
    xc = x - mu
    var = jnp.mean(xc * xc, axis=-1, keepdims=True)
    return xc * lax.rsqrt(var + LN_EPS) * g + b


def _inproj_kernel(nb, x_ref, wz_ref, wst_ref, wq_ref, wk_ref, wvt_ref, eq_ref, ek_ref, lng_ref, lnb_ref,
                   sw_ref, sbt_ref, bra_ref, u_ref, q_ref, k_ref, vt_ref, sel_ref, rank_ref, cnt_ref,
                   kmean_ref, carry_ref):
    i = pl.program_id(0)
    il = i % nb

    @pl.when(i == 0)
    def _():
        kmean_ref[...] = jnp.zeros_like(kmean_ref)

    @pl.when(il == 0)
    def _():
        carry_ref[...] = jnp.zeros_like(carry_ref)

    xb = x_ref[...].astype(BF16)

    z = jax.nn.gelu(_dot(xb, wz_ref[...]))
    u = z[:, :SGU_WIDTH]
    vn = _layer_norm(z[:, SGU_WIDTH:], lng_ref[...], lnb_ref[...]).astype(BF16)
    r_io = lax.broadcasted_iota(jnp.int32, (SGU_CHUNK, SGU_CHUNK), 0)
    c_io = lax.broadcasted_iota(jnp.int32, (SGU_CHUNK, SGU_CHUNK), 1)
    tril = r_io >= c_io
    for g in range(SGU_GROUPS):
        w = jnp.where(tril, sw_ref[g], jnp.zeros((), BF16))
        bias = sbt_ref[:, g:g + 1]
        gs = slice(g * SGU_CHUNK, (g + 1) * SGU_CHUNK)
        for c in range(MOBA_BLOCK // SGU_CHUNK):
            rs = slice(c * SGU_CHUNK, (c + 1) * SGU_CHUNK)
            mixed = _dot(w, vn[rs, gs]) + bias
            bra_ref[rs, gs] = (u[rs, gs] * mixed).astype(BF16)

    ut = _dot_nt(wst_ref[...], xb)
    for c in range(MOBA_BLOCK // S5_CHUNK):
        u_ref[c] = ut[:, c * S5_CHUNK:(c + 1) * S5_CHUNK]

    lane = lax.broadcasted_iota(jnp.int32, (1, QK_PAD), 1) & (HEAD_PAD - 1)
    qa = _dot(xb, wq_ref[...]) + eq_ref[...] + jnp.where(lane == Q_LANE_BLK, il.astype(F32), 0.0)
    for h in range(ATT_HEADS):
        q_ref[0, h] = qa[:, h * HEAD_PAD:(h + 1) * HEAD_PAD]
    ka = _dot(xb, wk_ref[...])
    k_ref[...] = (ka + ek_ref[...]).astype(BF16)
    kmean_ref[pl.ds(il, 1), :] = jnp.mean(ka, axis=0, keepdims=True)
    vt_ref[0] = _dot_nt(wvt_ref[...], xb).astype(BF16)

    blk = lax.broadcasted_iota(jnp.int32, (nb, MOBA_BLOCK), 0)
    neg_inf = jnp.full((), -jnp.inf, F32)
    r_io = lax.broadcasted_iota(jnp.int32, (MOBA_BLOCK, MOBA_BLOCK), 0)
    c_io = lax.broadcasted_iota(jnp.int32, (MOBA_BLOCK, MOBA_BLOCK), 1)
    earlier = (r_io < c_io).astype(BF16)
    heads = range(ATT_HEADS)
    past = blk < il
    gates = []
    for h in heads:
        hs = slice(h * HEAD_PAD, h * HEAD_PAD + HEAD_DIM)
        gates.append(jnp.where(past, _dot_nt(kmean_ref[:, hs], qa[:, hs], precision=_HI), neg_inf))
    sels = [[] for _ in heads]
    for r in range(MOBA_TOPK):
        ms = [jnp.max(g, axis=0, keepdims=True) for g in gates]
        idxs = [jnp.min(jnp.where(gates[h] == ms[h], blk, nb), axis=0, keepdims=True) for h in heads]
        for h in heads:
            sels[h].append(jnp.where(r < il, idxs[h], -1))
        gates = [jnp.where(blk == idxs[h], neg_inf, gates[h]) for h in heads]
    hits = [[blk == s for s in sels[h]] for h in heads]
    onehots = [jnp.where(hits[h][0] | hits[h][1] | hits[h][2], 1.0, 0.0) for h in heads]
    befores = [carry_ref[h] + _dot(onehots[h].astype(BF16), earlier) for h in heads]
    for h in heads:
        carry_ref[h] = carry_ref[h] + jnp.sum(onehots[h], axis=1, keepdims=True)
        for r in range(MOBA_TOPK):
            sel_ref[0, r, h:h + 1, :] = sels[h][r]
            rank = jnp.sum(jnp.where(hits[h][r], befores[h], 0.0), axis=0, keepdims=True)
            rank_ref[0, r, h:h + 1, :] = rank.astype(jnp.int32)

    @pl.when(il == nb - 1)
    def _():
        cnt_ref[0] = carry_ref[...]


def _inproj(x2, wz, wst, wq, wk, wvt, eq, ek, lng, lnb, sw, sbt, nb, first_block):
    seq = nb * MOBA_BLOCK
    n, bsz, grid = seq, 1, nb
    const = lambda i: (0, 0)
    return pl.pallas_call(
        functools.partial(_inproj_kernel, nb),
        grid=(grid,),
        in_specs=[
            pl.BlockSpec((MOBA_BLOCK, D_MODEL), lambda i: (first_block + i, 0)),
            pl.BlockSpec(wz.shape, const),
            pl.BlockSpec(wst.shape, const),
            pl.BlockSpec(wq.shape, const),
            pl.BlockSpec(wk.shape, const),
            pl.BlockSpec(wvt.shape, const),
            pl.BlockSpec(eq.shape, const),
            pl.BlockSpec(ek.shape, const),
            pl.BlockSpec(lng.shape, const),
            pl.BlockSpec(lnb.shape, const),
            pl.BlockSpec(sw.shape, lambda i: (0, 0, 0)),
            pl.BlockSpec(sbt.shape, const),
        ],
        out_specs=[
            pl.BlockSpec((MOBA_BLOCK, SGU_WIDTH), lambda i: (i, 0)),
            pl.BlockSpec((MOBA_BLOCK // S5_CHUNK, SSM_WIDTH, S5_CHUNK), lambda i: (i, 0, 0)),
            pl.BlockSpec((1, ATT_HEADS, MOBA_BLOCK, HEAD_PAD), lambda i: (i // nb, 0, i % nb, 0)),
            pl.BlockSpec((MOBA_BLOCK, QK_PAD), lambda i: (i, 0)),
            pl.BlockSpec((1, ATT_WIDTH, MOBA_BLOCK), lambda i: (i // nb, 0, i % nb)),
            pl.BlockSpec((1, MOBA_TOPK, ATT_HEADS, MOBA_BLOCK), lambda i: (i // nb, 0, 0, i % nb)),
            pl.BlockSpec((1, MOBA_TOPK, ATT_HEADS, MOBA_BLOCK), lambda i: (i // nb, 0, 0, i % nb)),
            pl.BlockSpec((1, ATT_HEADS, nb, MOBA_BLOCK), lambda i: (i // nb, 0, 0, 0)),
        ],
        out_shape=[
            jax.ShapeDtypeStruct((n, SGU_WIDTH), BF16),
            jax.ShapeDtypeStruct((n // S5_CHUNK, SSM_WIDTH, S5_CHUNK), F32),
            jax.ShapeDtypeStruct((bsz, ATT_HEADS, seq, HEAD_PAD), F32),
            jax.ShapeDtypeStruct((n, QK_PAD), BF16),
            jax.ShapeDtypeStruct((bsz, ATT_WIDTH, seq), BF16),
            jax.ShapeDtypeStruct((bsz, MOBA_TOPK, ATT_HEADS, seq), jnp.int32),
            jax.ShapeDtypeStruct((bsz, MOBA_TOPK, ATT_HEADS, seq), jnp.int32),
            jax.ShapeDtypeStruct((bsz, ATT_HEADS, nb, MOBA_BLOCK), F32),
        ],
        scratch_shapes=[pltpu.VMEM((nb, QK_PAD), F32),
                        pltpu.VMEM((ATT_HEADS, nb, MOBA_BLOCK), F32)],
        compiler_params=pltpu.CompilerParams(
            dimension_semantics=("arbitrary",), vmem_limit_bytes=VMEM_LIMIT),
        name="inproj_sgu_gate",
    )(x2, wz, wst, wq, wk, wvt, eq, ek, lng, lnb, sw, sbt)


def _s5_param_kernel(lre_r, lim_r, lre_c, lim_c, ldt, bt_re, bt_im, ct_re, ct_im, cr_r, ci_r,
                     t_ref, w_ref, m_ref, a_ref):
    two_p = 2 * SSM_STATE
    dt = jnp.exp(ldt[0])

    def powers(ar, ai, e):
        mag = jnp.exp(ar * e)
        return mag * jnp.cos(ai * e), mag * jnp.sin(ai * e)

    lr, li = lre_r[0], lim_r[0]
    ar, ai = lr * dt, li * dt
    lbr, lbi = powers(ar, ai, 1.0)
    den = lr * lr + li * li
    cfr = ((lbr - 1.0) * lr + lbi * li) / den
    cfi = (lbi * lr - (lbr - 1.0) * li) / den
    bbr = cfr * bt_re[0] - cfi * bt_im[0]
    bbi = cfr * bt_im[0] + cfi * bt_re[0]

    chunk = S5_CHUNK
    s_col = lax.broadcasted_iota(jnp.int32, (chunk, two_p), 0).astype(F32)
    first = lax.broadcasted_iota(jnp.int32, (chunk, two_p), 1) < SSM_STATE
    rev_r, rev_i = powers(ar, ai, (chunk - 1.0) - s_col)

    for hq in range(SSM_GROUP):
        br, bi = bbr[hq:hq + 1, :], bbi[hq:hq + 1, :]
        w = jnp.where(first, rev_r * br - rev_i * bi, rev_r * bi + rev_i * br)
        w_ref[0, hq * chunk:(hq + 1) * chunk, :] = w.astype(BF16)

    lrc, lic = lre_c[0], lim_c[0]
    arc, aic = lrc * dt, lic * dt
    t_row = lax.broadcasted_iota(jnp.int32, (two_p, chunk), 1).astype(F32)
    top = lax.broadcasted_iota(jnp.int32, (two_p, chunk), 0) < SSM_STATE
    pw_r, pw_i = powers(arc, aic, t_row)
    er, ei = powers(arc, aic, t_row + 1.0)

    for h in range(SSM_GROUP):
        cr, ci = ct_re[0][:, h:h + 1], ct_im[0][:, h:h + 1]
        m = jnp.where(top, er * cr - ei * ci, -(er * ci + ei * cr))
        m_ref[0, :, h * chunk:(h + 1) * chunk] = m.astype(BF16)

    first16 = lax.broadcasted_iota(jnp.int32, (SSM_GROUP, two_p), 1) < SSM_STATE
    crr, cir = cr_r[0], ci_r[0]
    g2 = jnp.concatenate(
        [jnp.where(first16, crr * bbr[hq:hq + 1, :] - cir * bbi[hq:hq + 1, :],
                   -(crr * bbi[hq:hq + 1, :] + cir * bbr[hq:hq + 1, :])) for hq in range(SSM_GROUP)],
        axis=0)
    taps = _dot(g2, jnp.where(top, pw_r, pw_i), _HI)

    causal = (lax.broadcasted_iota(jnp.int32, (chunk, chunk), 0)
              <= lax.broadcasted_iota(jnp.int32, (chunk, chunk), 1))
    for hq in range(SSM_GROUP):
        for h in range(SSM_GROUP):
            row = hq * SSM_GROUP + h
            k_rows = jnp.broadcast_to(taps[row:row + 1, :], (chunk, chunk))
            toep = pltpu.roll(k_rows, 0, axis=1, stride=1, stride_axis=0)
            t_ref[0, hq * chunk:(hq + 1) * chunk, h * chunk:(h + 1) * chunk] = (
                jnp.where(causal, toep, 0.0).astype(BF16))

    pr, pi_ = lbr, lbi
    for _ in range(S5_CHUNK.bit_length() - 1):
        pr, pi_ = pr * pr - pi_ * pi_, 2.0 * pr * pi_
    a_ref[0, 0:1, :] = pr
    a_ref[0, 1:2, :] = pi_


def _s5_params(lre, lim, ldt, b_re, b_im, c_re, c_im):
    g, p = lre.shape
    dup = lambda a: jnp.concatenate([a, a], axis=-1)
    lre_r, lim_r = dup(lre)[:, None, :], dup(lim)[:, None, :]
    lre_c, lim_c = dup(lre)[:, :, None], dup(lim)[:, :, None]
    bt_re = dup(jnp.swapaxes(b_re, 1, 2))
    bt_im = dup(jnp.swapaxes(b_im, 1, 2))
    ct = lambda c: jnp.concatenate([jnp.swapaxes(c, 1, 2)] * 2, axis=1)
    ct_re, ct_im = ct(c_re), ct(c_im)
    ldt3 = ldt[:, None, None]
    return (lre_r, lim_r, lre_c, lim_c, ldt3, bt_re, bt_im, ct_re, ct_im, dup(c_re), dup(c_im))


N_S5_PARAMS = 11


def _s5_kernel(cpb, nseq, *refs):
    params = refs[:N_S5_PARAMS]
    u_refs = refs[N_S5_PARAMS:N_S5_PARAMS + nseq]
    d_ref = refs[N_S5_PARAMS + nseq]
    y_refs = refs[N_S5_PARAMS + nseq + 1:N_S5_PARAMS + 2 * nseq + 1]
    t_scr, w_scr, m_scr, a_scr = refs[N_S5_PARAMS + 2 * nseq + 1:]
    _s5_param_kernel(*params, t_scr, w_scr, m_scr, a_scr)
    _s5_scan_kernel(cpb, u_refs, t_scr, w_scr, m_scr, a_scr, d_ref, y_refs)


def _s5_scan_kernel(cpb, u_refs, t_ref, w_ref, m_ref, a_ref, d_ref, y_refs):
    us = [jnp.concatenate([u_ref[:, hq, :] for u_ref in u_refs], axis=0)
          for hq in range(SSM_GROUP)]
    ub = jnp.concatenate([u.astype(BF16) for u in us], axis=1)
    rows = ub.shape[0]
    two_p = 2 * SSM_STATE
    h = _dot(ub, w_ref[0])
    ar = a_ref[0, 0:1, :]
    ai = a_ref[0, 1:2, :]
    lane = lax.broadcasted_iota(jnp.int32, (1, two_p), 1)
    sign = jnp.where(lane < SSM_STATE, -1.0, 1.0)
    chunk = lax.broadcasted_iota(jnp.int32, (rows, two_p), 0) % cpb
    step = 1
    while step < cpb:
        prev = jnp.where(chunk >= step, pltpu.roll(h, step, axis=0), 0.0)
        swapped = pltpu.roll(prev, SSM_STATE, axis=1)
        h = h + prev * ar + swapped * (ai * sign)
        ar, ai = ar * ar - ai * ai, 2.0 * ar * ai
        step *= 2
    hprev = jnp.where(chunk >= 1, pltpu.roll(h, 1, axis=0), 0.0)
    y = _dot(ub, t_ref[0]) + _dot(hprev.astype(BF16), m_ref[0])
    for hq in range(SSM_GROUP):
        yh = jax.nn.gelu(y[:, hq * S5_CHUNK:(hq + 1) * S5_CHUNK] + d_ref[0, hq:hq + 1, :] * us[hq])
        for b, y_ref in enumerate(y_refs):
            y_ref[:, hq, :] = yh[b * cpb:(b + 1) * cpb]


def _s5(utcs, params, dtab, cpb):
    nseq = len(utcs)
    two_p = 2 * SSM_STATE
    idx = lambda i: (i, 0, 0)
    chan = pl.BlockSpec((cpb, SSM_GROUP, S5_CHUNK), lambda i: (0, i, 0))
    return pl.pallas_call(
        functools.partial(_s5_kernel, cpb, nseq),
        grid=(SSM_GROUPS,),
        in_specs=([pl.BlockSpec((1,) + a.shape[1:], idx) for a in params]
                  + [chan] * nseq + [pl.BlockSpec((1, SSM_GROUP, S5_CHUNK), idx)]),
        out_specs=[chan] * nseq,
        out_shape=[jax.ShapeDtypeStruct(u.shape, F32) for u in utcs],
        scratch_shapes=[pltpu.VMEM((1, S5_COLS, S5_COLS), BF16),
                        pltpu.VMEM((1, S5_COLS, two_p), BF16),
                        pltpu.VMEM((1, two_p, S5_COLS), BF16),
                        pltpu.VMEM((1, 2, two_p), F32)],
        compiler_params=pltpu.CompilerParams(vmem_limit_bytes=VMEM_LIMIT),
        name="s5_scan",
    )(*params, *utcs, dtab)


def _plan_kernel(nb, tpb, cnt_ref, start_ref, tmap_ref, nused_ref):
    cnt = cnt_ref[0, 0]
    padded = jnp.floor((cnt + (MOBA_BLOCK - 1.0)) * (1.0 / MOBA_BLOCK)) * MOBA_BLOCK
    r_io = lax.broadcasted_iota(jnp.int32, (nb, nb), 0)
    c_io = lax.broadcasted_iota(jnp.int32, (nb, nb), 1)
    start = _dot((c_io < r_io).astype(F32), padded, _HI)
    start_ref[0, 0] = start
    end = (start + padded)[:, 0:1]
    tile_row = lax.broadcasted_iota(jnp.int32, (nb, tpb), 1).astype(F32) * MOBA_BLOCK
    blk_of_tile = jnp.sum(jnp.where(end <= tile_row, 1.0, 0.0), axis=0, keepdims=True)
    tmap_ref[0] = jnp.minimum(blk_of_tile, nb - 1.0).astype(jnp.int32)
    total = jnp.max(end, axis=0, keepdims=True)
    nused_ref[0] = jnp.broadcast_to(total * (1.0 / MOBA_BLOCK), (1, HEAD_PAD)).astype(jnp.int32)


def _plan(cnt, nb, tpb):
    bsz = cnt.shape[0]
    nbh = bsz * ATT_HEADS
    return pl.pallas_call(
        functools.partial(_plan_kernel, nb, tpb),
        grid=(bsz, ATT_HEADS),
        in_specs=[pl.BlockSpec((1, 1, nb, MOBA_BLOCK), lambda b, h: (b, h, 0, 0))],
        out_specs=[
            pl.BlockSpec((1, 1, nb, MOBA_BLOCK), lambda b, h: (b, h, 0, 0)),
            pl.BlockSpec((1, 1, tpb), lambda b, h: (b * ATT_HEADS + h, 0, 0)),
            pl.BlockSpec((1, 1, HEAD_PAD), lambda b, h: (b * ATT_HEADS + h, 0, 0)),
        ],
        out_shape=[
            jax.ShapeDtypeStruct((bsz, ATT_HEADS, nb, MOBA_BLOCK), F32),
            jax.ShapeDtypeStruct((nbh, 1, tpb), jnp.int32),
            jax.ShapeDtypeStruct((nbh, 1, HEAD_PAD), jnp.int32),
        ],
        name="route_plan",
    )(cnt)


def _pos_kernel(nb, cap, trash, sel_ref, rank_ref, start_ref, dst_ref):
    b = pl.program_id(0) // nb
    blk = lax.broadcasted_iota(jnp.int32, (nb, MOBA_BLOCK), 0)
    lane = lax.broadcasted_iota(jnp.int32, (1, MOBA_BLOCK), 1)
    for h in range(ATT_HEADS):
        start = start_ref[0, h]
        base = (b * ATT_HEADS + h) * cap
        for r in range(MOBA_TOPK):
            s = sel_ref[r, h:h + 1, :]
            first = jnp.sum(jnp.where(blk == s, start, 0.0), axis=0, keepdims=True).astype(jnp.int32)
            dst = base + first + rank_ref[r, h:h + 1, :]
            dst_ref[r, h:h + 1, :] = jnp.where(s >= 0, dst, trash + lane)


def _pos(sel, rank, start, nb, cap, trash):
    n = sel.shape[-1]
    blk3 = pl.BlockSpec((MOBA_TOPK, ATT_HEADS, MOBA_BLOCK), lambda i: (0, 0, i))
    return pl.pallas_call(
        functools.partial(_pos_kernel, nb, cap, trash),
        grid=(n // MOBA_BLOCK,),
        in_specs=[blk3, blk3,
                  pl.BlockSpec((1, ATT_HEADS, nb, MOBA_BLOCK), lambda i: (i // nb, 0, 0, 0))],
        out_specs=blk3,
        out_shape=jax.ShapeDtypeStruct(sel.shape, jnp.int32),
        name="route_pos",
    )(sel, rank, start)


def _sc_mesh():
    return plsc.VectorSubcoreMesh(core_axis_name="core", subcore_axis_name="subcore")


def _sc_scatter_rows(x, idx, rows_out):
    nrep, nin = idx.shape

    @pl.kernel(out_type=jax.ShapeDtypeStruct((rows_out, HEAD_PAD), x.dtype), mesh=_sc_mesh(),
               scratch_types=[])
    def scatter(x_hbm, i_hbm, o_hbm):
        def body(x_vmem, *i_vmems):
            for i_vmem in i_vmems:
                pltpu.sync_copy(x_vmem, o_hbm.at[i_vmem.at[0]])

        idx_spec = lambda r: pl.BlockSpec((1, SC_WINDOW), lambda i: (r, i))
        pltpu.emit_pipeline(
            body,
            grid=(nin // SC_WINDOW,),
            in_specs=[pl.BlockSpec((SC_WINDOW, HEAD_PAD), lambda i: (i, 0))]
                     + [idx_spec(r) for r in range(nrep)],
            out_specs=[],
            core_axis_name=("core", "subcore"),
            dimension_semantics=(pltpu.PARALLEL,),
        )(x_hbm, *([i_hbm] * nrep))

    return scatter(x, idx)


def _sc_gather_rows(x, idx):
    nout = idx.shape[1]

    @pl.kernel(out_type=jax.ShapeDtypeStruct((nout, HEAD_PAD), x.dtype), mesh=_sc_mesh())
    def gather(x_hbm, i_hbm, o_hbm):
        def body(i_vmem, o_vmem):
            pltpu.sync_copy(x_hbm.at[i_vmem.at[0]], o_vmem)

        pltpu.emit_pipeline(
            body,
            grid=(nout // SC_WINDOW,),
            in_specs=[pl.BlockSpec((1, SC_WINDOW), lambda i: (0, i))],
            out_specs=[pl.BlockSpec((SC_WINDOW, HEAD_PAD), lambda i: (i, 0))],
            core_axis_name=("core", "subcore"),
            dimension_semantics=(pltpu.PARALLEL,),
        )(i_hbm, o_hbm)

    return gather(x, idx)


def _routed_kernel(tmap_ref, nused_ref, qs_ref, slope_ref, k_ref, vt_ref, o_ref):
    h = pl.program_id(1)
    bh = pl.program_id(0) * ATT_HEADS + h
    s = pl.program_id(2)

    @pl.when(s * ROUTED_TILES < nused_ref[bh])
    def _():
        slope = slope_ref[pl.ds(h, 1), :]
        row = lax.broadcasted_iota(jnp.int32, (HEAD_DIM, MOBA_BLOCK), 0)
        tiles = range(ROUTED_TILES)
        js = [tmap_ref[bh, s * ROUTED_TILES + u] for u in tiles]
        ks = [pl.multiple_of(j * MOBA_BLOCK, MOBA_BLOCK) for j in js]
        zs = [_dot_nt(k_ref[pl.ds(ks[u], MOBA_BLOCK), :],
                      qs_ref[u * MOBA_BLOCK:(u + 1) * MOBA_BLOCK, :].astype(BF16)) for u in tiles]
        ms = [jnp.max(z, axis=0, keepdims=True) for z in zs]
        ps = [jnp.exp(zs[u] - ms[u]) for u in tiles]
        ls = [jnp.sum(p, axis=0, keepdims=True) for p in ps]
        ots = [_dot(vt_ref[:, pl.ds(ks[u], MOBA_BLOCK)], ps[u].astype(BF16)) for u in tiles]
        for u in tiles:
            m = ms[u] + slope * jnp.full((1, MOBA_BLOCK), ks[u], jnp.int32).astype(F32)
            stats = jnp.where(row == 0, m, jnp.where(row == 1, ls[u], 0.0))
            o_ref[u * MOBA_BLOCK:(u + 1) * MOBA_BLOCK, :] = jnp.concatenate([ots[u], stats], axis=0).T


def _routed(qs, ka, vt, slope_tab, tmap, nused, bsz, seq, tpb):
    steps = tpb // ROUTED_TILES
    rows = ROUTED_TILES * MOBA_BLOCK
    shift = ROUTED_TILES.bit_length() - 1

    def step_of(b, h, s, tm, nu):
        bh = b * ATT_HEADS + h
        used = lax.shift_right_logical(nu[bh] + (ROUTED_TILES - 1), shift)
        return bh * steps + jnp.minimum(s, jnp.maximum(used - 1, 0)), 0

    grid_spec = pltpu.PrefetchScalarGridSpec(
        num_scalar_prefetch=2,
        grid=(bsz, ATT_HEADS, steps),
        in_specs=[
            pl.BlockSpec((rows, HEAD_PAD), step_of),
            pl.BlockSpec(slope_tab.shape, lambda b, h, s, tm, nu: (0, 0)),
            pl.BlockSpec((seq, HEAD_PAD), lambda b, h, s, tm, nu: (b, h)),
            pl.BlockSpec((HEAD_DIM, seq), lambda b, h, s, tm, nu: (h, b)),
        ],
        out_specs=pl.BlockSpec((rows, HEAD_PAD), step_of),
    )
    return pl.pallas_call(
        _routed_kernel,
        grid_spec=grid_spec,
        out_shape=jax.ShapeDtypeStruct(qs.shape, F32),
        compiler_params=pltpu.CompilerParams(vmem_limit_bytes=VMEM_LIMIT),
        name="moba_routed",
    )(tmap, nused, qs, slope_tab, ka, vt)


def _combine_kernel(nb, q_ref, k_ref, vt_ref, sel_ref, g_ref, slope_ref, o_ref):
    il = pl.program_id(0) % nb
    kio = lax.broadcasted_iota(jnp.int32, (MOBA_BLOCK, MOBA_BLOCK), 0)
    qio = lax.broadcasted_iota(jnp.int32, (MOBA_BLOCK, MOBA_BLOCK), 1)
    causal = kio <= qio
    own_shift = jnp.full((1, MOBA_BLOCK), il * MOBA_BLOCK, jnp.int32).astype(F32)
    outs = []
    heads = range(ATT_HEADS)
    zs = [_dot_nt(k_ref[:, h * HEAD_PAD:(h + 1) * HEAD_PAD], q_ref[h].astype(BF16)) for h in heads]
    zs = [jnp.where(causal, z, NEG_BIG) for z in zs]
    ms = [jnp.max(z, axis=0, keepdims=True) for z in zs]
    ps = [jnp.exp(zs[h] - ms[h]) for h in heads]
    ls = [jnp.sum(p, axis=0, keepdims=True) for p in ps]
    os_ = [_dot(vt_ref[h * HEAD_DIM:(h + 1) * HEAD_DIM, :], ps[h].astype(BF16)) for h in heads]
    for h in heads:
        l0, o0 = ls[h], os_[h]
        m0 = ms[h] + slope_ref[h:h + 1, :] * own_shift
        parts = []
        for r in range(MOBA_TOPK):
            gt = g_ref[r, h].T
            valid = sel_ref[r, h:h + 1, :] >= 0
            parts.append((jnp.where(valid, gt[HEAD_DIM:HEAD_DIM + 1, :], NEG_BIG),
                          jnp.where(valid, gt[HEAD_DIM + 1:HEAD_DIM + 2, :], 0.0),
                          jnp.where(valid, gt[:HEAD_DIM, :], 0.0)))
        m = m0
        for mr, _, _ in parts:
            m = jnp.maximum(m, mr)
        w = jnp.exp(m0 - m)
        num, den = w * o0, w * l0
        for mr, lr, orr in parts:
            w = jnp.exp(mr - m)
            num, den = num + w * orr, den + w * lr
        outs.append(num / den)
    o_ref[...] = jnp.concatenate(outs, axis=0).T.astype(BF16)


def _combine(qhm, ka, vt, sel, g4, slope_tab, nb):
    n = ka.shape[0]
    return pl.pallas_call(
        functools.partial(_combine_kernel, nb),
        grid=(n // MOBA_BLOCK,),
        in_specs=[
            pl.BlockSpec((ATT_HEADS, MOBA_BLOCK, HEAD_PAD), lambda i: (0, i, 0)),
            pl.BlockSpec((MOBA_BLOCK, QK_PAD), lambda i: (i, 0)),
            pl.BlockSpec((ATT_WIDTH, MOBA_BLOCK), lambda i: (0, i)),
            pl.BlockSpec((MOBA_TOPK, ATT_HEADS, MOBA_BLOCK), lambda i: (0, 0, i)),
            pl.BlockSpec((MOBA_TOPK, ATT_HEADS, MOBA_BLOCK, HEAD_PAD), lambda i: (0, 0, i, 0)),
            pl.BlockSpec(slope_tab.shape, lambda i: (0, 0)),
        ],
        out_specs=pl.BlockSpec((MOBA_BLOCK, ATT_WIDTH), lambda i: (i, 0)),
        out_shape=jax.ShapeDtypeStruct((n, ATT_WIDTH), BF16),
        compiler_params=pltpu.CompilerParams(vmem_limit_bytes=VMEM_LIMIT),
        name="moba_own_combine",
    )(qhm, ka, vt, sel, g4, slope_tab)


def _merge_kernel(x_ref, bra_ref, ys_ref, brc_ref, wg_ref, wa_ref, wb_ref, wc_ref, gw_ref, gb_ref,
                  wo_ref, lg_ref, lb_ref, o_ref):
    x = x_ref[...]
    xb = x.astype(BF16)

    def gate(k):
        return jax.nn.sigmoid(_dot(xb, wg_ref[:, k * D_MODEL:(k + 1) * D_MODEL]))

    merged = gate(0) * _dot(bra_ref[...], wa_ref[...])
    ys = jnp.concatenate([ys_ref[c].T for c in range(ys_ref.shape[0])], axis=0)
    brb = ys * jax.nn.sigmoid(_dot(ys.astype(BF16), gw_ref[...]) + gb_ref[...])
    merged = merged + gate(1) * _dot(brb.astype(BF16), wb_ref[...])
    merged = merged + gate(2) * _dot(brc_ref[...], wc_ref[...])
    mix = _dot(merged.astype(BF16), wo_ref[...])
    o_ref[...] = _layer_norm(DN_ALPHA * x + mix, lg_ref[...], lb_ref[...])


def _merge(x2, bra, ys, brc, wg, wa, wb, wc, gw, gb, wo, lg, lb, tm, first_tile):
    n = bra.shape[0]
    const = lambda i: (0, 0)
    tile = lambda w: pl.BlockSpec((tm, w), lambda i: (i, 0))
    full = lambda a: pl.BlockSpec(a.shape, const)
    return pl.pallas_call(
        _merge_kernel,
        grid=(n // tm,),
        in_specs=[pl.BlockSpec((tm, D_MODEL), lambda i: (first_tile + i, 0)), tile(SGU_WIDTH),
                  pl.BlockSpec((tm // S5_CHUNK, SSM_WIDTH, S5_CHUNK), lambda i: (i, 0, 0)),
                  tile(ATT_WIDTH),
                  full(wg), full(wa), full(wb), full(wc), full(gw), full(gb), full(wo), full(lg), full(lb)],
        out_specs=tile(D_MODEL),
        out_shape=jax.ShapeDtypeStruct((n, D_MODEL), F32),
        compiler_params=pltpu.CompilerParams(vmem_limit_bytes=VMEM_LIMIT),
        name="merge_ln",
    )(x2, bra, ys, brc, wg, wa, wb, wc, gw, gb, wo, lg, lb)


FF_CHUNK = D_FF // 2


def _ffn_kernel(x_ref, w1_ref, w3_ref, w2_ref, lg_ref, lb_ref, *rest):
    o_ref = rest[-1]
    x = x_ref[...]
    xb = x.astype(BF16)
    acc = None
    for c in range(D_FF // FF_CHUNK):
        cs = slice(c * FF_CHUNK, (c + 1) * FF_CHUNK)
        h = (jax.nn.silu(_dot(xb, w1_ref[:, cs])) * _dot(xb, w3_ref[:, cs])).astype(BF16)
        part = _dot(h, w2_ref[cs, :])
        acc = part if acc is None else acc + part
    o_ref[...] = _layer_norm(DN_ALPHA * x + acc, lg_ref[...], lb_ref[...])


def _ffn(x2, w1, w3, w2, lg, lb, tm, out_rows, first_tile, dest):
    n = x2.shape[0]
    const = lambda i: (0, 0)
    full = lambda a: pl.BlockSpec(a.shape, const)
    in_specs = [pl.BlockSpec((tm, D_MODEL), lambda i: (i, 0)),
                full(w1), full(w3), full(w2), full(lg), full(lb)]
    args = [x2, w1, w3, w2, lg, lb]
    aliases = {}
    if dest is not None:
        in_specs.append(pl.BlockSpec(memory_space=pl.ANY))
        aliases = {len(args): 0}
        args.append(dest)
    return pl.pallas_call(
        _ffn_kernel,
        grid=(n // tm,),
        in_specs=in_specs,
        out_specs=pl.BlockSpec((tm, D_MODEL), lambda i: (first_tile + i, 0)),
        out_shape=jax.ShapeDtypeStruct((out_rows, D_MODEL), F32),
        input_output_aliases=aliases,
        compiler_params=pltpu.CompilerParams(vmem_limit_bytes=VMEM_LIMIT),
        name="ffn_ln",
    )(*args)


def _pad_heads(w):
    d = w.shape[0]
    w = w.reshape(d, ATT_HEADS, HEAD_DIM)
    return jnp.pad(w, ((0, 0), (0, 0), (0, HEAD_PAD - HEAD_DIM))).reshape(d, QK_PAD)


def _alibi_extras():
    slopes = 2.0 ** (-8.0 * jnp.arange(1, ATT_HEADS + 1, dtype=F32) / ATT_HEADS)
    row = jnp.arange(MOBA_BLOCK, dtype=F32)
    eq = jnp.zeros((MOBA_BLOCK, ATT_HEADS, HEAD_PAD), F32)
    eq = eq.at[:, :, Q_LANE_ONE].set(1.0).at[:, :, Q_LANE_ROW].set(row[:, None])
    ek = jnp.zeros((MOBA_BLOCK, ATT_HEADS, HEAD_PAD), F32)
    ek = ek.at[:, :, Q_LANE_ONE].set(row[:, None] * slopes[None, :])
    ek = ek.at[:, :, Q_LANE_BLK].set(-slopes[None, :] * MOBA_BLOCK).at[:, :, Q_LANE_ROW].set(-slopes[None, :])
    slope_tab = jnp.broadcast_to(slopes[:, None], (ATT_HEADS, MOBA_BLOCK))
    return eq.reshape(MOBA_BLOCK, QK_PAD), ek.reshape(MOBA_BLOCK, QK_PAD), slope_tab


def kernel(x, w_in, sgu_ln_g, sgu_ln_b, sgu_w, sgu_b, ssm_lambda_re, ssm_lambda_im, ssm_log_dt,
           ssm_b_re, ssm_b_im, ssm_c_re, ssm_c_im, ssm_d, glu_w, glu_b, w_branch_a, w_branch_b,
           w_branch_c, w_out, ln1_g, ln1_b, ffn_w1, ffn_w3, ffn_w2, ln2_g, ln2_b):
    bsz, seq, _ = x.shape
    n = bsz * seq
    nb = seq // MOBA_BLOCK
    cpb = seq // S5_CHUNK
    tm = 512 if n % 512 == 0 else MOBA_BLOCK
    eq, ek, slope_tab = _alibi_extras()
    scale = HEAD_DIM ** -0.5
    cap = (MOBA_TOPK + 1) * seq
    tpb = cap // MOBA_BLOCK
    trash = ATT_HEADS * cap
    o_q = 2 * SGU_WIDTH + SSM_WIDTH
    o_g = o_q + 3 * ATT_WIDTH

    xs = [(x.reshape(n, D_MODEL), b * nb) for b in range(bsz)]
    out = None
    for l in range(DEPTH):
        last = l == DEPTH - 1
        wl = w_in[l]
        wz = wl[:, :2 * SGU_WIDTH].astype(BF16)
        wst = wl[:, 2 * SGU_WIDTH:o_q].T.astype(BF16)
        wq = _pad_heads(wl[:, o_q:o_q + ATT_WIDTH] * scale).astype(BF16)
        wk = _pad_heads(wl[:, o_q + ATT_WIDTH:o_q + 2 * ATT_WIDTH]).astype(BF16)
        wvt = wl[:, o_q + 2 * ATT_WIDTH:o_g].T.astype(BF16)
        wg = wl[:, o_g:].astype(BF16)

        proj = [_inproj(xs[b][0], wz, wst, wq, wk, wvt, eq, ek, sgu_ln_g[l][None, :],
                        sgu_ln_b[l][None, :], sgu_w[l].astype(BF16), sgu_b[l].T, nb, xs[b][1])
                for b in range(bsz)]

        s5_params = _s5_params(
            ssm_lambda_re[l], ssm_lambda_im[l], ssm_log_dt[l], ssm_b_re[l], ssm_b_im[l],
            ssm_c_re[l], ssm_c_im[l])
        dtab = jnp.broadcast_to(ssm_d[l].reshape(SSM_GROUPS, SSM_GROUP, 1),
                                (SSM_GROUPS, SSM_GROUP, S5_CHUNK))
        yss = _s5([p[1] for p in proj], s5_params, dtab, cpb)

        merge_w = (wg, w_branch_a[l].astype(BF16), w_branch_b[l].astype(BF16), w_branch_c[l].astype(BF16),
                   glu_w[l].astype(BF16), glu_b[l][None, :], w_out[l].astype(BF16),
                   ln1_g[l][None, :], ln1_b[l][None, :])
        ffn_w = (ffn_w1[l].astype(BF16), ffn_w3[l].astype(BF16), ffn_w2[l].astype(BF16),
                 ln2_g[l][None, :], ln2_b[l][None, :])
        for b in range(bsz):
            bra, _, qhm, ka, vt, sel, rank, cnt = proj[b]
            start, tmap, nused = _plan(cnt, nb, tpb)
            dst = _pos(sel[0], rank[0], start, nb, cap, trash).reshape(MOBA_TOPK, ATT_HEADS * seq)
            qs = _sc_scatter_rows(qhm.reshape(ATT_HEADS * seq, HEAD_PAD), dst, trash + MOBA_BLOCK)
            part = _routed(qs, ka, vt[0], slope_tab, tmap[:, 0, :], nused[:, 0, 0], 1, seq, tpb)
            g = _sc_gather_rows(part, dst.reshape(1, MOBA_TOPK * ATT_HEADS * seq))
            brc = _combine(qhm[0], ka, vt[0], sel[0],
                           g.reshape(MOBA_TOPK, ATT_HEADS, seq, HEAD_PAD), slope_tab, nb)
            x1 = _merge(xs[b][0], bra, yss[b], brc, *merge_w, tm, xs[b][1] * MOBA_BLOCK // tm)
            if last:
                out = _ffn(x1, *ffn_w, tm, n, b * (seq // tm), out)
            else:
                xs[b] = (_ffn(x1, *ffn_w, tm, seq, 0, None), 0)
    return out.reshape(bsz, seq, D_MODEL)
```

```python
import functools

import jax
import jax.numpy as jnp
from jax import lax
from jax.experimental import pallas as pl
from jax.experimental.pallas import tpu as pltpu
from jax.experimental.pallas import tpu_sc as plsc

F32 = jnp.float32
BF16 = jnp.bfloat16

D_MODEL = 1024
SGU_CHUNK = 128
SGU_GROUPS = 4
SGU_WIDTH = 512
SSM_WIDTH = 512
SSM_GROUP = 16
SSM_GROUPS = 32
SSM_STATE = 64
ATT_HEADS = 8
HEAD_DIM = 64
ATT_WIDTH = 512
MOBA_BLOCK = 256
MOBA_TOPK = 3
D_FF = 2816
DEPTH = 2
DN_ALPHA = (2 * DEPTH) ** 0.25
LN_EPS = 1e-5
NEG_BIG = -1e30

HEAD_PAD = 128
QK_PAD = ATT_HEADS * HEAD_PAD
Q_LANE_ONE = HEAD_DIM
Q_LANE_BLK = HEAD_DIM + 1
Q_LANE_ROW = HEAD_DIM + 2
SC_WINDOW = 128
ROUTED_TILES = 8
S5_CHUNK = 128
S5_COLS = S5_CHUNK * SSM_GROUP
VMEM_LIMIT = 56 * 1024 * 1024

_HI = lax.Precision.HIGHEST


def _dot(a, b, precision=None):
    return jnp.dot(a, b, preferred_element_type=F32, precision=precision)


def _dot_nt(a, b, precision=None):
    return lax.dot_general(a, b, (((1,), (1,)), ((), ())),
                           preferred_element_type=F32, precision=precision)


def _layer_norm(x, g, b):
    mu = jnp.mean(x, axis=-1, keepdims=True)
    xc = x - mu
    var = jnp.mean(xc * xc, axis=-1, keepdims=True)
    return xc * lax.rsqrt(var + LN_EPS) * g + b


def _inproj_kernel(nb, x_ref, wz_ref, wst_ref, wq_ref, wk_ref, wvt_ref, eq_ref, ek_ref, lng_ref, lnb_ref,
                   sw_ref, sbt_ref, bra_ref, u_ref, q_ref, k_ref, vt_ref, sel_ref, rank_ref, cnt_ref,
                   kmean_ref, carry_ref):
    i = pl.program_id(0)
    il = i % nb

    @pl.when(i == 0)
    def _():
        kmean_ref[...] = jnp.zeros_like(kmean_ref)

    @pl.when(il == 0)
    def _():
        carry_ref[...] = jnp.zeros_like(carry_ref)

    xb = x_ref[...].astype(BF16)

    z = jax.nn.gelu(_dot(xb, wz_ref[...]))
    u = z[:, :SGU_WIDTH]
    vn = _layer_norm(z[:, SGU_WIDTH:], lng_ref[...], lnb_ref[...]).astype(BF16)
    r_io = lax.broadcasted_iota(jnp.int32, (SGU_CHUNK, SGU_CHUNK), 0)
    c_io = lax.broadcasted_iota(jnp.int32, (SGU_CHUNK, SGU_CHUNK), 1)
    tril = r_io >= c_io
    for g in range(SGU_GROUPS):
        w = jnp.where(tril, sw_ref[g], jnp.zeros((), BF16))
        bias = sbt_ref[:, g:g + 1]
        gs = slice(g * SGU_CHUNK, (g + 1) * SGU_CHUNK)
        for c in range(MOBA_BLOCK // SGU_CHUNK):
            rs = slice(c * SGU_CHUNK, (c + 1) * SGU_CHUNK)
            mixed = _dot(w, vn[rs, gs]) + bias
            bra_ref[rs, gs] = (u[rs, gs] * mixed).astype(BF16)

    ut = _dot_nt(wst_ref[...], xb)
    for c in range(MOBA_BLOCK // S5_CHUNK):
        u_ref[c] = ut[:, c * S5_CHUNK:(c + 1) * S5_CHUNK]

    lane = lax.broadcasted_iota(jnp.int32, (1, HEAD_PAD), 1)
    is_head = lane < HEAD_DIM
    blk_lane = jnp.where(lane == Q_LANE_BLK, il.astype(F32), 0.0)
    q2 = _dot(xb, wq_ref[...])
    k2 = _dot(xb, wk_ref[...])
    qh, kmeans = [], []
    for pair in range(ATT_HEADS // 2):
        ps = slice(pair * HEAD_PAD, (pair + 1) * HEAD_PAD)
        q_pair, k_pair = q2[:, ps], k2[:, ps]
        sources = ((q_pair, k_pair), (pltpu.roll(q_pair, HEAD_DIM, axis=1), pltpu.roll(k_pair, HEAD_DIM, axis=1)))
        for odd, (q_src, k_src) in enumerate(sources):
            h = 2 * pair + odd
            hs = slice(h * HEAD_PAD, (h + 1) * HEAD_PAD)
            qh.append(jnp.where(is_head, q_src, eq_ref[:, hs] + blk_lane))
            q_ref[0, h] = qh[h]
            k_ref[:, hs] = jnp.where(is_head, k_src, ek_ref[:, hs]).astype(BF16)
            kmeans.append(jnp.mean(k_src, axis=0, keepdims=True))
    kmean_ref[pl.ds(il, 1), :] = jnp.concatenate(kmeans, axis=1)
    vt_ref[0] = _dot_nt(wvt_ref[...], xb).astype(BF16)

    blk = lax.broadcasted_iota(jnp.int32, (nb, MOBA_BLOCK), 0)
    neg_inf = jnp.full((), -jnp.inf, F32)
    r_io = lax.broadcasted_iota(jnp.int32, (MOBA_BLOCK, MOBA_BLOCK), 0)
    c_io = lax.broadcasted_iota(jnp.int32, (MOBA_BLOCK, MOBA_BLOCK), 1)
    earlier = (r_io < c_io).astype(BF16)
    heads = range(ATT_HEADS)
    past = blk < il
    gates = []
    for h in heads:
        hs = slice(h * HEAD_PAD, h * HEAD_PAD + HEAD_DIM)
        gates.append(jnp.where(past, _dot_nt(kmean_ref[:, hs], qh[h][:, :HEAD_DIM], precision=_HI), neg_inf))
    sels = [[] for _ in heads]
    for r in range(MOBA_TOPK):
        ms = [jnp.max(g, axis=0, keepdims=True) for g in gates]
        idxs = [jnp.min(jnp.where(gates[h] == ms[h], blk, nb), axis=0, keepdims=True) for h in heads]
        for h in heads:
            sels[h].append(jnp.where(r < il, idxs[h], -1))
        gates = [jnp.where(blk == idxs[h], neg_inf, gates[h]) for h in heads]
    hits = [[blk == s for s in sels[h]] for h in heads]
    onehots = [jnp.where(hits[h][0] | hits[h][1] | hits[h][2], 1.0, 0.0) for h in heads]
    befores = [carry_ref[h] + _dot(onehots[h].astype(BF16), earlier) for h in heads]
    for h in heads:
        carry_ref[h] = carry_ref[h] + jnp.sum(onehots[h], axis=1, keepdims=True)
        for r in range(MOBA_TOPK):
            sel_ref[0, r, h:h + 1, :] = sels[h][r]
            rank = jnp.sum(jnp.where(hits[h][r], befores[h], 0.0), axis=0, keepdims=True)
            rank_ref[0, r, h:h + 1, :] = rank.astype(jnp.int32)

    @pl.when(il == nb - 1)
    def _():
        cnt_ref[0] = carry_ref[...]


def _inproj(x2, wz, wst, wq, wk, wvt, eq, ek, lng, lnb, sw, sbt, nb, first_block):
    seq = nb * MOBA_BLOCK
    n, bsz, grid = seq, 1, nb
    const = lambda i: (0, 0)
    return pl.pallas_call(
        functools.partial(_inproj_kernel, nb),
        grid=(grid,),
        in_specs=[
            pl.BlockSpec((MOBA_BLOCK, D_MODEL), lambda i: (first_block + i, 0)),
            pl.BlockSpec(wz.shape, const),
            pl.BlockSpec(wst.shape, const),
            pl.BlockSpec(wq.shape, const),
            pl.BlockSpec(wk.shape, const),
            pl.BlockSpec(wvt.shape, const),
            pl.BlockSpec(eq.shape, const),
            pl.BlockSpec(ek.shape, const),
            pl.BlockSpec(lng.shape, const),
            pl.BlockSpec(lnb.shape, const),
            pl.BlockSpec(sw.shape, lambda i: (0, 0, 0)),
            pl.BlockSpec(sbt.shape, const),
        ],
        out_specs=[
            pl.BlockSpec((MOBA_BLOCK, SGU_WIDTH), lambda i: (i, 0)),
            pl.BlockSpec((MOBA_BLOCK // S5_CHUNK, SSM_WIDTH, S5_CHUNK), lambda i: (i, 0, 0)),
            pl.BlockSpec((1, ATT_HEADS, MOBA_BLOCK, HEAD_PAD), lambda i: (i // nb, 0, i % nb, 0)),
            pl.BlockSpec((MOBA_BLOCK, QK_PAD), lambda i: (i, 0)),
            pl.BlockSpec((1, ATT_WIDTH, MOBA_BLOCK), lambda i: (i // nb, 0, i % nb)),
            pl.BlockSpec((1, MOBA_TOPK, ATT_HEADS, MOBA_BLOCK), lambda i: (i // nb, 0, 0, i % nb)),
            pl.BlockSpec((1, MOBA_TOPK, ATT_HEADS, MOBA_BLOCK), lambda i: (i // nb, 0, 0, i % nb)),
            pl.BlockSpec((1, ATT_HEADS, nb, MOBA_BLOCK), lambda i: (i // nb, 0, 0, 0)),
        ],
        out_shape=[
            jax.ShapeDtypeStruct((n, SGU_WIDTH), BF16),
            jax.ShapeDtypeStruct((n // S5_CHUNK, SSM_WIDTH, S5_CHUNK), F32),
            jax.ShapeDtypeStruct((bsz, ATT_HEADS, seq, HEAD_PAD), F32),
            jax.ShapeDtypeStruct((n, QK_PAD), BF16),
            jax.ShapeDtypeStruct((bsz, ATT_WIDTH, seq), BF16),
            jax.ShapeDtypeStruct((bsz, MOBA_TOPK, ATT_HEADS, seq), jnp.int32),
            jax.ShapeDtypeStruct((bsz, MOBA_TOPK, ATT_HEADS, seq), jnp.int32),
            jax.ShapeDtypeStruct((bsz, ATT_HEADS, nb, MOBA_BLOCK), F32),
        ],
        scratch_shapes=[pltpu.VMEM((nb, QK_PAD), F32),
                        pltpu.VMEM((ATT_HEADS, nb, MOBA_BLOCK), F32)],
        compiler_params=pltpu.CompilerParams(
            dimension_semantics=("arbitrary",), vmem_limit_bytes=VMEM_LIMIT),
        name="inproj_sgu_gate",
    )(x2, wz, wst, wq, wk, wvt, eq, ek, lng, lnb, sw, sbt)


def _s5_param_kernel(lre_r, lim_r, lre_c, lim_c, ldt, bt_re, bt_im, ct_re, ct_im, cr_r, ci_r,
                     t_ref, w_ref, m_ref, a_ref):
    two_p = 2 * SSM_STATE
    dt = jnp.exp(ldt[0])

    def powers(ar, ai, e):
        mag = jnp.exp(ar * e)
        return mag * jnp.cos(ai * e), mag * jnp.sin(ai * e)

    lr, li = lre_r[0], lim_r[0]
    ar, ai = lr * dt, li * dt
    lbr, lbi = powers(ar, ai, 1.0)
    den = lr * lr + li * li
    cfr = ((lbr - 1.0) * lr + lbi * li) / den
    cfi = (lbi * lr - (lbr - 1.0) * li) / den
    bbr = cfr * bt_re[0] - cfi * bt_im[0]
    bbi = cfr * bt_im[0] + cfi * bt_re[0]

    chunk = S5_CHUNK
    s_col = lax.broadcasted_iota(jnp.int32, (chunk, two_p), 0).astype(F32)
    first = lax.broadcasted_iota(jnp.int32, (chunk, two_p), 1) < SSM_STATE
    rev_r, rev_i = powers(ar, ai, (chunk - 1.0) - s_col)

    for hq in range(SSM_GROUP):
        br, bi = bbr[hq:hq + 1, :], bbi[hq:hq + 1, :]
        w = jnp.where(first, rev_r * br - rev_i * bi, rev_r * bi + rev_i * br)
        w_ref[0, hq * chunk:(hq + 1) * chunk, :] = w.astype(BF16)

    lrc, lic = lre_c[0], lim_c[0]
    arc, aic = lrc * dt, lic * dt
    t_row = lax.broadcasted_iota(jnp.int32, (two_p, chunk), 1).astype(F32)
    top = lax.broadcasted_iota(jnp.int32, (two_p, chunk), 0) < SSM_STATE
    pw_r, pw_i = powers(arc, aic, t_row)
    er, ei = powers(arc, aic, t_row + 1.0)

    for h in range(SSM_GROUP):
        cr, ci = ct_re[0][:, h:h + 1], ct_im[0][:, h:h + 1]
        m = jnp.where(top, er * cr - ei * ci, -(er * ci + ei * cr))
        m_ref[0, :, h * chunk:(h + 1) * chunk] = m.astype(BF16)

    first16 = lax.broadcasted_iota(jnp.int32, (SSM_GROUP, two_p), 1) < SSM_STATE
    crr, cir = cr_r[0], ci_r[0]
    g2 = jnp.concatenate(
        [jnp.where(first16, crr * bbr[hq:hq + 1, :] - cir * bbi[hq:hq + 1, :],
                   -(crr * bbi[hq:hq + 1, :] + cir * bbr[hq:hq + 1, :])) for hq in range(SSM_GROUP)],
        axis=0)
    taps = _dot(g2, jnp.where(top, pw_r, pw_i), _HI)

    causal = (lax.broadcasted_iota(jnp.int32, (chunk, chunk), 0)
              <= lax.broadcasted_iota(jnp.int32, (chunk, chunk), 1))
    for hq in range(SSM_GROUP):
        for h in range(SSM_GROUP):
            row = hq * SSM_GROUP + h
            k_rows = jnp.broadcast_to(taps[row:row + 1, :], (chunk, chunk))
            toep = pltpu.roll(k_rows, 0, axis=1, stride=1, stride_axis=0)
            t_ref[0, hq * chunk:(hq + 1) * chunk, h * chunk:(h + 1) * chunk] = (
                jnp.where(causal, toep, 0.0).astype(BF16))

    pr, pi_ = lbr, lbi
    for _ in range(S5_CHUNK.bit_length() - 1):
        pr, pi_ = pr * pr - pi_ * pi_, 2.0 * pr * pi_
    a_ref[0, 0:1, :] = pr
    a_ref[0, 1:2, :] = pi_


def _s5_params(lre, lim, ldt, b_re, b_im, c_re, c_im):
    g, p = lre.shape
    dup = lambda a: jnp.concatenate([a, a], axis=-1)
    lre_r, lim_r = dup(lre)[:, None, :], dup(lim)[:, None, :]
    lre_c, lim_c = dup(lre)[:, :, None], dup(lim)[:, :, None]
    bt_re = dup(jnp.swapaxes(b_re, 1, 2))
    bt_im = dup(jnp.swapaxes(b_im, 1, 2))
    ct = lambda c: jnp.concatenate([jnp.swapaxes(c, 1, 2)] * 2, axis=1)
    ct_re, ct_im = ct(c_re), ct(c_im)
    ldt3 = ldt[:, None, None]
    return (lre_r, lim_r, lre_c, lim_c, ldt3, bt_re, bt_im, ct_re, ct_im, dup(c_re), dup(c_im))


N_S5_PARAMS = 11


def _s5_kernel(cpb, nseq, *refs):
    params = refs[:N_S5_PARAMS]
    u_refs = refs[N_S5_PARAMS:N_S5_PARAMS + nseq]
    d_ref = refs[N_S5_PARAMS + nseq]
    y_refs = refs[N_S5_PARAMS + nseq + 1:N_S5_PARAMS + 2 * nseq + 1]
    t_scr, w_scr, m_scr, a_scr = refs[N_S5_PARAMS + 2 * nseq + 1:]
    _s5_param_kernel(*params, t_scr, w_scr, m_scr, a_scr)
    _s5_scan_kernel(cpb, u_refs, t_scr, w_scr, m_scr, a_scr, d_ref, y_refs)


def _s5_scan_kernel(cpb, u_refs, t_ref, w_ref, m_ref, a_ref, d_ref, y_refs):
    us = [jnp.concatenate([u_ref[:, hq, :] for u_ref in u_refs], axis=0)
          for hq in range(SSM_GROUP)]
    ub = jnp.concatenate([u.astype(BF16) for u in us], axis=1)
    rows = ub.shape[0]
    two_p = 2 * SSM_STATE
    h = _dot(ub, w_ref[0])
    ar = a_ref[0, 0:1, :]
    ai = a_ref[0, 1:2, :]
    lane = lax.broadcasted_iota(jnp.int32, (1, two_p), 1)
    sign = jnp.where(lane < SSM_STATE, -1.0, 1.0)
    chunk = lax.broadcasted_iota(jnp.int32, (rows, two_p), 0) % cpb
    step = 1
    while step < cpb:
        prev = jnp.where(chunk >= step, pltpu.roll(h, step, axis=0), 0.0)
        swapped = pltpu.roll(prev, SSM_STATE, axis=1)
        h = h + prev * ar + swapped * (ai * sign)
        ar, ai = ar * ar - ai * ai, 2.0 * ar * ai
        step *= 2
    hprev = jnp.where(chunk >= 1, pltpu.roll(h, 1, axis=0), 0.0)
    y = _dot(ub, t_ref[0]) + _dot(hprev.astype(BF16), m_ref[0])
    for hq in range(SSM_GROUP):
        yh = jax.nn.gelu(y[:, hq * S5_CHUNK:(hq + 1) * S5_CHUNK] + d_ref[0, hq:hq + 1, :] * us[hq])
        for b, y_ref in enumerate(y_refs):
            y_ref[:, hq, :] = yh[b * cpb:(b + 1) * cpb]


def _s5(utcs, params, dtab, cpb):
    nseq = len(utcs)
    two_p = 2 * SSM_STATE
    idx = lambda i: (i, 0, 0)
    chan = pl.BlockSpec((cpb, SSM_GROUP, S5_CHUNK), lambda i: (0, i, 0))
    return pl.pallas_call(
        functools.partial(_s5_kernel, cpb, nseq),
        grid=(SSM_GROUPS,),
        in_specs=([pl.BlockSpec((1,) + a.shape[1:], idx) for a in params]
                  + [chan] * nseq + [pl.BlockSpec((1, SSM_GROUP, S5_CHUNK), idx)]),
        out_specs=[chan] * nseq,
        out_shape=[jax.ShapeDtypeStruct(u.shape, F32) for u in utcs],
        scratch_shapes=[pltpu.VMEM((1, S5_COLS, S5_COLS), BF16),
                        pltpu.VMEM((1, S5_COLS, two_p), BF16),
                        pltpu.VMEM((1, two_p, S5_COLS), BF16),
                        pltpu.VMEM((1, 2, two_p), F32)],
        compiler_params=pltpu.CompilerParams(vmem_limit_bytes=VMEM_LIMIT),
        name="s5_scan",
    )(*params, *utcs, dtab)


def _plan_kernel(nb, tpb, cnt_ref, start_ref, tmap_ref, nused_ref):
    cnt = cnt_ref[0, 0]
    padded = jnp.floor((cnt + (MOBA_BLOCK - 1.0)) * (1.0 / MOBA_BLOCK)) * MOBA_BLOCK
    r_io = lax.broadcasted_iota(jnp.int32, (nb, nb), 0)
    c_io = lax.broadcasted_iota(jnp.int32, (nb, nb), 1)
    start = _dot((c_io < r_io).astype(F32), padded, _HI)
    start_ref[0, 0] = start
    end = (start + padded)[:, 0:1]
    tile_row = lax.broadcasted_iota(jnp.int32, (nb, tpb), 1).astype(F32) * MOBA_BLOCK
    blk_of_tile = jnp.sum(jnp.where(end <= tile_row, 1.0, 0.0), axis=0, keepdims=True)
    tmap_ref[0] = jnp.minimum(blk_of_tile, nb - 1.0).astype(jnp.int32)
    total = jnp.max(end, axis=0, keepdims=True)
    nused_ref[0] = jnp.broadcast_to(total * (1.0 / MOBA_BLOCK), (1, HEAD_PAD)).astype(jnp.int32)


def _plan(cnt, nb, tpb):
    bsz = cnt.shape[0]
    nbh = bsz * ATT_HEADS
    return pl.pallas_call(
        functools.partial(_plan_kernel, nb, tpb),
        grid=(bsz, ATT_HEADS),
        in_specs=[pl.BlockSpec((1, 1, nb, MOBA_BLOCK), lambda b, h: (b, h, 0, 0))],
        out_specs=[
            pl.BlockSpec((1, 1, nb, MOBA_BLOCK), lambda b, h: (b, h, 0, 0)),
            pl.BlockSpec((1, 1, tpb), lambda b, h: (b * ATT_HEADS + h, 0, 0)),
            pl.BlockSpec((1, 1, HEAD_PAD), lambda b, h: (b * ATT_HEADS + h, 0, 0)),
        ],
        out_shape=[
            jax.ShapeDtypeStruct((bsz, ATT_HEADS, nb, MOBA_BLOCK), F32),
            jax.ShapeDtypeStruct((nbh, 1, tpb), jnp.int32),
            jax.ShapeDtypeStruct((nbh, 1, HEAD_PAD), jnp.int32),
        ],
        name="route_plan",
    )(cnt)


def _pos_kernel(nb, cap, trash, sel_ref, rank_ref, start_ref, dst_ref):
    b = pl.program_id(0) // nb
    blk = lax.broadcasted_iota(jnp.int32, (nb, MOBA_BLOCK), 0)
    lane = lax.broadcasted_iota(jnp.int32, (1, MOBA_BLOCK), 1)
    for h in range(ATT_HEADS):
        start = start_ref[0, h]
        base = (b * ATT_HEADS + h) * cap
        for r in range(MOBA_TOPK):
            s = sel_ref[r, h:h + 1, :]
            first = jnp.sum(jnp.where(blk == s, start, 0.0), axis=0, keepdims=True).astype(jnp.int32)
            dst = base + first + rank_ref[r, h:h + 1, :]
            dst_ref[r, h:h + 1, :] = jnp.where(s >= 0, dst, trash + lane)


def _pos(sel, rank, start, nb, cap, trash):
    n = sel.shape[-1]
    blk3 = pl.BlockSpec((MOBA_TOPK, ATT_HEADS, MOBA_BLOCK), lambda i: (0, 0, i))
    return pl.pallas_call(
        functools.partial(_pos_kernel, nb, cap, trash),
        grid=(n // MOBA_BLOCK,),
        in_specs=[blk3, blk3,
                  pl.BlockSpec((1, ATT_HEADS, nb, MOBA_BLOCK), lambda i: (i // nb, 0, 0, 0))],
        out_specs=blk3,
        out_shape=jax.ShapeDtypeStruct(sel.shape, jnp.int32),
        name="route_pos",
    )(sel, rank, start)


def _sc_mesh():
    return plsc.VectorSubcoreMesh(core_axis_name="core", subcore_axis_name="subcore")


def _sc_scatter_rows(x, idx, rows_out):
    nrep, nin = idx.shape

    @pl.kernel(out_type=jax.ShapeDtypeStruct((rows_out, HEAD_PAD), x.dtype), mesh=_sc_mesh(),
               scratch_types=[])
    def scatter(x_hbm, i_hbm, o_hbm):
        def body(x_vmem, *i_vmems):
            for i_vmem in i_vmems:
                pltpu.sync_copy(x_vmem, o_hbm.at[i_vmem.at[0]])

        idx_spec = lambda r: pl.BlockSpec((1, SC_WINDOW), lambda i: (r, i))
        pltpu.emit_pipeline(
            body,
            grid=(nin // SC_WINDOW,),
            in_specs=[pl.BlockSpec((SC_WINDOW, HEAD_PAD), lambda i: (i, 0))]
                     + [idx_spec(r) for r in range(nrep)],
            out_specs=[],
            core_axis_name=("core", "subcore"),
            dimension_semantics=(pltpu.PARALLEL,),
        )(x_hbm, *([i_hbm] * nrep))

    return scatter(x, idx)


def _sc_gather_rows(x, idx):
    nout = idx.shape[1]

    @pl.kernel(out_type=jax.ShapeDtypeStruct((nout, HEAD_PAD), x.dtype), mesh=_sc_mesh())
    def gather(x_hbm, i_hbm, o_hbm):
        def body(i_vmem, o_vmem):
            pltpu.sync_copy(x_hbm.at[i_vmem.at[0]], o_vmem)

        pltpu.emit_pipeline(
            body,
            grid=(nout // SC_WINDOW,),
            in_specs=[pl.BlockSpec((1, SC_WINDOW), lambda i: (0, i))],
            out_specs=[pl.BlockSpec((SC_WINDOW, HEAD_PAD), lambda i: (i, 0))],
            core_axis_name=("core", "subcore"),
            dimension_semantics=(pltpu.PARALLEL,),
        )(i_hbm, o_hbm)

    return gather(x, idx)


def _routed_kernel(tmap_ref, nused_ref, qs_ref, slope_ref, k_ref, vt_ref, o_ref):
    h = pl.program_id(1)
    bh = pl.program_id(0) * ATT_HEADS + h
    s = pl.program_id(2)

    @pl.when(s * ROUTED_TILES < nused_ref[bh])
    def _():
        slope = slope_ref[pl.ds(h, 1), :]
        row = lax.broadcasted_iota(jnp.int32, (HEAD_DIM, MOBA_BLOCK), 0)
        tiles = range(ROUTED_TILES)
        js = [tmap_ref[bh, s * ROUTED_TILES + u] for u in tiles]
        ks = [pl.multiple_of(j * MOBA_BLOCK, MOBA_BLOCK) for j in js]
        zs = [_dot_nt(k_ref[pl.ds(ks[u], MOBA_BLOCK), :],
                      qs_ref[u * MOBA_BLOCK:(u + 1) * MOBA_BLOCK, :].astype(BF16)) for u in tiles]
        ms = [jnp.max(z, axis=0, keepdims=True) for z in zs]
        ps = [jnp.exp(zs[u] - ms[u]) for u in tiles]
        ls = [jnp.sum(p, axis=0, keepdims=True) for p in ps]
        ots = [_dot(vt_ref[:, pl.ds(ks[u], MOBA_BLOCK)], ps[u].astype(BF16)) for u in tiles]
        for u in tiles:
            m = ms[u] + slope * jnp.full((1, MOBA_BLOCK), ks[u], jnp.int32).astype(F32)
            stats = jnp.where(row == 0, m, jnp.where(row == 1, ls[u], 0.0))
            o_ref[u * MOBA_BLOCK:(u + 1) * MOBA_BLOCK, :] = jnp.concatenate([ots[u], stats], axis=0).T


def _routed(qs, ka, vt, slope_tab, tmap, nused, bsz, seq, tpb):
    steps = tpb // ROUTED_TILES
    rows = ROUTED_TILES * MOBA_BLOCK
    shift = ROUTED_TILES.bit_length() - 1

    def step_of(b, h, s, tm, nu):
        bh = b * ATT_HEADS + h
        used = lax.shift_right_logical(nu[bh] + (ROUTED_TILES - 1), shift)
        return bh * steps + jnp.minimum(s, jnp.maximum(used - 1, 0)), 0

    grid_spec = pltpu.PrefetchScalarGridSpec(
        num_scalar_prefetch=2,
        grid=(bsz, ATT_HEADS, steps),
        in_specs=[
            pl.BlockSpec((rows, HEAD_PAD), step_of),
            pl.BlockSpec(slope_tab.shape, lambda b, h, s, tm, nu: (0, 0)),
            pl.BlockSpec((seq, HEAD_PAD), lambda b, h, s, tm, nu: (b, h)),
            pl.BlockSpec((HEAD_DIM, seq), lambda b, h, s, tm, nu: (h, b)),
        ],
        out_specs=pl.BlockSpec((rows, HEAD_PAD), step_of),
    )
    return pl.pallas_call(
        _routed_kernel,
        grid_spec=grid_spec,
        out_shape=jax.ShapeDtypeStruct(qs.shape, F32),
        compiler_params=pltpu.CompilerParams(vmem_limit_bytes=VMEM_LIMIT),
        name="moba_routed",
    )(tmap, nused, qs, slope_tab, ka, vt)


def _combine_kernel(nb, q_ref, k_ref, vt_ref, sel_ref, g_ref, slope_ref, o_ref):
    il = pl.program_id(0) % nb
    kio = lax.broadcasted_iota(jnp.int32, (MOBA_BLOCK, MOBA_BLOCK), 0)
    qio = lax.broadcasted_iota(jnp.int32, (MOBA_BLOCK, MOBA_BLOCK), 1)
    causal = kio <= qio
    own_shift = jnp.full((1, MOBA_BLOCK), il * MOBA_BLOCK, jnp.int32).astype(F32)
    outs = []
    heads = range(ATT_HEADS)
    zs = [_dot_nt(k_ref[:, h * HEAD_PAD:(h + 1) * HEAD_PAD], q_ref[h].astype(BF16)) for h in heads]
    zs = [jnp.where(causal, z, NEG_BIG) for z in zs]
    ms = [jnp.max(z, axis=0, keepdims=True) for z in zs]
    ps = [jnp.exp(zs[h] - ms[h]) for h in heads]
    ls = [jnp.sum(p, axis=0, keepdims=True) for p in ps]
    os_ = [_dot(vt_ref[h * HEAD_DIM:(h + 1) * HEAD_DIM, :], ps[h].astype(BF16)) for h in heads]
    for h in heads:
        l0, o0 = ls[h], os_[h]
        m0 = ms[h] + slope_ref[h:h + 1, :] * own_shift
        parts = []
        for r in range(MOBA_TOPK):
            gt = g_ref[r, h].T
            valid = sel_ref[r, h:h + 1, :] >= 0
            parts.append((jnp.where(valid, gt[HEAD_DIM:HEAD_DIM + 1, :], NEG_BIG),
                          jnp.where(valid, gt[HEAD_DIM + 1:HEAD_DIM + 2, :], 0.0),
                          jnp.where(valid, gt[:HEAD_DIM, :], 0.0)))
        m = m0
        for mr, _, _ in parts:
            m = jnp.maximum(m, mr)
        w = jnp.exp(m0 - m)
        num, den = w * o0, w * l0
        for mr, lr, orr in parts:
            w = jnp.exp(mr - m)
            num, den = num + w * orr, den + w * lr
        outs.append(num / den)
    o_ref[...] = jnp.concatenate(outs, axis=0).T.astype(BF16)


def _combine(qhm, ka, vt, sel, g4, slope_tab, nb):
    n = ka.shape[0]
    return pl.pallas_call(
        functools.partial(_combine_kernel, nb),
        grid=(n // MOBA_BLOCK,),
        in_specs=[
            pl.BlockSpec((ATT_HEADS, MOBA_BLOCK, HEAD_PAD), lambda i: (0, i, 0)),
            pl.BlockSpec((MOBA_BLOCK, QK_PAD), lambda i: (i, 0)),
            pl.BlockSpec((ATT_WIDTH, MOBA_BLOCK), lambda i: (0, i)),
            pl.BlockSpec((MOBA_TOPK, ATT_HEADS, MOBA_BLOCK), lambda i: (0, 0, i)),
            pl.BlockSpec((MOBA_TOPK, ATT_HEADS, MOBA_BLOCK, HEAD_PAD), lambda i: (0, 0, i, 0)),
            pl.BlockSpec(slope_tab.shape, lambda i: (0, 0)),
        ],
        out_specs=pl.BlockSpec((MOBA_BLOCK, ATT_WIDTH), lambda i: (i, 0)),
        out_shape=jax.ShapeDtypeStruct((n, ATT_WIDTH), BF16),
        compiler_params=pltpu.CompilerParams(vmem_limit_bytes=VMEM_LIMIT),
        name="moba_own_combine",
    )(qhm, ka, vt, sel, g4, slope_tab)


def _merge_kernel(x_ref, bra_ref, ys_ref, brc_ref, wg_ref, wa_ref, wb_ref, wc_ref, gw_ref, gb_ref,
                  wo_ref, lg_ref, lb_ref, o_ref):
    x = x_ref[...]
    xb = x.astype(BF16)

    def gate(k):
        return jax.nn.sigmoid(_dot(xb, wg_ref[:, k * D_MODEL:(k + 1) * D_MODEL]))

    merged = gate(0) * _dot(bra_ref[...], wa_ref[...])
    ys = jnp.concatenate([ys_ref[c].T for c in range(ys_ref.shape[0])], axis=0)
    brb = ys * jax.nn.sigmoid(_dot(ys.astype(BF16), gw_ref[...]) + gb_ref[...])
    merged = merged + gate(1) * _dot(brb.astype(BF16), wb_ref[...])
    merged = merged + gate(2) * _dot(brc_ref[...], wc_ref[...])
    mix = _dot(merged.astype(BF16), wo_ref[...])
    o_ref[...] = _layer_norm(DN_ALPHA * x + mix, lg_ref[...], lb_ref[...])


def _merge(x2, bra, ys, brc, wg, wa, wb, wc, gw, gb, wo, lg, lb, tm, first_tile):
    n = bra.shape[0]
    const = lambda i: (0, 0)
    tile = lambda w: pl.BlockSpec((tm, w), lambda i: (i, 0))
    full = lambda a: pl.BlockSpec(a.shape, const)
    return pl.pallas_call(
        _merge_kernel,
        grid=(n // tm,),
        in_specs=[pl.BlockSpec((tm, D_MODEL), lambda i: (first_tile + i, 0)), tile(SGU_WIDTH),
                  pl.BlockSpec((tm // S5_CHUNK, SSM_WIDTH, S5_CHUNK), lambda i: (i, 0, 0)),
                  tile(ATT_WIDTH),
                  full(wg), full(wa), full(wb), full(wc), full(gw), full(gb), full(wo), full(lg), full(lb)],
        out_specs=tile(D_MODEL),
        out_shape=jax.ShapeDtypeStruct((n, D_MODEL), F32),
        compiler_params=pltpu.CompilerParams(vmem_limit_bytes=VMEM_LIMIT),
        name="merge_ln",
    )(x2, bra, ys, brc, wg, wa, wb, wc, gw, gb, wo, lg, lb)


FF_CHUNK = D_FF // 2


def _ffn_kernel(x_ref, w1_ref, w3_ref, w2_ref, lg_ref, lb_ref, *rest):
    o_ref = rest[-1]
    x = x_ref[...]
    xb = x.astype(BF16)
    acc = None
    for c in range(D_FF // FF_CHUNK):
        cs = slice(c * FF_CHUNK, (c + 1) * FF_CHUNK)
        h = (jax.nn.silu(_dot(xb, w1_ref[:, cs])) * _dot(xb, w3_ref[:, cs])).astype(BF16)
        part = _dot(h, w2_ref[cs, :])
        acc = part if acc is None else acc + part
    o_ref[...] = _layer_norm(DN_ALPHA * x + acc, lg_ref[...], lb_ref[...])


def _ffn(x2, w1, w3, w2, lg, lb, tm, out_rows, first_tile, dest):
    n = x2.shape[0]
    const = lambda i: (0, 0)
    full = lambda a: pl.BlockSpec(a.shape, const)
    in_specs = [pl.BlockSpec((tm, D_MODEL), lambda i: (i, 0)),
                full(w1), full(w3), full(w2), full(lg), full(lb)]
    args = [x2, w1, w3, w2, lg, lb]
    aliases = {}
    if dest is not None:
        in_specs.append(pl.BlockSpec(memory_space=pl.ANY))
        aliases = {len(args): 0}
        args.append(dest)
    return pl.pallas_call(
        _ffn_kernel,
        grid=(n // tm,),
        in_specs=in_specs,
        out_specs=pl.BlockSpec((tm, D_MODEL), lambda i: (first_tile + i, 0)),
        out_shape=jax.ShapeDtypeStruct((out_rows, D_MODEL), F32),
        input_output_aliases=aliases,
        compiler_params=pltpu.CompilerParams(vmem_limit_bytes=VMEM_LIMIT),
        name="ffn_ln",
    )(*args)


def _alibi_extras():
    slopes = 2.0 ** (-8.0 * jnp.arange(1, ATT_HEADS + 1, dtype=F32) / ATT_HEADS)
    row = jnp.arange(MOBA_BLOCK, dtype=F32)
    eq = jnp.zeros((MOBA_BLOCK, ATT_HEADS, HEAD_PAD), F32)
    eq = eq.at[:, :, Q_LANE_ONE].set(1.0).at[:, :, Q_LANE_ROW].set(row[:, None])
    ek = jnp.zeros((MOBA_BLOCK, ATT_HEADS, HEAD_PAD), F32)
    ek = ek.at[:, :, Q_LANE_ONE].set(row[:, None] * slopes[None, :])
    ek = ek.at[:, :, Q_LANE_BLK].set(-slopes[None, :] * MOBA_BLOCK).at[:, :, Q_LANE_ROW].set(-slopes[None, :])
    slope_tab = jnp.broadcast_to(slopes[:, None], (ATT_HEADS, MOBA_BLOCK))
    return eq.reshape(MOBA_BLOCK, QK_PAD), ek.reshape(MOBA_BLOCK, QK_PAD), slope_tab


def kernel(x, w_in, sgu_ln_g, sgu_ln_b, sgu_w, sgu_b, ssm_lambda_re, ssm_lambda_im, ssm_log_dt,
           ssm_b_re, ssm_b_im, ssm_c_re, ssm_c_im, ssm_d, glu_w, glu_b, w_branch_a, w_branch_b,
           w_branch_c, w_out, ln1_g, ln1_b, ffn_w1, ffn_w3, ffn_w2, ln2_g, ln2_b):
    bsz, seq, _ = x.shape
    n = bsz * seq
    nb = seq // MOBA_BLOCK
    cpb = seq // S5_CHUNK
    tm = 512 if n % 512 == 0 else MOBA_BLOCK
    eq, ek, slope_tab = _alibi_extras()
    scale = HEAD_DIM ** -0.5
    cap = (MOBA_TOPK + 1) * seq
    tpb = cap // MOBA_BLOCK
    trash = ATT_HEADS * cap
    o_q = 2 * SGU_WIDTH + SSM_WIDTH
    o_g = o_q + 3 * ATT_WIDTH

    xs = [(x.reshape(n, D_MODEL), b * nb) for b in range(bsz)]
    out = None
    for l in range(DEPTH):
        last = l == DEPTH - 1
        wl = w_in[l]
        wz = wl[:, :2 * SGU_WIDTH].astype(BF16)
        wst = wl[:, 2 * SGU_WIDTH:o_q].T.astype(BF16)
        wq = (wl[:, o_q:o_q + ATT_WIDTH] * scale).astype(BF16)
        wk = wl[:, o_q + ATT_WIDTH:o_q + 2 * ATT_WIDTH].astype(BF16)
        wvt = wl[:, o_q + 2 * ATT_WIDTH:o_g].T.astype(BF16)
        wg = wl[:, o_g:].astype(BF16)

        proj = [_inproj(xs[b][0], wz, wst, wq, wk, wvt, eq, ek, sgu_ln_g[l][None, :],
                        sgu_ln_b[l][None, :], sgu_w[l].astype(BF16), sgu_b[l].T, nb, xs[b][1])
                for b in range(bsz)]

        s5_params = _s5_params(
            ssm_lambda_re[l], ssm_lambda_im[l], ssm_log_dt[l], ssm_b_re[l], ssm_b_im[l],
            ssm_c_re[l], ssm_c_im[l])
        dtab = jnp.broadcast_to(ssm_d[l].reshape(SSM_GROUPS, SSM_GROUP, 1),
                                (SSM_GROUPS, SSM_GROUP, S5_CHUNK))
        yss = _s5([p[1] for p in proj], s5_params, dtab, cpb)

        merge_w = (wg, w_branch_a[l].astype(BF16), w_branch_b[l].astype(BF16), w_branch_c[l].astype(BF16),
                   glu_w[l].astype(BF16), glu_b[l][None, :], w_out[l].astype(BF16),
                   ln1_g[l][None, :], ln1_b[l][None, :])
        ffn_w = (ffn_w1[l].astype(BF16), ffn_w3[l].astype(BF16), ffn_w2[l].astype(BF16),
                 ln2_g[l][None, :], ln2_b[l][None, :])
        for b in range(bsz):
            bra, _, qhm, ka, vt, sel, rank, cnt = proj[b]
            start, tmap, nused = _plan(cnt, nb, tpb)
            dst = _pos(sel[0], rank[0], start, nb, cap, trash).reshape(MOBA_TOPK, ATT_HEADS * seq)
            qs = _sc_scatter_rows(qhm.reshape(ATT_HEADS * seq, HEAD_PAD), dst, trash + MOBA_BLOCK)
            part = _routed(qs, ka, vt[0], slope_tab, tmap[:, 0, :], nused[:, 0, 0], 1, seq, tpb)
            g = _sc_gather_rows(part, dst.reshape(1, MOBA_TOPK * ATT_HEADS * seq))
            brc = _combine(qhm[0], ka, vt[0], sel[0],
                           g.reshape(MOBA_TOPK, ATT_HEADS, seq, HEAD_PAD), slope_tab, nb)
            x1 = _merge(xs[b][0], bra, yss[b], brc, *merge_w, tm, xs[b][1] * MOBA_BLOCK // tm)
            if last:
                out = _ffn(x1, *ffn_w, tm, n, b * (seq // tm), out)
            else:
                xs[b] = (_ffn(x1, *ffn_w, tm, seq, 0, None), 0)
    return out.reshape(bsz, seq, D_MODEL)
```

```python
import functools

import jax
import jax.numpy as jnp
from jax import lax
from jax.experimental import pallas as pl
from jax.experimental.pallas import tpu as pltpu
from jax.experimental.pallas import tpu_sc as plsc

F32 = jnp.float32
BF16 = jnp.bfloat16

D_MODEL = 1024
SGU_CHUNK = 128
SGU_GROUPS = 4
SGU_WIDTH = 512
SSM_WIDTH = 512
SSM_GROUP = 16
SSM_GROUPS = 32
SSM_STATE = 64
ATT_HEADS = 8
HEAD_DIM = 64
ATT_WIDTH = 512
MOBA_BLOCK = 256
MOBA_TOPK = 3
D_FF = 2816
DEPTH = 2
DN_ALPHA = (2 * DEPTH) ** 0.25
LN_EPS = 1e-5
NEG_BIG = -1e30

HEAD_PAD = 128
QK_PAD = ATT_HEADS * HEAD_PAD
Q_LANE_ONE = HEAD_DIM
Q_LANE_BLK = HEAD_DIM + 1
Q_LANE_ROW = HEAD_DIM + 2
SC_WINDOW = 128
ROUTED_TILES = 16
S5_CHUNK = 128
S5_COLS = S5_CHUNK * SSM_GROUP
VMEM_LIMIT = 56 * 1024 * 1024

_HI = lax.Precision.HIGHEST


def _dot(a, b, precision=None):
    return jnp.dot(a, b, preferred_element_type=F32, precision=precision)


def _dot_nt(a, b, precision=None):
    return lax.dot_general(a, b, (((1,), (1,)), ((), ())),
                           preferred_element_type=F32, precision=precision)


def _layer_norm(x, g, b):
    mu = jnp.mean(x, axis=-1, keepdims=True)
    xc = x - mu
    var = jnp.mean(xc * xc, axis=-1, keepdims=True)
    return xc * lax.rsqrt(var + LN_EPS) * g + b


def _inproj_kernel(nb, x_ref, wz_ref, wst_ref, wq_ref, wk_ref, wvt_ref, eq_ref, ek_ref, lng_ref, lnb_ref,
                   sw_ref, sbt_ref, bra_ref, u_ref, q_ref, k_ref, vt_ref, sel_ref, rank_ref, cnt_ref,
                   kmean_ref, carry_ref):
    i = pl.program_id(0)
    il = i % nb

    @pl.when(i == 0)
    def _():
        kmean_ref[...] = jnp.zeros_like(kmean_ref)

    @pl.when(il == 0)
    def _():
        carry_ref[...] = jnp.zeros_like(carry_ref)

    xb = x_ref[...].astype(BF16)

    z = jax.nn.gelu(_dot(xb, wz_ref[...]))
    u = z[:, :SGU_WIDTH]
    vn = _layer_norm(z[:, SGU_WIDTH:], lng_ref[...], lnb_ref[...]).astype(BF16)
    r_io = lax.broadcasted_iota(jnp.int32, (SGU_CHUNK, SGU_CHUNK), 0)
    c_io = lax.broadcasted_iota(jnp.int32, (SGU_CHUNK, SGU_CHUNK), 1)
    tril = r_io >= c_io
    for g in range(SGU_GROUPS):
        w = jnp.where(tril, sw_ref[g], jnp.zeros((), BF16))
        bias = sbt_ref[:, g:g + 1]
        gs = slice(g * SGU_CHUNK, (g + 1) * SGU_CHUNK)
        for c in range(MOBA_BLOCK // SGU_CHUNK):
            rs = slice(c * SGU_CHUNK, (c + 1) * SGU_CHUNK)
            mixed = _dot(w, vn[rs, gs]) + bias
            bra_ref[rs, gs] = (u[rs, gs] * mixed).astype(BF16)

    ut = _dot_nt(wst_ref[...], xb)
    for c in range(MOBA_BLOCK // S5_CHUNK):
        u_ref[c] = ut[:, c * S5_CHUNK:(c + 1) * S5_CHUNK]

    lane = lax.broadcasted_iota(jnp.int32, (1, HEAD_PAD), 1)
    is_head = lane < HEAD_DIM
    blk_lane = jnp.where(lane == Q_LANE_BLK, il.astype(F32), 0.0)
    q2 = _dot(xb, wq_ref[...])
    k2 = _dot(xb, wk_ref[...])
    qh, kmeans = [], []
    for pair in range(ATT_HEADS // 2):
        ps = slice(pair * HEAD_PAD, (pair + 1) * HEAD_PAD)
        q_pair, k_pair = q2[:, ps], k2[:, ps]
        sources = ((q_pair, k_pair), (pltpu.roll(q_pair, HEAD_DIM, axis=1), pltpu.roll(k_pair, HEAD_DIM, axis=1)))
        for odd, (q_src, k_src) in enumerate(sources):
            h = 2 * pair + odd
            hs = slice(h * HEAD_PAD, (h + 1) * HEAD_PAD)
            qh.append(jnp.where(is_head, q_src, eq_ref[:, hs] + blk_lane))
            q_ref[0, h] = qh[h]
            k_ref[:, hs] = jnp.where(is_head, k_src, ek_ref[:, hs]).astype(BF16)
            kmeans.append(jnp.mean(k_src, axis=0, keepdims=True))
    kmean_ref[pl.ds(il, 1), :] = jnp.concatenate(kmeans, axis=1)
    vt_ref[0] = _dot_nt(wvt_ref[...], xb).astype(BF16)

    blk = lax.broadcasted_iota(jnp.int32, (nb, MOBA_BLOCK), 0)
    neg_inf = jnp.full((), -jnp.inf, F32)
    r_io = lax.broadcasted_iota(jnp.int32, (MOBA_BLOCK, MOBA_BLOCK), 0)
    c_io = lax.broadcasted_iota(jnp.int32, (MOBA_BLOCK, MOBA_BLOCK), 1)
    earlier = (r_io < c_io).astype(BF16)
    heads = range(ATT_HEADS)
    past = blk < il
    gates = []
    for h in heads:
        hs = slice(h * HEAD_PAD, h * HEAD_PAD + HEAD_DIM)
        gates.append(jnp.where(past, _dot_nt(kmean_ref[:, hs], qh[h][:, :HEAD_DIM], precision=_HI), neg_inf))
    sels = [[] for _ in heads]
    for r in range(MOBA_TOPK):
        ms = [jnp.max(g, axis=0, keepdims=True) for g in gates]
        idxs = [jnp.min(jnp.where(gates[h] == ms[h], blk, nb), axis=0, keepdims=True) for h in heads]
        for h in heads:
            sels[h].append(jnp.where(r < il, idxs[h], -1))
        gates = [jnp.where(blk == idxs[h], neg_inf, gates[h]) for h in heads]
    hits = [[blk == s for s in sels[h]] for h in heads]
    onehots = [jnp.where(hits[h][0] | hits[h][1] | hits[h][2], 1.0, 0.0) for h in heads]
    befores = [carry_ref[h] + _dot(onehots[h].astype(BF16), earlier) for h in heads]
    for h in heads:
        carry_ref[h] = carry_ref[h] + jnp.sum(onehots[h], axis=1, keepdims=True)
        for r in range(MOBA_TOPK):
            sel_ref[0, r, h:h + 1, :] = sels[h][r]
            rank = jnp.sum(jnp.where(hits[h][r], befores[h], 0.0), axis=0, keepdims=True)
            rank_ref[0, r, h:h + 1, :] = rank.astype(jnp.int32)

    @pl.when(il == nb - 1)
    def _():
        cnt_ref[0] = carry_ref[...]


def _inproj(x2, wz, wst, wq, wk, wvt, eq, ek, lng, lnb, sw, sbt, nb, first_block):
    seq = nb * MOBA_BLOCK
    n, bsz, grid = seq, 1, nb
    const = lambda i: (0, 0)
    return pl.pallas_call(
        functools.partial(_inproj_kernel, nb),
        grid=(grid,),
        in_specs=[
            pl.BlockSpec((MOBA_BLOCK, D_MODEL), lambda i: (first_block + i, 0)),
            pl.BlockSpec(wz.shape, const),
            pl.BlockSpec(wst.shape, const),
            pl.BlockSpec(wq.shape, const),
            pl.BlockSpec(wk.shape, const),
            pl.BlockSpec(wvt.shape, const),
            pl.BlockSpec(eq.shape, const),
            pl.BlockSpec(ek.shape, const),
            pl.BlockSpec(lng.shape, const),
            pl.BlockSpec(lnb.shape, const),
            pl.BlockSpec(sw.shape, lambda i: (0, 0, 0)),
            pl.BlockSpec(sbt.shape, const),
        ],
        out_specs=[
            pl.BlockSpec((MOBA_BLOCK, SGU_WIDTH), lambda i: (i, 0)),
            pl.BlockSpec((MOBA_BLOCK // S5_CHUNK, SSM_WIDTH, S5_CHUNK), lambda i: (i, 0, 0)),
            pl.BlockSpec((1, ATT_HEADS, MOBA_BLOCK, HEAD_PAD), lambda i: (i // nb, 0, i % nb, 0)),
            pl.BlockSpec((MOBA_BLOCK, QK_PAD), lambda i: (i, 0)),
            pl.BlockSpec((1, ATT_WIDTH, MOBA_BLOCK), lambda i: (i // nb, 0, i % nb)),
            pl.BlockSpec((1, MOBA_TOPK, ATT_HEADS, MOBA_BLOCK), lambda i: (i // nb, 0, 0, i % nb)),
            pl.BlockSpec((1, MOBA_TOPK, ATT_HEADS, MOBA_BLOCK), lambda i: (i // nb, 0, 0, i % nb)),
            pl.BlockSpec((1, ATT_HEADS, nb, MOBA_BLOCK), lambda i: (i // nb, 0, 0, 0)),
        ],
        out_shape=[
            jax.ShapeDtypeStruct((n, SGU_WIDTH), BF16),
            jax.ShapeDtypeStruct((n // S5_CHUNK, SSM_WIDTH, S5_CHUNK), F32),
            jax.ShapeDtypeStruct((bsz, ATT_HEADS, seq, HEAD_PAD), F32),
            jax.ShapeDtypeStruct((n, QK_PAD), BF16),
            jax.ShapeDtypeStruct((bsz, ATT_WIDTH, seq), BF16),
            jax.ShapeDtypeStruct((bsz, MOBA_TOPK, ATT_HEADS, seq), jnp.int32),
            jax.ShapeDtypeStruct((bsz, MOBA_TOPK, ATT_HEADS, seq), jnp.int32),
            jax.ShapeDtypeStruct((bsz, ATT_HEADS, nb, MOBA_BLOCK), F32),
        ],
        scratch_shapes=[pltpu.VMEM((nb, QK_PAD), F32),
                        pltpu.VMEM((ATT_HEADS, nb, MOBA_BLOCK), F32)],
        compiler_params=pltpu.CompilerParams(
            dimension_semantics=("arbitrary",), vmem_limit_bytes=VMEM_LIMIT),
        name="inproj_sgu_gate",
    )(x2, wz, wst, wq, wk, wvt, eq, ek, lng, lnb, sw, sbt)


def _s5_param_kernel(lre_r, lim_r, lre_c, lim_c, ldt, bt_re, bt_im, ct_re, ct_im, cr_r, ci_r,
                     t_ref, w_ref, m_ref, a_ref):
    two_p = 2 * SSM_STATE
    dt = jnp.exp(ldt[0])

    def powers(ar, ai, e):
        mag = jnp.exp(ar * e)
        return mag * jnp.cos(ai * e), mag * jnp.sin(ai * e)

    lr, li = lre_r[0], lim_r[0]
    ar, ai = lr * dt, li * dt
    lbr, lbi = powers(ar, ai, 1.0)
    den = lr * lr + li * li
    cfr = ((lbr - 1.0) * lr + lbi * li) / den
    cfi = (lbi * lr - (lbr - 1.0) * li) / den
    bbr = cfr * bt_re[0] - cfi * bt_im[0]
    bbi = cfr * bt_im[0] + cfi * bt_re[0]

    chunk = S5_CHUNK
    s_col = lax.broadcasted_iota(jnp.int32, (chunk, two_p), 0).astype(F32)
    first = lax.broadcasted_iota(jnp.int32, (chunk, two_p), 1) < SSM_STATE
    rev_r, rev_i = powers(ar, ai, (chunk - 1.0) - s_col)

    for hq in range(SSM_GROUP):
        br, bi = bbr[hq:hq + 1, :], bbi[hq:hq + 1, :]
        w = jnp.where(first, rev_r * br - rev_i * bi, rev_r * bi + rev_i * br)
        w_ref[0, hq * chunk:(hq + 1) * chunk, :] = w.astype(BF16)

    lrc, lic = lre_c[0], lim_c[0]
    arc, aic = lrc * dt, lic * dt
    t_row = lax.broadcasted_iota(jnp.int32, (two_p, chunk), 1).astype(F32)
    top = lax.broadcasted_iota(jnp.int32, (two_p, chunk), 0) < SSM_STATE
    pw_r, pw_i = powers(arc, aic, t_row)
    er, ei = powers(arc, aic, t_row + 1.0)

    for h in range(SSM_GROUP):
        cr, ci = ct_re[0][:, h:h + 1], ct_im[0][:, h:h + 1]
        m = jnp.where(top, er * cr - ei * ci, -(er * ci + ei * cr))
        m_ref[0, :, h * chunk:(h + 1) * chunk] = m.astype(BF16)

    first16 = lax.broadcasted_iota(jnp.int32, (SSM_GROUP, two_p), 1) < SSM_STATE
    crr, cir = cr_r[0], ci_r[0]
    g2 = jnp.concatenate(
        [jnp.where(first16, crr * bbr[hq:hq + 1, :] - cir * bbi[hq:hq + 1, :],
                   -(crr * bbi[hq:hq + 1, :] + cir * bbr[hq:hq + 1, :])) for hq in range(SSM_GROUP)],
        axis=0)
    taps = _dot(g2, jnp.where(top, pw_r, pw_i), _HI)

    causal = (lax.broadcasted_iota(jnp.int32, (chunk, chunk), 0)
              <= lax.broadcasted_iota(jnp.int32, (chunk, chunk), 1))
    for hq in range(SSM_GROUP):
        for h in range(SSM_GROUP):
            row = hq * SSM_GROUP + h
            k_rows = jnp.broadcast_to(taps[row:row + 1, :], (chunk, chunk))
            toep = pltpu.roll(k_rows, 0, axis=1, stride=1, stride_axis=0)
            t_ref[0, hq * chunk:(hq + 1) * chunk, h * chunk:(h + 1) * chunk] = (
                jnp.where(causal, toep, 0.0).astype(BF16))

    pr, pi_ = lbr, lbi
    for _ in range(S5_CHUNK.bit_length() - 1):
        pr, pi_ = pr * pr - pi_ * pi_, 2.0 * pr * pi_
    a_ref[0, 0:1, :] = pr
    a_ref[0, 1:2, :] = pi_


def _s5_params(lre, lim, ldt, b_re, b_im, c_re, c_im):
    g, p = lre.shape
    dup = lambda a: jnp.concatenate([a, a], axis=-1)
    lre_r, lim_r = dup(lre)[:, None, :], dup(lim)[:, None, :]
    lre_c, lim_c = dup(lre)[:, :, None], dup(lim)[:, :, None]
    bt_re = dup(jnp.swapaxes(b_re, 1, 2))
    bt_im = dup(jnp.swapaxes(b_im, 1, 2))
    ct = lambda c: jnp.concatenate([jnp.swapaxes(c, 1, 2)] * 2, axis=1)
    ct_re, ct_im = ct(c_re), ct(c_im)
    ldt3 = ldt[:, None, None]
    return (lre_r, lim_r, lre_c, lim_c, ldt3, bt_re, bt_im, ct_re, ct_im, dup(c_re), dup(c_im))


N_S5_PARAMS = 11


def _s5_kernel(cpb, nseq, *refs):
    params = refs[:N_S5_PARAMS]
    u_refs = refs[N_S5_PARAMS:N_S5_PARAMS + nseq]
    d_ref = refs[N_S5_PARAMS + nseq]
    y_refs = refs[N_S5_PARAMS + nseq + 1:N_S5_PARAMS + 2 * nseq + 1]
    t_scr, w_scr, m_scr, a_scr = refs[N_S5_PARAMS + 2 * nseq + 1:]
    _s5_param_kernel(*params, t_scr, w_scr, m_scr, a_scr)
    _s5_scan_kernel(cpb, u_refs, t_scr, w_scr, m_scr, a_scr, d_ref, y_refs)


def _s5_scan_kernel(cpb, u_refs, t_ref, w_ref, m_ref, a_ref, d_ref, y_refs):
    us = [jnp.concatenate([u_ref[:, hq, :] for u_ref in u_refs], axis=0)
          for hq in range(SSM_GROUP)]
    ub = jnp.concatenate([u.astype(BF16) for u in us], axis=1)
    rows = ub.shape[0]
    two_p = 2 * SSM_STATE
    h = _dot(ub, w_ref[0])
    ar = a_ref[0, 0:1, :]
    ai = a_ref[0, 1:2, :]
    lane = lax.broadcasted_iota(jnp.int32, (1, two_p), 1)
    sign = jnp.where(lane < SSM_STATE, -1.0, 1.0)
    chunk = lax.broadcasted_iota(jnp.int32, (rows, two_p), 0) % cpb
    step = 1
    while step < cpb:
        prev = jnp.where(chunk >= step, pltpu.roll(h, step, axis=0), 0.0)
        swapped = pltpu.roll(prev, SSM_STATE, axis=1)
        h = h + prev * ar + swapped * (ai * sign)
        ar, ai = ar * ar - ai * ai, 2.0 * ar * ai
        step *= 2
    hprev = jnp.where(chunk >= 1, pltpu.roll(h, 1, axis=0), 0.0)
    y = _dot(ub, t_ref[0]) + _dot(hprev.astype(BF16), m_ref[0])
    for hq in range(SSM_GROUP):
        yh = jax.nn.gelu(y[:, hq * S5_CHUNK:(hq + 1) * S5_CHUNK] + d_ref[0, hq:hq + 1, :] * us[hq])
        for b, y_ref in enumerate(y_refs):
            y_ref[:, hq, :] = yh[b * cpb:(b + 1) * cpb]


def _s5(utcs, params, dtab, cpb):
    nseq = len(utcs)
    two_p = 2 * SSM_STATE
    idx = lambda i: (i, 0, 0)
    chan = pl.BlockSpec((cpb, SSM_GROUP, S5_CHUNK), lambda i: (0, i, 0))
    return pl.pallas_call(
        functools.partial(_s5_kernel, cpb, nseq),
        grid=(SSM_GROUPS,),
        in_specs=([pl.BlockSpec((1,) + a.shape[1:], idx) for a in params]
                  + [chan] * nseq + [pl.BlockSpec((1, SSM_GROUP, S5_CHUNK), idx)]),
        out_specs=[chan] * nseq,
        out_shape=[jax.ShapeDtypeStruct(u.shape, F32) for u in utcs],
        scratch_shapes=[pltpu.VMEM((1, S5_COLS, S5_COLS), BF16),
                        pltpu.VMEM((1, S5_COLS, two_p), BF16),
                        pltpu.VMEM((1, two_p, S5_COLS), BF16),
                        pltpu.VMEM((1, 2, two_p), F32)],
        compiler_params=pltpu.CompilerParams(vmem_limit_bytes=VMEM_LIMIT),
        name="s5_scan",
    )(*params, *utcs, dtab)


def _plan_kernel(nb, tpb, cnt_ref, start_ref, tmap_ref, nused_ref):
    cnt = cnt_ref[0, 0]
    padded = jnp.floor((cnt + (MOBA_BLOCK - 1.0)) * (1.0 / MOBA_BLOCK)) * MOBA_BLOCK
    r_io = lax.broadcasted_iota(jnp.int32, (nb, nb), 0)
    c_io = lax.broadcasted_iota(jnp.int32, (nb, nb), 1)
    start = _dot((c_io < r_io).astype(F32), padded, _HI)
    start_ref[0, 0] = start
    end = (start + padded)[:, 0:1]
    tile_row = lax.broadcasted_iota(jnp.int32, (nb, tpb), 1).astype(F32) * MOBA_BLOCK
    blk_of_tile = jnp.sum(jnp.where(end <= tile_row, 1.0, 0.0), axis=0, keepdims=True)
    tmap_ref[0] = jnp.minimum(blk_of_tile, nb - 1.0).astype(jnp.int32)
    total = jnp.max(end, axis=0, keepdims=True)
    nused_ref[0] = jnp.broadcast_to(total * (1.0 / MOBA_BLOCK), (1, HEAD_PAD)).astype(jnp.int32)


def _plan(cnt, nb, tpb):
    bsz = cnt.shape[0]
    nbh = bsz * ATT_HEADS
    return pl.pallas_call(
        functools.partial(_plan_kernel, nb, tpb),
        grid=(bsz, ATT_HEADS),
        in_specs=[pl.BlockSpec((1, 1, nb, MOBA_BLOCK), lambda b, h: (b, h, 0, 0))],
        out_specs=[
            pl.BlockSpec((1, 1, nb, MOBA_BLOCK), lambda b, h: (b, h, 0, 0)),
            pl.BlockSpec((1, 1, tpb), lambda b, h: (b * ATT_HEADS + h, 0, 0)),
            pl.BlockSpec((1, 1, HEAD_PAD), lambda b, h: (b * ATT_HEADS + h, 0, 0)),
        ],
        out_shape=[
            jax.ShapeDtypeStruct((bsz, ATT_HEADS, nb, MOBA_BLOCK), F32),
            jax.ShapeDtypeStruct((nbh, 1, tpb), jnp.int32),
            jax.ShapeDtypeStruct((nbh, 1, HEAD_PAD), jnp.int32),
        ],
        name="route_plan",
    )(cnt)


def _pos_kernel(nb, cap, trash, sel_ref, rank_ref, start_ref, dst_ref):
    b = pl.program_id(0) // nb
    blk = lax.broadcasted_iota(jnp.int32, (nb, MOBA_BLOCK), 0)
    lane = lax.broadcasted_iota(jnp.int32, (1, MOBA_BLOCK), 1)
    for h in range(ATT_HEADS):
        start = start_ref[0, h]
        base = (b * ATT_HEADS + h) * cap
        for r in range(MOBA_TOPK):
            s = sel_ref[r, h:h + 1, :]
            first = jnp.sum(jnp.where(blk == s, start, 0.0), axis=0, keepdims=True).astype(jnp.int32)
            dst = base + first + rank_ref[r, h:h + 1, :]
            dst_ref[r, h:h + 1, :] = jnp.where(s >= 0, dst, trash + lane)


def _pos(sel, rank, start, nb, cap, trash):
    n = sel.shape[-1]
    blk3 = pl.BlockSpec((MOBA_TOPK, ATT_HEADS, MOBA_BLOCK), lambda i: (0, 0, i))
    return pl.pallas_call(
        functools.partial(_pos_kernel, nb, cap, trash),
        grid=(n // MOBA_BLOCK,),
        in_specs=[blk3, blk3,
                  pl.BlockSpec((1, ATT_HEADS, nb, MOBA_BLOCK), lambda i: (i // nb, 0, 0, 0))],
        out_specs=blk3,
        out_shape=jax.ShapeDtypeStruct(sel.shape, jnp.int32),
        name="route_pos",
    )(sel, rank, start)


def _sc_mesh():
    return plsc.VectorSubcoreMesh(core_axis_name="core", subcore_axis_name="subcore")


def _sc_scatter_rows(x, idx, rows_out):
    nrep, nin = idx.shape

    @pl.kernel(out_type=jax.ShapeDtypeStruct((rows_out, HEAD_PAD), x.dtype), mesh=_sc_mesh(),
               scratch_types=[])
    def scatter(x_hbm, i_hbm, o_hbm):
        def body(x_vmem, *i_vmems):
            for i_vmem in i_vmems:
                pltpu.sync_copy(x_vmem, o_hbm.at[i_vmem.at[0]])

        idx_spec = lambda r: pl.BlockSpec((1, SC_WINDOW), lambda i: (r, i))
        pltpu.emit_pipeline(
            body,
            grid=(nin // SC_WINDOW,),
            in_specs=[pl.BlockSpec((SC_WINDOW, HEAD_PAD), lambda i: (i, 0))]
                     + [idx_spec(r) for r in range(nrep)],
            out_specs=[],
            core_axis_name=("core", "subcore"),
            dimension_semantics=(pltpu.PARALLEL,),
        )(x_hbm, *([i_hbm] * nrep))

    return scatter(x, idx)


def _sc_gather_rows(x, idx):
    nout = idx.shape[1]

    @pl.kernel(out_type=jax.ShapeDtypeStruct((nout, HEAD_PAD), x.dtype), mesh=_sc_mesh())
    def gather(x_hbm, i_hbm, o_hbm):
        def body(i_vmem, o_vmem):
            pltpu.sync_copy(x_hbm.at[i_vmem.at[0]], o_vmem)

        pltpu.emit_pipeline(
            body,
            grid=(nout // SC_WINDOW,),
            in_specs=[pl.BlockSpec((1, SC_WINDOW), lambda i: (0, i))],
            out_specs=[pl.BlockSpec((SC_WINDOW, HEAD_PAD), lambda i: (i, 0))],
            core_axis_name=("core", "subcore"),
            dimension_semantics=(pltpu.PARALLEL,),
        )(i_hbm, o_hbm)

    return gather(x, idx)


def _routed_kernel(tmap_ref, nused_ref, qs_ref, slope_ref, k_ref, vt_ref, o_ref):
    h = pl.program_id(1)
    bh = pl.program_id(0) * ATT_HEADS + h
    s = pl.program_id(2)

    @pl.when(s * ROUTED_TILES < nused_ref[bh])
    def _():
        slope = slope_ref[pl.ds(h, 1), :]
        row = lax.broadcasted_iota(jnp.int32, (HEAD_DIM, MOBA_BLOCK), 0)
        tiles = range(ROUTED_TILES)
        js = [tmap_ref[bh, s * ROUTED_TILES + u] for u in tiles]
        ks = [pl.multiple_of(j * MOBA_BLOCK, MOBA_BLOCK) for j in js]
        zs = [_dot_nt(k_ref[pl.ds(ks[u], MOBA_BLOCK), :],
                      qs_ref[u * MOBA_BLOCK:(u + 1) * MOBA_BLOCK, :].astype(BF16)) for u in tiles]
        ms = [jnp.max(z, axis=0, keepdims=True) for z in zs]
        ps = [jnp.exp(zs[u] - ms[u]) for u in tiles]
        ls = [jnp.sum(p, axis=0, keepdims=True) for p in ps]
        ots = [_dot(vt_ref[:, pl.ds(ks[u], MOBA_BLOCK)], ps[u].astype(BF16)) for u in tiles]
        for u in tiles:
            m = ms[u] + slope * jnp.full((1, MOBA_BLOCK), ks[u], jnp.int32).astype(F32)
            stats = jnp.where(row == 0, m, jnp.where(row == 1, ls[u], 0.0))
            o_ref[u * MOBA_BLOCK:(u + 1) * MOBA_BLOCK, :] = jnp.concatenate([ots[u], stats], axis=0).T


def _routed(qs, ka, vt, slope_tab, tmap, nused, bsz, seq, tpb):
    steps = tpb // ROUTED_TILES
    rows = ROUTED_TILES * MOBA_BLOCK
    shift = ROUTED_TILES.bit_length() - 1

    def step_of(b, h, s, tm, nu):
        bh = b * ATT_HEADS + h
        used = lax.shift_right_logical(nu[bh] + (ROUTED_TILES - 1), shift)
        return bh * steps + jnp.minimum(s, jnp.maximum(used - 1, 0)), 0

    grid_spec = pltpu.PrefetchScalarGridSpec(
        num_scalar_prefetch=2,
        grid=(bsz, ATT_HEADS, steps),
        in_specs=[
            pl.BlockSpec((rows, HEAD_PAD), step_of),
            pl.BlockSpec(slope_tab.shape, lambda b, h, s, tm, nu: (0, 0)),
            pl.BlockSpec((seq, HEAD_PAD), lambda b, h, s, tm, nu: (b, h)),
            pl.BlockSpec((HEAD_DIM, seq), lambda b, h, s, tm, nu: (h, b)),
        ],
        out_specs=pl.BlockSpec((rows, HEAD_PAD), step_of),
    )
    return pl.pallas_call(
        _routed_kernel,
        grid_spec=grid_spec,
        out_shape=jax.ShapeDtypeStruct(qs.shape, F32),
        compiler_params=pltpu.CompilerParams(vmem_limit_bytes=VMEM_LIMIT),
        name="moba_routed",
    )(tmap, nused, qs, slope_tab, ka, vt)


def _combine_kernel(nb, q_ref, k_ref, vt_ref, sel_ref, g_ref, slope_ref, o_ref):
    il = pl.program_id(0) % nb
    kio = lax.broadcasted_iota(jnp.int32, (MOBA_BLOCK, MOBA_BLOCK), 0)
    qio = lax.broadcasted_iota(jnp.int32, (MOBA_BLOCK, MOBA_BLOCK), 1)
    causal = kio <= qio
    own_shift = jnp.full((1, MOBA_BLOCK), il * MOBA_BLOCK, jnp.int32).astype(F32)
    outs = []
    heads = range(ATT_HEADS)
    zs = [_dot_nt(k_ref[:, h * HEAD_PAD:(h + 1) * HEAD_PAD], q_ref[h].astype(BF16)) for h in heads]
    zs = [jnp.where(causal, z, NEG_BIG) for z in zs]
    ms = [jnp.max(z, axis=0, keepdims=True) for z in zs]
    ps = [jnp.exp(zs[h] - ms[h]) for h in heads]
    ls = [jnp.sum(p, axis=0, keepdims=True) for p in ps]
    os_ = [_dot(vt_ref[h * HEAD_DIM:(h + 1) * HEAD_DIM, :], ps[h].astype(BF16)) for h in heads]
    for h in heads:
        l0, o0 = ls[h], os_[h]
        m0 = ms[h] + slope_ref[h:h + 1, :] * own_shift
        parts = []
        for r in range(MOBA_TOPK):
            gt = g_ref[r, h].T
            valid = sel_ref[r, h:h + 1, :] >= 0
            parts.append((jnp.where(valid, gt[HEAD_DIM:HEAD_DIM + 1, :], NEG_BIG),
                          jnp.where(valid, gt[HEAD_DIM + 1:HEAD_DIM + 2, :], 0.0),
                          jnp.where(valid, gt[:HEAD_DIM, :], 0.0)))
        m = m0
        for mr, _, _ in parts:
            m = jnp.maximum(m, mr)
        w = jnp.exp(m0 - m)
        num, den = w * o0, w * l0
        for mr, lr, orr in parts:
            w = jnp.exp(mr - m)
            num, den = num + w * orr, den + w * lr
        outs.append(num / den)
    o_ref[...] = jnp.concatenate(outs, axis=0).T.astype(BF16)


def _combine(qhm, ka, vt, sel, g4, slope_tab, nb):
    n = ka.shape[0]
    return pl.pallas_call(
        functools.partial(_combine_kernel, nb),
        grid=(n // MOBA_BLOCK,),
        in_specs=[
            pl.BlockSpec((ATT_HEADS, MOBA_BLOCK, HEAD_PAD), lambda i: (0, i, 0)),
            pl.BlockSpec((MOBA_BLOCK, QK_PAD), lambda i: (i, 0)),
            pl.BlockSpec((ATT_WIDTH, MOBA_BLOCK), lambda i: (0, i)),
            pl.BlockSpec((MOBA_TOPK, ATT_HEADS, MOBA_BLOCK), lambda i: (0, 0, i)),
            pl.BlockSpec((MOBA_TOPK, ATT_HEADS, MOBA_BLOCK, HEAD_PAD), lambda i: (0, 0, i, 0)),
            pl.BlockSpec(slope_tab.shape, lambda i: (0, 0)),
        ],
        out_specs=pl.BlockSpec((MOBA_BLOCK, ATT_WIDTH), lambda i: (i, 0)),
        out_shape=jax.ShapeDtypeStruct((n, ATT_WIDTH), BF16),
        compiler_params=pltpu.CompilerParams(vmem_limit_bytes=VMEM_LIMIT),
        name="moba_own_combine",
    )(qhm, ka, vt, sel, g4, slope_tab)


def _merge_kernel(x_ref, bra_ref, ys_ref, brc_ref, wg_ref, wa_ref, wb_ref, wc_ref, gw_ref, gb_ref,
                  wo_ref, lg_ref, lb_ref, o_ref):
    x = x_ref[...]
    xb = x.astype(BF16)

    def gate(k):
        return jax.nn.sigmoid(_dot(xb, wg_ref[:, k * D_MODEL:(k + 1) * D_MODEL]))

    merged = gate(0) * _dot(bra_ref[...], wa_ref[...])
    ys = jnp.concatenate([ys_ref[c].T for c in range(ys_ref.shape[0])], axis=0)
    brb = ys * jax.nn.sigmoid(_dot(ys.astype(BF16), gw_ref[...]) + gb_ref[...])
    merged = merged + gate(1) * _dot(brb.astype(BF16), wb_ref[...])
    merged = merged + gate(2) * _dot(brc_ref[...], wc_ref[...])
    mix = _dot(merged.astype(BF16), wo_ref[...])
    o_ref[...] = _layer_norm(DN_ALPHA * x + mix, lg_ref[...], lb_ref[...])


def _merge(x2, bra, ys, brc, wg, wa, wb, wc, gw, gb, wo, lg, lb, tm, first_tile):
    n = bra.shape[0]
    const = lambda i: (0, 0)
    tile = lambda w: pl.BlockSpec((tm, w), lambda i: (i, 0))
    full = lambda a: pl.BlockSpec(a.shape, const)
    return pl.pallas_call(
        _merge_kernel,
        grid=(n // tm,),
        in_specs=[pl.BlockSpec((tm, D_MODEL), lambda i: (first_tile + i, 0)), tile(SGU_WIDTH),
                  pl.BlockSpec((tm // S5_CHUNK, SSM_WIDTH, S5_CHUNK), lambda i: (i, 0, 0)),
                  tile(ATT_WIDTH),
                  full(wg), full(wa), full(wb), full(wc), full(gw), full(gb), full(wo), full(lg), full(lb)],
        out_specs=tile(D_MODEL),
        out_shape=jax.ShapeDtypeStruct((n, D_MODEL), F32),
        compiler_params=pltpu.CompilerParams(vmem_limit_bytes=VMEM_LIMIT),
        name="merge_ln",
    )(x2, bra, ys, brc, wg, wa, wb, wc, gw, gb, wo, lg, lb)


FF_CHUNK = D_FF // 2


def _ffn_kernel(x_ref, w1_ref, w3_ref, w2_ref, lg_ref, lb_ref, *rest):
    o_ref = rest[-1]
    x = x_ref[...]
    xb = x.astype(BF16)
    acc = None
    for c in range(D_FF // FF_CHUNK):
        cs = slice(c * FF_CHUNK, (c + 1) * FF_CHUNK)
        h = (jax.nn.silu(_dot(xb, w1_ref[:, cs])) * _dot(xb, w3_ref[:, cs])).astype(BF16)
        part = _dot(h, w2_ref[cs, :])
        acc = part if acc is None else acc + part
    o_ref[...] = _layer_norm(DN_ALPHA * x + acc, lg_ref[...], lb_ref[...])


def _ffn(x2, w1, w3, w2, lg, lb, tm, out_rows, first_tile, dest):
    n = x2.shape[0]
    const = lambda i: (0, 0)
    full = lambda a: pl.BlockSpec(a.shape, const)
    in_specs = [pl.BlockSpec((tm, D_MODEL), lambda i: (i, 0)),
                full(w1), full(w3), full(w2), full(lg), full(lb)]
    args = [x2, w1, w3, w2, lg, lb]
    aliases = {}
    if dest is not None:
        in_specs.append(pl.BlockSpec(memory_space=pl.ANY))
        aliases = {len(args): 0}
        args.append(dest)
    return pl.pallas_call(
        _ffn_kernel,
        grid=(n // tm,),
        in_specs=in_specs,
        out_specs=pl.BlockSpec((tm, D_MODEL), lambda i: (first_tile + i, 0)),
        out_shape=jax.ShapeDtypeStruct((out_rows, D_MODEL), F32),
        input_output_aliases=aliases,
        compiler_params=pltpu.CompilerParams(vmem_limit_bytes=VMEM_LIMIT),
        name="ffn_ln",
    )(*args)


def _alibi_extras():
    slopes = 2.0 ** (-8.0 * jnp.arange(1, ATT_HEADS + 1, dtype=F32) / ATT_HEADS)
    row = jnp.arange(MOBA_BLOCK, dtype=F32)
    eq = jnp.zeros((MOBA_BLOCK, ATT_HEADS, HEAD_PAD), F32)
    eq = eq.at[:, :, Q_LANE_ONE].set(1.0).at[:, :, Q_LANE_ROW].set(row[:, None])
    ek = jnp.zeros((MOBA_BLOCK, ATT_HEADS, HEAD_PAD), F32)
    ek = ek.at[:, :, Q_LANE_ONE].set(row[:, None] * slopes[None, :])
    ek = ek.at[:, :, Q_LANE_BLK].set(-slopes[None, :] * MOBA_BLOCK).at[:, :, Q_LANE_ROW].set(-slopes[None, :])
    slope_tab = jnp.broadcast_to(slopes[:, None], (ATT_HEADS, MOBA_BLOCK))
    return eq.reshape(MOBA_BLOCK, QK_PAD), ek.reshape(MOBA_BLOCK, QK_PAD), slope_tab


def kernel(x, w_in, sgu_ln_g, sgu_ln_b, sgu_w, sgu_b, ssm_lambda_re, ssm_lambda_im, ssm_log_dt,
           ssm_b_re, ssm_b_im, ssm_c_re, ssm_c_im, ssm_d, glu_w, glu_b, w_branch_a, w_branch_b,
           w_branch_c, w_out, ln1_g, ln1_b, ffn_w1, ffn_w3, ffn_w2, ln2_g, ln2_b):
    bsz, seq, _ = x.shape
    n = bsz * seq
    nb = seq // MOBA_BLOCK
    cpb = seq // S5_CHUNK
    tm = 512 if n % 512 == 0 else MOBA_BLOCK
    eq, ek, slope_tab = _alibi_extras()
    scale = HEAD_DIM ** -0.5
    cap = (MOBA_TOPK + 1) * seq
    tpb = cap // MOBA_BLOCK
    trash = ATT_HEADS * cap
    o_q = 2 * SGU_WIDTH + SSM_WIDTH
    o_g = o_q + 3 * ATT_WIDTH

    xs = [(x.reshape(n, D_MODEL), b * nb) for b in range(bsz)]
    out = None
    for l in range(DEPTH):
        last = l == DEPTH - 1
        wl = w_in[l]
        wz = wl[:, :2 * SGU_WIDTH].astype(BF16)
        wst = wl[:, 2 * SGU_WIDTH:o_q].T.astype(BF16)
        wq = (wl[:, o_q:o_q + ATT_WIDTH] * scale).astype(BF16)
        wk = wl[:, o_q + ATT_WIDTH:o_q + 2 * ATT_WIDTH].astype(BF16)
        wvt = wl[:, o_q + 2 * ATT_WIDTH:o_g].T.astype(BF16)
        wg = wl[:, o_g:].astype(BF16)

        proj = [_inproj(xs[b][0], wz, wst, wq, wk, wvt, eq, ek, sgu_ln_g[l][None, :],
                        sgu_ln_b[l][None, :], sgu_w[l].astype(BF16), sgu_b[l].T, nb, xs[b][1])
                for b in range(bsz)]

        s5_params = _s5_params(
            ssm_lambda_re[l], ssm_lambda_im[l], ssm_log_dt[l], ssm_b_re[l], ssm_b_im[l],
            ssm_c_re[l], ssm_c_im[l])
        dtab = jnp.broadcast_to(ssm_d[l].reshape(SSM_GROUPS, SSM_GROUP, 1),
                                (SSM_GROUPS, SSM_GROUP, S5_CHUNK))
        yss = _s5([p[1] for p in proj], s5_params, dtab, cpb)

        merge_w = (wg, w_branch_a[l].astype(BF16), w_branch_b[l].astype(BF16), w_branch_c[l].astype(BF16),
                   glu_w[l].astype(BF16), glu_b[l][None, :], w_out[l].astype(BF16),
                   ln1_g[l][None, :], ln1_b[l][None, :])
        ffn_w = (ffn_w1[l].astype(BF16), ffn_w3[l].astype(BF16), ffn_w2[l].astype(BF16),
                 ln2_g[l][None, :], ln2_b[l][None, :])
        for b in range(bsz):
            bra, _, qhm, ka, vt, sel, rank, cnt = proj[b]
            start, tmap, nused = _plan(cnt, nb, tpb)
            dst = _pos(sel[0], rank[0], start, nb, cap, trash).reshape(MOBA_TOPK, ATT_HEADS * seq)
            qs = _sc_scatter_rows(qhm.reshape(ATT_HEADS * seq, HEAD_PAD), dst, trash + MOBA_BLOCK)
            part = _routed(qs, ka, vt[0], slope_tab, tmap[:, 0, :], nused[:, 0, 0], 1, seq, tpb)
            g = _sc_gather_rows(part, dst.reshape(1, MOBA_TOPK * ATT_HEADS * seq))
            brc = _combine(qhm[0], ka, vt[0], sel[0],
                           g.reshape(MOBA_TOPK, ATT_HEADS, seq, HEAD_PAD), slope_tab, nb)
            x1 = _merge(xs[b][0], bra, yss[b], brc, *merge_w, tm, xs[b][1] * MOBA_BLOCK // tm)
            if last:
                out = _ffn(x1, *ffn_w, tm, n, b * (seq // tm), out)
            else:
                xs[b] = (_ffn(x1, *ffn_w, tm, seq, 0, None), 0)
    return out.reshape(bsz, seq, D_MODEL)
```

```python
import functools

import jax
import jax.numpy as jnp
from jax import lax
from jax.experimental import pallas as pl
from jax.experimental.pallas import tpu as pltpu
from jax.experimental.pallas import tpu_sc as plsc

F32 = jnp.float32
BF16 = jnp.bfloat16

D_MODEL = 1024
SGU_CHUNK = 128
SGU_GROUPS = 4
SGU_WIDTH = 512
SSM_WIDTH = 512
SSM_GROUP = 16
SSM_GROUPS = 32
SSM_STATE = 64
ATT_HEADS = 8
HEAD_DIM = 64
ATT_WIDTH = 512
MOBA_BLOCK = 256
MOBA_TOPK = 3
D_FF = 2816
DEPTH = 2
DN_ALPHA = (2 * DEPTH) ** 0.25
LN_EPS = 1e-5
NEG_BIG = -1e30

HEAD_PAD = 128
QK_PAD = ATT_HEADS * HEAD_PAD
Q_LANE_ONE = HEAD_DIM
Q_LANE_BLK = HEAD_DIM + 1
Q_LANE_ROW = HEAD_DIM + 2
SC_WINDOW = 128
ROUTED_TILES = 32
S5_CHUNK = 128
S5_COLS = S5_CHUNK * SSM_GROUP
VMEM_LIMIT = 56 * 1024 * 1024

_HI = lax.Precision.HIGHEST


def _dot(a, b, precision=None):
    return jnp.dot(a, b, preferred_element_type=F32, precision=precision)


def _dot_nt(a, b, precision=None):
    return lax.dot_general(a, b, (((1,), (1,)), ((), ())),
                           preferred_element_type=F32, precision=precision)


def _layer_norm(x, g, b):
    mu = jnp.mean(x, axis=-1, keepdims=True)
    xc = x - mu
    var = jnp.mean(xc * xc, axis=-1, keepdims=True)
    return xc * lax.rsqrt(var + LN_EPS) * g + b


def _inproj_kernel(nb, x_ref, wz_ref, wst_ref, wq_ref, wk_ref, wvt_ref, eq_ref, ek_ref, lng_ref, lnb_ref,
                   sw_ref, sbt_ref, bra_ref, u_ref, q_ref, k_ref, vt_ref, sel_ref, rank_ref, cnt_ref,
                   kmean_ref, carry_ref):
    i = pl.program_id(0)
    il = i % nb

    @pl.when(i == 0)
    def _():
        kmean_ref[...] = jnp.zeros_like(kmean_ref)

    @pl.when(il == 0)
    def _():
        carry_ref[...] = jnp.zeros_like(carry_ref)

    xb = x_ref[...].astype(BF16)

    z = jax.nn.gelu(_dot(xb, wz_ref[...]))
    u = z[:, :SGU_WIDTH]
    vn = _layer_norm(z[:, SGU_WIDTH:], lng_ref[...], lnb_ref[...]).astype(BF16)
    r_io = lax.broadcasted_iota(jnp.int32, (SGU_CHUNK, SGU_CHUNK), 0)
    c_io = lax.broadcasted_iota(jnp.int32, (SGU_CHUNK, SGU_CHUNK), 1)
    tril = r_io >= c_io
    for g in range(SGU_GROUPS):
        w = jnp.where(tril, sw_ref[g], jnp.zeros((), BF16))
        bias = sbt_ref[:, g:g + 1]
        gs = slice(g * SGU_CHUNK, (g + 1) * SGU_CHUNK)
        for c in range(MOBA_BLOCK // SGU_CHUNK):
            rs = slice(c * SGU_CHUNK, (c + 1) * SGU_CHUNK)
            mixed = _dot(w, vn[rs, gs]) + bias
            bra_ref[rs, gs] = (u[rs, gs] * mixed).astype(BF16)

    ut = _dot_nt(wst_ref[...], xb)
    for c in range(MOBA_BLOCK // S5_CHUNK):
        u_ref[c] = ut[:, c * S5_CHUNK:(c + 1) * S5_CHUNK]

    lane = lax.broadcasted_iota(jnp.int32, (1, HEAD_PAD), 1)
    is_head = lane < HEAD_DIM
    blk_lane = jnp.where(lane == Q_LANE_BLK, il.astype(F32), 0.0)
    q2 = _dot(xb, wq_ref[...])
    k2 = _dot(xb, wk_ref[...])
    qh, kmeans = [], []
    for pair in range(ATT_HEADS // 2):
        ps = slice(pair * HEAD_PAD, (pair + 1) * HEAD_PAD)
        q_pair, k_pair = q2[:, ps], k2[:, ps]
        sources = ((q_pair, k_pair), (pltpu.roll(q_pair, HEAD_DIM, axis=1), pltpu.roll(k_pair, HEAD_DIM, axis=1)))
        for odd, (q_src, k_src) in enumerate(sources):
            h = 2 * pair + odd
            hs = slice(h * HEAD_PAD, (h + 1) * HEAD_PAD)
            qh.append(jnp.where(is_head, q_src, eq_ref[:, hs] + blk_lane))
            q_ref[0, h] = qh[h]
            k_ref[:, hs] = jnp.where(is_head, k_src, ek_ref[:, hs]).astype(BF16)
            kmeans.append(jnp.mean(k_src, axis=0, keepdims=True))
    kmean_ref[pl.ds(il, 1), :] = jnp.concatenate(kmeans, axis=1)
    vt_ref[0] = _dot_nt(wvt_ref[...], xb).astype(BF16)

    blk = lax.broadcasted_iota(jnp.int32, (nb, MOBA_BLOCK), 0)
    neg_inf = jnp.full((), -jnp.inf, F32)
    r_io = lax.broadcasted_iota(jnp.int32, (MOBA_BLOCK, MOBA_BLOCK), 0)
    c_io = lax.broadcasted_iota(jnp.int32, (MOBA_BLOCK, MOBA_BLOCK), 1)
    earlier = (r_io < c_io).astype(BF16)
    heads = range(ATT_HEADS)
    past = blk < il
    gates = []
    for h in heads:
        hs = slice(h * HEAD_PAD, h * HEAD_PAD + HEAD_DIM)
        gates.append(jnp.where(past, _dot_nt(kmean_ref[:, hs], qh[h][:, :HEAD_DIM], precision=_HI), neg_inf))
    sels = [[] for _ in heads]
    for r in range(MOBA_TOPK):
        ms = [jnp.max(g, axis=0, keepdims=True) for g in gates]
        idxs = [jnp.min(jnp.where(gates[h] == ms[h], blk, nb), axis=0, keepdims=True) for h in heads]
        for h in heads:
            sels[h].append(jnp.where(r < il, idxs[h], -1))
        gates = [jnp.where(blk == idxs[h], neg_inf, gates[h]) for h in heads]
    hits = [[blk == s for s in sels[h]] for h in heads]
    onehots = [jnp.where(hits[h][0] | hits[h][1] | hits[h][2], 1.0, 0.0) for h in heads]
    befores = [carry_ref[h] + _dot(onehots[h].astype(BF16), earlier) for h in heads]
    for h in heads:
        carry_ref[h] = carry_ref[h] + jnp.sum(onehots[h], axis=1, keepdims=True)
        for r in range(MOBA_TOPK):
            sel_ref[0, r, h:h + 1, :] = sels[h][r]
            rank = jnp.sum(jnp.where(hits[h][r], befores[h], 0.0), axis=0, keepdims=True)
            rank_ref[0, r, h:h + 1, :] = rank.astype(jnp.int32)

    @pl.when(il == nb - 1)
    def _():
        cnt_ref[0] = carry_ref[...]


def _inproj(x2, wz, wst, wq, wk, wvt, eq, ek, lng, lnb, sw, sbt, nb, first_block):
    seq = nb * MOBA_BLOCK
    n, bsz, grid = seq, 1, nb
    const = lambda i: (0, 0)
    return pl.pallas_call(
        functools.partial(_inproj_kernel, nb),
        grid=(grid,),
        in_specs=[
            pl.BlockSpec((MOBA_BLOCK, D_MODEL), lambda i: (first_block + i, 0)),
            pl.BlockSpec(wz.shape, const),
            pl.BlockSpec(wst.shape, const),
            pl.BlockSpec(wq.shape, const),
            pl.BlockSpec(wk.shape, const),
            pl.BlockSpec(wvt.shape, const),
            pl.BlockSpec(eq.shape, const),
            pl.BlockSpec(ek.shape, const),
            pl.BlockSpec(lng.shape, const),
            pl.BlockSpec(lnb.shape, const),
            pl.BlockSpec(sw.shape, lambda i: (0, 0, 0)),
            pl.BlockSpec(sbt.shape, const),
        ],
        out_specs=[
            pl.BlockSpec((MOBA_BLOCK, SGU_WIDTH), lambda i: (i, 0)),
            pl.BlockSpec((MOBA_BLOCK // S5_CHUNK, SSM_WIDTH, S5_CHUNK), lambda i: (i, 0, 0)),
            pl.BlockSpec((1, ATT_HEADS, MOBA_BLOCK, HEAD_PAD), lambda i: (i // nb, 0, i % nb, 0)),
            pl.BlockSpec((MOBA_BLOCK, QK_PAD), lambda i: (i, 0)),
            pl.BlockSpec((1, ATT_WIDTH, MOBA_BLOCK), lambda i: (i // nb, 0, i % nb)),
            pl.BlockSpec((1, MOBA_TOPK, ATT_HEADS, MOBA_BLOCK), lambda i: (i // nb, 0, 0, i % nb)),
            pl.BlockSpec((1, MOBA_TOPK, ATT_HEADS, MOBA_BLOCK), lambda i: (i // nb, 0, 0, i % nb)),
            pl.BlockSpec((1, ATT_HEADS, nb, MOBA_BLOCK), lambda i: (i // nb, 0, 0, 0)),
        ],
        out_shape=[
            jax.ShapeDtypeStruct((n, SGU_WIDTH), BF16),
            jax.ShapeDtypeStruct((n // S5_CHUNK, SSM_WIDTH, S5_CHUNK), F32),
            jax.ShapeDtypeStruct((bsz, ATT_HEADS, seq, HEAD_PAD), F32),
            jax.ShapeDtypeStruct((n, QK_PAD), BF16),
            jax.ShapeDtypeStruct((bsz, ATT_WIDTH, seq), BF16),
            jax.ShapeDtypeStruct((bsz, MOBA_TOPK, ATT_HEADS, seq), jnp.int32),
            jax.ShapeDtypeStruct((bsz, MOBA_TOPK, ATT_HEADS, seq), jnp.int32),
            jax.ShapeDtypeStruct((bsz, ATT_HEADS, nb, MOBA_BLOCK), F32),
        ],
        scratch_shapes=[pltpu.VMEM((nb, QK_PAD), F32),
                        pltpu.VMEM((ATT_HEADS, nb, MOBA_BLOCK), F32)],
        compiler_params=pltpu.CompilerParams(
            dimension_semantics=("arbitrary",), vmem_limit_bytes=VMEM_LIMIT),
        name="inproj_sgu_gate",
    )(x2, wz, wst, wq, wk, wvt, eq, ek, lng, lnb, sw, sbt)


def _s5_param_kernel(lre_r, lim_r, lre_c, lim_c, ldt, bt_re, bt_im, ct_re, ct_im, cr_r, ci_r,
                     t_ref, w_ref, m_ref, a_ref):
    two_p = 2 * SSM_STATE
    dt = jnp.exp(ldt[0])

    def powers(ar, ai, e):
        mag = jnp.exp(ar * e)
        return mag * jnp.cos(ai * e), mag * jnp.sin(ai * e)

    lr, li = lre_r[0], lim_r[0]
    ar, ai = lr * dt, li * dt
    lbr, lbi = powers(ar, ai, 1.0)
    den = lr * lr + li * li
    cfr = ((lbr - 1.0) * lr + lbi * li) / den
    cfi = (lbi * lr - (lbr - 1.0) * li) / den
    bbr = cfr * bt_re[0] - cfi * bt_im[0]
    bbi = cfr * bt_im[0] + cfi * bt_re[0]

    chunk = S5_CHUNK
    s_col = lax.broadcasted_iota(jnp.int32, (chunk, two_p), 0).astype(F32)
    first = lax.broadcasted_iota(jnp.int32, (chunk, two_p), 1) < SSM_STATE
    rev_r, rev_i = powers(ar, ai, (chunk - 1.0) - s_col)

    for hq in range(SSM_GROUP):
        br, bi = bbr[hq:hq + 1, :], bbi[hq:hq + 1, :]
        w = jnp.where(first, rev_r * br - rev_i * bi, rev_r * bi + rev_i * br)
        w_ref[0, hq * chunk:(hq + 1) * chunk, :] = w.astype(BF16)

    lrc, lic = lre_c[0], lim_c[0]
    arc, aic = lrc * dt, lic * dt
    t_row = lax.broadcasted_iota(jnp.int32, (two_p, chunk), 1).astype(F32)
    top = lax.broadcasted_iota(jnp.int32, (two_p, chunk), 0) < SSM_STATE
    pw_r, pw_i = powers(arc, aic, t_row)
    er, ei = powers(arc, aic, t_row + 1.0)

    for h in range(SSM_GROUP):
        cr, ci = ct_re[0][:, h:h + 1], ct_im[0][:, h:h + 1]
        m = jnp.where(top, er * cr - ei * ci, -(er * ci + ei * cr))
        m_ref[0, :, h * chunk:(h + 1) * chunk] = m.astype(BF16)

    first16 = lax.broadcasted_iota(jnp.int32, (SSM_GROUP, two_p), 1) < SSM_STATE
    crr, cir = cr_r[0], ci_r[0]
    g2 = jnp.concatenate(
        [jnp.where(first16, crr * bbr[hq:hq + 1, :] - cir * bbi[hq:hq + 1, :],
                   -(crr * bbi[hq:hq + 1, :] + cir * bbr[hq:hq + 1, :])) for hq in range(SSM_GROUP)],
        axis=0)
    taps = _dot(g2, jnp.where(top, pw_r, pw_i), _HI)

    causal = (lax.broadcasted_iota(jnp.int32, (chunk, chunk), 0)
              <= lax.broadcasted_iota(jnp.int32, (chunk, chunk), 1))
    for hq in range(SSM_GROUP):
        for h in range(SSM_GROUP):
            row = hq * SSM_GROUP + h
            k_rows = jnp.broadcast_to(taps[row:row + 1, :], (chunk, chunk))
            toep = pltpu.roll(k_rows, 0, axis=1, stride=1, stride_axis=0)
            t_ref[0, hq * chunk:(hq + 1) * chunk, h * chunk:(h + 1) * chunk] = (
                jnp.where(causal, toep, 0.0).astype(BF16))

    pr, pi_ = lbr, lbi
    for _ in range(S5_CHUNK.bit_length() - 1):
        pr, pi_ = pr * pr - pi_ * pi_, 2.0 * pr * pi_
    a_ref[0, 0:1, :] = pr
    a_ref[0, 1:2, :] = pi_


def _s5_params(lre, lim, ldt, b_re, b_im, c_re, c_im):
    g, p = lre.shape
    dup = lambda a: jnp.concatenate([a, a], axis=-1)
    lre_r, lim_r = dup(lre)[:, None, :], dup(lim)[:, None, :]
    lre_c, lim_c = dup(lre)[:, :, None], dup(lim)[:, :, None]
    bt_re = dup(jnp.swapaxes(b_re, 1, 2))
    bt_im = dup(jnp.swapaxes(b_im, 1, 2))
    ct = lambda c: jnp.concatenate([jnp.swapaxes(c, 1, 2)] * 2, axis=1)
    ct_re, ct_im = ct(c_re), ct(c_im)
    ldt3 = ldt[:, None, None]
    return (lre_r, lim_r, lre_c, lim_c, ldt3, bt_re, bt_im, ct_re, ct_im, dup(c_re), dup(c_im))


N_S5_PARAMS = 11


def _s5_kernel(cpb, nseq, *refs):
    params = refs[:N_S5_PARAMS]
    u_refs = refs[N_S5_PARAMS:N_S5_PARAMS + nseq]
    d_ref = refs[N_S5_PARAMS + nseq]
    y_refs = refs[N_S5_PARAMS + nseq + 1:N_S5_PARAMS + 2 * nseq + 1]
    t_scr, w_scr, m_scr, a_scr = refs[N_S5_PARAMS + 2 * nseq + 1:]
    _s5_param_kernel(*params, t_scr, w_scr, m_scr, a_scr)
    _s5_scan_kernel(cpb, u_refs, t_scr, w_scr, m_scr, a_scr, d_ref, y_refs)


def _s5_scan_kernel(cpb, u_refs, t_ref, w_ref, m_ref, a_ref, d_ref, y_refs):
    us = [jnp.concatenate([u_ref[:, hq, :] for u_ref in u_refs], axis=0)
          for hq in range(SSM_GROUP)]
    ub = jnp.concatenate([u.astype(BF16) for u in us], axis=1)
    rows = ub.shape[0]
    two_p = 2 * SSM_STATE
    h = _dot(ub, w_ref[0])
    ar = a_ref[0, 0:1, :]
    ai = a_ref[0, 1:2, :]
    lane = lax.broadcasted_iota(jnp.int32, (1, two_p), 1)
    sign = jnp.where(lane < SSM_STATE, -1.0, 1.0)
    chunk = lax.broadcasted_iota(jnp.int32, (rows, two_p), 0) % cpb
    step = 1
    while step < cpb:
        prev = jnp.where(chunk >= step, pltpu.roll(h, step, axis=0), 0.0)
        swapped = pltpu.roll(prev, SSM_STATE, axis=1)
        h = h + prev * ar + swapped * (ai * sign)
        ar, ai = ar * ar - ai * ai, 2.0 * ar * ai
        step *= 2
    hprev = jnp.where(chunk >= 1, pltpu.roll(h, 1, axis=0), 0.0)
    y = _dot(ub, t_ref[0]) + _dot(hprev.astype(BF16), m_ref[0])
    for hq in range(SSM_GROUP):
        yh = jax.nn.gelu(y[:, hq * S5_CHUNK:(hq + 1) * S5_CHUNK] + d_ref[0, hq:hq + 1, :] * us[hq])
        for b, y_ref in enumerate(y_refs):
            y_ref[:, hq, :] = yh[b * cpb:(b + 1) * cpb]


def _s5(utcs, params, dtab, cpb):
    nseq = len(utcs)
    two_p = 2 * SSM_STATE
    idx = lambda i: (i, 0, 0)
    chan = pl.BlockSpec((cpb, SSM_GROUP, S5_CHUNK), lambda i: (0, i, 0))
    return pl.pallas_call(
        functools.partial(_s5_kernel, cpb, nseq),
        grid=(SSM_GROUPS,),
        in_specs=([pl.BlockSpec((1,) + a.shape[1:], idx) for a in params]
                  + [chan] * nseq + [pl.BlockSpec((1, SSM_GROUP, S5_CHUNK), idx)]),
        out_specs=[chan] * nseq,
        out_shape=[jax.ShapeDtypeStruct(u.shape, F32) for u in utcs],
        scratch_shapes=[pltpu.VMEM((1, S5_COLS, S5_COLS), BF16),
                        pltpu.VMEM((1, S5_COLS, two_p), BF16),
                        pltpu.VMEM((1, two_p, S5_COLS), BF16),
                        pltpu.VMEM((1, 2, two_p), F32)],
        compiler_params=pltpu.CompilerParams(vmem_limit_bytes=VMEM_LIMIT),
        name="s5_scan",
    )(*params, *utcs, dtab)


def _plan_kernel(nb, tpb, cnt_ref, start_ref, tmap_ref, nused_ref):
    cnt = cnt_ref[0, 0]
    padded = jnp.floor((cnt + (MOBA_BLOCK - 1.0)) * (1.0 / MOBA_BLOCK)) * MOBA_BLOCK
    r_io = lax.broadcasted_iota(jnp.int32, (nb, nb), 0)
    c_io = lax.broadcasted_iota(jnp.int32, (nb, nb), 1)
    start = _dot((c_io < r_io).astype(F32), padded, _HI)
    start_ref[0, 0] = start
    end = (start + padded)[:, 0:1]
    tile_row = lax.broadcasted_iota(jnp.int32, (nb, tpb), 1).astype(F32) * MOBA_BLOCK
    blk_of_tile = jnp.sum(jnp.where(end <= tile_row, 1.0, 0.0), axis=0, keepdims=True)
    tmap_ref[0] = jnp.minimum(blk_of_tile, nb - 1.0).astype(jnp.int32)
    total = jnp.max(end, axis=0, keepdims=True)
    nused_ref[0] = jnp.broadcast_to(total * (1.0 / MOBA_BLOCK), (1, HEAD_PAD)).astype(jnp.int32)


def _plan(cnt, nb, tpb):
    bsz = cnt.shape[0]
    nbh = bsz * ATT_HEADS
    return pl.pallas_call(
        functools.partial(_plan_kernel, nb, tpb),
        grid=(bsz, ATT_HEADS),
        in_specs=[pl.BlockSpec((1, 1, nb, MOBA_BLOCK), lambda b, h: (b, h, 0, 0))],
        out_specs=[
            pl.BlockSpec((1, 1, nb, MOBA_BLOCK), lambda b, h: (b, h, 0, 0)),
            pl.BlockSpec((1, 1, tpb), lambda b, h: (b * ATT_HEADS + h, 0, 0)),
            pl.BlockSpec((1, 1, HEAD_PAD), lambda b, h: (b * ATT_HEADS + h, 0, 0)),
        ],
        out_shape=[
            jax.ShapeDtypeStruct((bsz, ATT_HEADS, nb, MOBA_BLOCK), F32),
            jax.ShapeDtypeStruct((nbh, 1, tpb), jnp.int32),
            jax.ShapeDtypeStruct((nbh, 1, HEAD_PAD), jnp.int32),
        ],
        name="route_plan",
    )(cnt)


def _pos_kernel(nb, cap, trash, sel_ref, rank_ref, start_ref, dst_ref):
    b = pl.program_id(0) // nb
    blk = lax.broadcasted_iota(jnp.int32, (nb, MOBA_BLOCK), 0)
    lane = lax.broadcasted_iota(jnp.int32, (1, MOBA_BLOCK), 1)
    for h in range(ATT_HEADS):
        start = start_ref[0, h]
        base = (b * ATT_HEADS + h) * cap
        for r in range(MOBA_TOPK):
            s = sel_ref[r, h:h + 1, :]
            first = jnp.sum(jnp.where(blk == s, start, 0.0), axis=0, keepdims=True).astype(jnp.int32)
            dst = base + first + rank_ref[r, h:h + 1, :]
            dst_ref[r, h:h + 1, :] = jnp.where(s >= 0, dst, trash + lane)


def _pos(sel, rank, start, nb, cap, trash):
    n = sel.shape[-1]
    blk3 = pl.BlockSpec((MOBA_TOPK, ATT_HEADS, MOBA_BLOCK), lambda i: (0, 0, i))
    return pl.pallas_call(
        functools.partial(_pos_kernel, nb, cap, trash),
        grid=(n // MOBA_BLOCK,),
        in_specs=[blk3, blk3,
                  pl.BlockSpec((1, ATT_HEADS, nb, MOBA_BLOCK), lambda i: (i // nb, 0, 0, 0))],
        out_specs=blk3,
        out_shape=jax.ShapeDtypeStruct(sel.shape, jnp.int32),
        name="route_pos",
    )(sel, rank, start)


def _sc_mesh():
    return plsc.VectorSubcoreMesh(core_axis_name="core", subcore_axis_name="subcore")


def _sc_scatter_rows(x, idx, rows_out):
    nrep, nin = idx.shape

    @pl.kernel(out_type=jax.ShapeDtypeStruct((rows_out, HEAD_PAD), x.dtype), mesh=_sc_mesh(),
               scratch_types=[])
    def scatter(x_hbm, i_hbm, o_hbm):
        def body(x_vmem, *i_vmems):
            for i_vmem in i_vmems:
                pltpu.sync_copy(x_vmem, o_hbm.at[i_vmem.at[0]])

        idx_spec = lambda r: pl.BlockSpec((1, SC_WINDOW), lambda i: (r, i))
        pltpu.emit_pipeline(
            body,
            grid=(nin // SC_WINDOW,),
            in_specs=[pl.BlockSpec((SC_WINDOW, HEAD_PAD), lambda i: (i, 0))]
                     + [idx_spec(r) for r in range(nrep)],
            out_specs=[],
            core_axis_name=("core", "subcore"),
            dimension_semantics=(pltpu.PARALLEL,),
        )(x_hbm, *([i_hbm] * nrep))

    return scatter(x, idx)


def _sc_gather_rows(x, idx):
    nout = idx.shape[1]

    @pl.kernel(out_type=jax.ShapeDtypeStruct((nout, HEAD_PAD), x.dtype), mesh=_sc_mesh())
    def gather(x_hbm, i_hbm, o_hbm):
        def body(i_vmem, o_vmem):
            pltpu.sync_copy(x_hbm.at[i_vmem.at[0]], o_vmem)

        pltpu.emit_pipeline(
            body,
            grid=(nout // SC_WINDOW,),
            in_specs=[pl.BlockSpec((1, SC_WINDOW), lambda i: (0, i))],
            out_specs=[pl.BlockSpec((SC_WINDOW, HEAD_PAD), lambda i: (i, 0))],
            core_axis_name=("core", "subcore"),
            dimension_semantics=(pltpu.PARALLEL,),
        )(i_hbm, o_hbm)

    return gather(x, idx)


def _routed_kernel(tmap_ref, nused_ref, qs_ref, slope_ref, k_ref, vt_ref, o_ref):
    h = pl.program_id(1)
    bh = pl.program_id(0) * ATT_HEADS + h
    s = pl.program_id(2)

    @pl.when(s * ROUTED_TILES < nused_ref[bh])
    def _():
        slope = slope_ref[pl.ds(h, 1), :]
        row = lax.broadcasted_iota(jnp.int32, (HEAD_DIM, MOBA_BLOCK), 0)
        tiles = range(ROUTED_TILES)
        js = [tmap_ref[bh, s * ROUTED_TILES + u] for u in tiles]
        ks = [pl.multiple_of(j * MOBA_BLOCK, MOBA_BLOCK) for j in js]
        zs = [_dot_nt(k_ref[pl.ds(ks[u], MOBA_BLOCK), :],
                      qs_ref[u * MOBA_BLOCK:(u + 1) * MOBA_BLOCK, :].astype(BF16)) for u in tiles]
        ms = [jnp.max(z, axis=0, keepdims=True) for z in zs]
        ps = [jnp.exp(zs[u] - ms[u]) for u in tiles]
        ls = [jnp.sum(p, axis=0, keepdims=True) for p in ps]
        ots = [_dot(vt_ref[:, pl.ds(ks[u], MOBA_BLOCK)], ps[u].astype(BF16)) for u in tiles]
        for u in tiles:
            m = ms[u] + slope * jnp.full((1, MOBA_BLOCK), ks[u], jnp.int32).astype(F32)
            stats = jnp.where(row == 0, m, jnp.where(row == 1, ls[u], 0.0))
            o_ref[u * MOBA_BLOCK:(u + 1) * MOBA_BLOCK, :] = jnp.concatenate([ots[u], stats], axis=0).T


def _routed(qs, ka, vt, slope_tab, tmap, nused, bsz, seq, tpb):
    steps = tpb // ROUTED_TILES
    rows = ROUTED_TILES * MOBA_BLOCK
    shift = ROUTED_TILES.bit_length() - 1

    def step_of(b, h, s, tm, nu):
        bh = b * ATT_HEADS + h
        used = lax.shift_right_logical(nu[bh] + (ROUTED_TILES - 1), shift)
        return bh * steps + jnp.minimum(s, jnp.maximum(used - 1, 0)), 0

    grid_spec = pltpu.PrefetchScalarGridSpec(
        num_scalar_prefetch=2,
        grid=(bsz, ATT_HEADS, steps),
        in_specs=[
            pl.BlockSpec((rows, HEAD_PAD), step_of),
            pl.BlockSpec(slope_tab.shape, lambda b, h, s, tm, nu: (0, 0)),
            pl.BlockSpec((seq, HEAD_PAD), lambda b, h, s, tm, nu: (b, h)),
            pl.BlockSpec((HEAD_DIM, seq), lambda b, h, s, tm, nu: (h, b)),
        ],
        out_specs=pl.BlockSpec((rows, HEAD_PAD), step_of),
    )
    return pl.pallas_call(
        _routed_kernel,
        grid_spec=grid_spec,
        out_shape=jax.ShapeDtypeStruct(qs.shape, F32),
        compiler_params=pltpu.CompilerParams(vmem_limit_bytes=VMEM_LIMIT),
        name="moba_routed",
    )(tmap, nused, qs, slope_tab, ka, vt)


def _combine_kernel(nb, q_ref, k_ref, vt_ref, sel_ref, g_ref, slope_ref, o_ref):
    il = pl.program_id(0) % nb
    kio = lax.broadcasted_iota(jnp.int32, (MOBA_BLOCK, MOBA_BLOCK), 0)
    qio = lax.broadcasted_iota(jnp.int32, (MOBA_BLOCK, MOBA_BLOCK), 1)
    causal = kio <= qio
    own_shift = jnp.full((1, MOBA_BLOCK), il * MOBA_BLOCK, jnp.int32).astype(F32)
    outs = []
    heads = range(ATT_HEADS)
    zs = [_dot_nt(k_ref[:, h * HEAD_PAD:(h + 1) * HEAD_PAD], q_ref[h].astype(BF16)) for h in heads]
    zs = [jnp.where(causal, z, NEG_BIG) for z in zs]
    ms = [jnp.max(z, axis=0, keepdims=True) for z in zs]
    ps = [jnp.exp(zs[h] - ms[h]) for h in heads]
    ls = [jnp.sum(p, axis=0, keepdims=True) for p in ps]
    os_ = [_dot(vt_ref[h * HEAD_DIM:(h + 1) * HEAD_DIM, :], ps[h].astype(BF16)) for h in heads]
    for h in heads:
        l0, o0 = ls[h], os_[h]
        m0 = ms[h] + slope_ref[h:h + 1, :] * own_shift
        parts = []
        for r in range(MOBA_TOPK):
            gt = g_ref[r, h].T
            valid = sel_ref[r, h:h + 1, :] >= 0
            parts.append((jnp.where(valid, gt[HEAD_DIM:HEAD_DIM + 1, :], NEG_BIG),
                          jnp.where(valid, gt[HEAD_DIM + 1:HEAD_DIM + 2, :], 0.0),
                          jnp.where(valid, gt[:HEAD_DIM, :], 0.0)))
        m = m0
        for mr, _, _ in parts:
            m = jnp.maximum(m, mr)
        w = jnp.exp(m0 - m)
        num, den = w * o0, w * l0
        for mr, lr, orr in parts:
            w = jnp.exp(mr - m)
            num, den = num + w * orr, den + w * lr
        outs.append(num / den)
    o_ref[...] = jnp.concatenate(outs, axis=0).T.astype(BF16)


def _combine(qhm, ka, vt, sel, g4, slope_tab, nb):
    n = ka.shape[0]
    return pl.pallas_call(
        functools.partial(_combine_kernel, nb),
        grid=(n // MOBA_BLOCK,),
        in_specs=[
            pl.BlockSpec((ATT_HEADS, MOBA_BLOCK, HEAD_PAD), lambda i: (0, i, 0)),
            pl.BlockSpec((MOBA_BLOCK, QK_PAD), lambda i: (i, 0)),
            pl.BlockSpec((ATT_WIDTH, MOBA_BLOCK), lambda i: (0, i)),
            pl.BlockSpec((MOBA_TOPK, ATT_HEADS, MOBA_BLOCK), lambda i: (0, 0, i)),
            pl.BlockSpec((MOBA_TOPK, ATT_HEADS, MOBA_BLOCK, HEAD_PAD), lambda i: (0, 0, i, 0)),
            pl.BlockSpec(slope_tab.shape, lambda i: (0, 0)),
        ],
        out_specs=pl.BlockSpec((MOBA_BLOCK, ATT_WIDTH), lambda i: (i, 0)),
        out_shape=jax.ShapeDtypeStruct((n, ATT_WIDTH), BF16),
        compiler_params=pltpu.CompilerParams(vmem_limit_bytes=VMEM_LIMIT),
        name="moba_own_combine",
    )(qhm, ka, vt, sel, g4, slope_tab)


def _merge_kernel(x_ref, bra_ref, ys_ref, brc_ref, wg_ref, wa_ref, wb_ref, wc_ref, gw_ref, gb_ref,
                  wo_ref, lg_ref, lb_ref, o_ref):
    x = x_ref[...]
    xb = x.astype(BF16)

    def gate(k):
        return jax.nn.sigmoid(_dot(xb, wg_ref[:, k * D_MODEL:(k + 1) * D_MODEL]))

    merged = gate(0) * _dot(bra_ref[...], wa_ref[...])
    ys = jnp.concatenate([ys_ref[c].T for c in range(ys_ref.shape[0])], axis=0)
    brb = ys * jax.nn.sigmoid(_dot(ys.astype(BF16), gw_ref[...]) + gb_ref[...])
    merged = merged + gate(1) * _dot(brb.astype(BF16), wb_ref[...])
    merged = merged + gate(2) * _dot(brc_ref[...], wc_ref[...])
    mix = _dot(merged.astype(BF16), wo_ref[...])
    o_ref[...] = _layer_norm(DN_ALPHA * x + mix, lg_ref[...], lb_ref[...])


def _merge(x2, bra, ys, brc, wg, wa, wb, wc, gw, gb, wo, lg, lb, tm, first_tile):
    n = bra.shape[0]
    const = lambda i: (0, 0)
    tile = lambda w: pl.BlockSpec((tm, w), lambda i: (i, 0))
    full = lambda a: pl.BlockSpec(a.shape, const)
    return pl.pallas_call(
        _merge_kernel,
        grid=(n // tm,),
        in_specs=[pl.BlockSpec((tm, D_MODEL), lambda i: (first_tile + i, 0)), tile(SGU_WIDTH),
                  pl.BlockSpec((tm // S5_CHUNK, SSM_WIDTH, S5_CHUNK), lambda i: (i, 0, 0)),
                  tile(ATT_WIDTH),
                  full(wg), full(wa), full(wb), full(wc), full(gw), full(gb), full(wo), full(lg), full(lb)],
        out_specs=tile(D_MODEL),
        out_shape=jax.ShapeDtypeStruct((n, D_MODEL), F32),
        compiler_params=pltpu.CompilerParams(vmem_limit_bytes=VMEM_LIMIT),
        name="merge_ln",
    )(x2, bra, ys, brc, wg, wa, wb, wc, gw, gb, wo, lg, lb)


FF_CHUNK = D_FF // 2


def _ffn_kernel(x_ref, w1_ref, w3_ref, w2_ref, lg_ref, lb_ref, *rest):
    o_ref = rest[-1]
    x = x_ref[...]
    xb = x.astype(BF16)
    acc = None
    for c in range(D_FF // FF_CHUNK):
        cs = slice(c * FF_CHUNK, (c + 1) * FF_CHUNK)
        h = (jax.nn.silu(_dot(xb, w1_ref[:, cs])) * _dot(xb, w3_ref[:, cs])).astype(BF16)
        part = _dot(h, w2_ref[cs, :])
        acc = part if acc is None else acc + part
    o_ref[...] = _layer_norm(DN_ALPHA * x + acc, lg_ref[...], lb_ref[...])


def _ffn(x2, w1, w3, w2, lg, lb, tm, out_rows, first_tile, dest):
    n = x2.shape[0]
    const = lambda i: (0, 0)
    full = lambda a: pl.BlockSpec(a.shape, const)
    in_specs = [pl.BlockSpec((tm, D_MODEL), lambda i: (i, 0)),
                full(w1), full(w3), full(w2), full(lg), full(lb)]
    args = [x2, w1, w3, w2, lg, lb]
    aliases = {}
    if dest is not None:
        in_specs.append(pl.BlockSpec(memory_space=pl.ANY))
        aliases = {len(args): 0}
        args.append(dest)
    return pl.pallas_call(
        _ffn_kernel,
        grid=(n // tm,),
        in_specs=in_specs,
        out_specs=pl.BlockSpec((tm, D_MODEL), lambda i: (first_tile + i, 0)),
        out_shape=jax.ShapeDtypeStruct((out_rows, D_MODEL), F32),
        input_output_aliases=aliases,
        compiler_params=pltpu.CompilerParams(vmem_limit_bytes=VMEM_LIMIT),
        name="ffn_ln",
    )(*args)


def _alibi_extras():
    slopes = 2.0 ** (-8.0 * jnp.arange(1, ATT_HEADS + 1, dtype=F32) / ATT_HEADS)
    row = jnp.arange(MOBA_BLOCK, dtype=F32)
    eq = jnp.zeros((MOBA_BLOCK, ATT_HEADS, HEAD_PAD), F32)
    eq = eq.at[:, :, Q_LANE_ONE].set(1.0).at[:, :, Q_LANE_ROW].set(row[:, None])
    ek = jnp.zeros((MOBA_BLOCK, ATT_HEADS, HEAD_PAD), F32)
    ek = ek.at[:, :, Q_LANE_ONE].set(row[:, None] * slopes[None, :])
    ek = ek.at[:, :, Q_LANE_BLK].set(-slopes[None, :] * MOBA_BLOCK).at[:, :, Q_LANE_ROW].set(-slopes[None, :])
    slope_tab = jnp.broadcast_to(slopes[:, None], (ATT_HEADS, MOBA_BLOCK))
    return eq.reshape(MOBA_BLOCK, QK_PAD), ek.reshape(MOBA_BLOCK, QK_PAD), slope_tab


def kernel(x, w_in, sgu_ln_g, sgu_ln_b, sgu_w, sgu_b, ssm_lambda_re, ssm_lambda_im, ssm_log_dt,
           ssm_b_re, ssm_b_im, ssm_c_re, ssm_c_im, ssm_d, glu_w, glu_b, w_branch_a, w_branch_b,
           w_branch_c, w_out, ln1_g, ln1_b, ffn_w1, ffn_w3, ffn_w2, ln2_g, ln2_b):
    bsz, seq, _ = x.shape
    n = bsz * seq
    nb = seq // MOBA_BLOCK
    cpb = seq // S5_CHUNK
    tm = 512 if n % 512 == 0 else MOBA_BLOCK
    eq, ek, slope_tab = _alibi_extras()
    scale = HEAD_DIM ** -0.5
    cap = (MOBA_TOPK + 1) * seq
    tpb = cap // MOBA_BLOCK
    trash = ATT_HEADS * cap
    o_q = 2 * SGU_WIDTH + SSM_WIDTH
    o_g = o_q + 3 * ATT_WIDTH

    xs = [(x.reshape(n, D_MODEL), b * nb) for b in range(bsz)]
    out = None
    for l in range(DEPTH):
        last = l == DEPTH - 1
        wl = w_in[l]
        wz = wl[:, :2 * SGU_WIDTH].astype(BF16)
        wst = wl[:, 2 * SGU_WIDTH:o_q].T.astype(BF16)
        wq = (wl[:, o_q:o_q + ATT_WIDTH] * scale).astype(BF16)
        wk = wl[:, o_q + ATT_WIDTH:o_q + 2 * ATT_WIDTH].astype(BF16)
        wvt = wl[:, o_q + 2 * ATT_WIDTH:o_g].T.astype(BF16)
        wg = wl[:, o_g:].astype(BF16)

        proj = [_inproj(xs[b][0], wz, wst, wq, wk, wvt, eq, ek, sgu_ln_g[l][None, :],
                        sgu_ln_b[l][None, :], sgu_w[l].astype(BF16), sgu_b[l].T, nb, xs[b][1])
                for b in range(bsz)]

        s5_params = _s5_params(
            ssm_lambda_re[l], ssm_lambda_im[l], ssm_log_dt[l], ssm_b_re[l], ssm_b_im[l],
            ssm_c_re[l], ssm_c_im[l])
        dtab = jnp.broadcast_to(ssm_d[l].reshape(SSM_GROUPS, SSM_GROUP, 1),
                                (SSM_GROUPS, SSM_GROUP, S5_CHUNK))
        yss = _s5([p[1] for p in proj], s5_params, dtab, cpb)

        merge_w = (wg, w_branch_a[l].astype(BF16), w_branch_b[l].astype(BF16), w_branch_c[l].astype(BF16),
                   glu_w[l].astype(BF16), glu_b[l][None, :], w_out[l].astype(BF16),
                   ln1_g[l][None, :], ln1_b[l][None, :])
        ffn_w = (ffn_w1[l].astype(BF16), ffn_w3[l].astype(BF16), ffn_w2[l].astype(BF16),
                 ln2_g[l][None, :], ln2_b[l][None, :])
        for b in range(bsz):
            bra, _, qhm, ka, vt, sel, rank, cnt = proj[b]
            start, tmap, nused = _plan(cnt, nb, tpb)
            dst = _pos(sel[0], rank[0], start, nb, cap, trash).reshape(MOBA_TOPK, ATT_HEADS * seq)
            qs = _sc_scatter_rows(qhm.reshape(ATT_HEADS * seq, HEAD_PAD), dst, trash + MOBA_BLOCK)
            part = _routed(qs, ka, vt[0], slope_tab, tmap[:, 0, :], nused[:, 0, 0], 1, seq, tpb)
            g = _sc_gather_rows(part, dst.reshape(1, MOBA_TOPK * ATT_HEADS * seq))
            brc = _combine(qhm[0], ka, vt[0], sel[0],
                           g.reshape(MOBA_TOPK, ATT_HEADS, seq, HEAD_PAD), slope_tab, nb)
            x1 = _merge(xs[b][0], bra, yss[b], brc, *merge_w, tm, xs[b][1] * MOBA_BLOCK // tm)
            if last:
                out = _ffn(x1, *ffn_w, tm, n, b * (seq // tm), out)
            else:
                xs[b] = (_ffn(x1, *ffn_w, tm, seq, 0, None), 0)
    return out.reshape(bsz, seq, D_MODEL)
```
